```python
import math
import jax
import jax.numpy as jnp
from jax import lax
import numpy as np

D_MODEL = 1024
BATCH = 1
SEQ = 16384
DEPTH = 2
DEC_BATCH = 32
DEC_SEQ = 1
PAST_LEN = 16384
PAGE_SIZE = 128

A_HEADS = 4
A_QK = 64
A_V = 2 * A_QK
ROPE_DIM = A_QK // 4
ROPE_THETA = 500000.0
Q_BLOCK = 128
G_HEADS = 4
G_QK = 64
G_V = 128
G_RANK = 16
G_NORMALIZER = 16.0
R_HEADS = 4
R_QK = 64
R_V = 128
R_ANGLE_BASE = 10000.0
CHUNK = 64
BRANCH_W = 512
D_FF = 2816
N_EXPERTS = 8
TOP_K = 2
D_FF_EXPERT = 3584
NORM_EPS = 1e-5
DEEPNORM_ALPHA = (2 * DEPTH) ** 0.25
DEEPNORM_BETA = (8 * DEPTH) ** -0.25
N_DENSE = (DEPTH + 1) // 2
N_MOE = DEPTH // 2
IN_SIZES = (A_HEADS * 2 * A_QK, A_HEADS * 2 * A_QK, A_HEADS * A_V,
            G_HEADS * G_QK, G_HEADS * G_QK, G_HEADS * G_V, G_RANK, G_HEADS * G_V,
            R_HEADS * R_QK, R_HEADS * R_QK, R_HEADS * R_V, R_HEADS * R_V,
            D_MODEL, D_MODEL, D_MODEL)
IN_W = sum(IN_SIZES)
IN_SPLITS = [int(s) for s in np.cumsum(IN_SIZES)[:-1]]

kernel_name = 'hybrid_diffattn_gla_retention_deepnorm_step'

F32 = jnp.float32


def _block_size(n, cap):
    c = min(n, cap)
    while n % c:
        c -= 1
    return c


def layer_norm(x, g, b):
    xf = x.astype(F32)
    mu = jnp.mean(xf, axis=-1, keepdims=True)
    var = jnp.mean(jnp.square(xf - mu), axis=-1, keepdims=True)
    return ((xf - mu) * lax.rsqrt(var + NORM_EPS) * g + b).astype(x.dtype)


def rms_norm(x, w):
    xf = x.astype(F32)
    return (xf * lax.rsqrt(jnp.mean(jnp.square(xf), axis=-1, keepdims=True) + NORM_EPS) * w).astype(x.dtype)


def partial_rope(x, pos):
    half = ROPE_DIM // 2
    inv = ROPE_THETA ** (-jnp.arange(half, dtype=F32) * 2.0 / ROPE_DIM)
    ang = pos.astype(F32)[:, None] * inv[None, :]
    cos = jnp.cos(ang)[None, :, None, None, :]
    sin = jnp.sin(ang)[None, :, None, None, :]
    xf = x.astype(F32)
    x1 = xf[..., :half]
    x2 = xf[..., half:ROPE_DIM]
    out = jnp.concatenate([x1 * cos - x2 * sin, x2 * cos + x1 * sin, xf[..., ROPE_DIM:]], axis=-1)
    return out.astype(x.dtype)


def retention_rotate(x, pos):
    inv = 1.0 / (R_ANGLE_BASE ** jnp.linspace(0.0, 1.0, R_QK // 2, dtype=F32))
    ang = pos.astype(F32)[:, None] * inv[None, :]
    cos = jnp.cos(ang)[None, :, None, :]
    sin = jnp.sin(ang)[None, :, None, :]
    xf = x.astype(F32).reshape(x.shape[:-1] + (R_QK // 2, 2))
    xe = xf[..., 0]
    xo = xf[..., 1]
    out = jnp.stack([xe * cos - xo * sin, xo * cos + xe * sin], axis=-1)
    return out.reshape(x.shape).astype(x.dtype)


def diff_attention(q, k, v, q_pos, k_pos, lam):
    B, Tq = q.shape[0], q.shape[1]
    blk = _block_size(Tq, Q_BLOCK)
    nb = Tq // blk
    qb = jnp.moveaxis(q.reshape((B, nb, blk) + q.shape[2:]), 1, 0)
    pb = q_pos.reshape(nb, blk)
    scale = A_QK ** -0.5

    def one_block(args):
        qi, pi = args
        s = jnp.einsum('bqhcd,bkhcd->bhcqk', qi, k).astype(F32) * scale
        mask = k_pos[None, :] <= pi[:, None]
        s = jnp.where(mask[None, None, None], s, -jnp.inf)
        p = jax.nn.softmax(s, axis=-1)
        w = p[:, :, 0] - lam * p[:, :, 1]
        return jnp.einsum('bhqk,bkhe->bqhe', w.astype(v.dtype), v)

    o = lax.map(one_block, (qb, pb))
    return jnp.moveaxis(o, 0, 1).reshape(B, Tq, q.shape[2], A_V)


def gla_scan(q, k, v, g, s0):
    B, T, H, dk = q.shape
    dv = v.shape[-1]
    c = _block_size(T, CHUNK)
    n = T // c
    def chunks(a):
        return jnp.moveaxis(a.astype(F32).reshape((B, n, c) + a.shape[2:]), 1, 0)
    mask = jnp.tril(jnp.ones((c, c), dtype=bool))[None, :, :, None, None]

    def step(S, inp):
        qc, kc, vc, gc = inp
        G = jnp.cumsum(gc, axis=1)
        inter = jnp.einsum('bthk,bhkv->bthv', qc * jnp.exp(G), S)
        decay = jnp.exp(jnp.where(mask, G[:, :, None] - G[:, None, :], -jnp.inf))
        A = jnp.sum(qc[:, :, None] * kc[:, None, :] * decay, axis=-1)
        intra = jnp.einsum('btjh,bjhv->bthv', A, vc)
        Glast = G[:, -1]
        S_new = jnp.exp(Glast)[..., None] * S + jnp.einsum(
            'bjhk,bjhv->bhkv', kc * jnp.exp(Glast[:, None] - G), vc)
        return S_new, inter + intra

    S, o = lax.scan(step, s0.astype(F32), (chunks(q), chunks(k), chunks(v), chunks(g)))
    o = jnp.moveaxis(o, 0, 1).reshape(B, T, H, dv)
    return o.astype(v.dtype), S.astype(s0.dtype)


def retention_scan(q, k, v, log_gamma, s0):
    B, T, H, dk = q.shape
    dv = v.shape[-1]
    c = _block_size(T, CHUNK)
    n = T // c
    def chunks(a):
        return jnp.moveaxis(a.astype(F32).reshape((B, n, c) + a.shape[2:]), 1, 0)
    idx = jnp.arange(c, dtype=F32)
    rel = idx[:, None] - idx[None, :]
    lg = log_gamma[:, None, None]
    D = jnp.where(rel[None] >= 0, jnp.exp(jnp.maximum(rel, 0.0)[None] * lg), 0.0)
    cross = jnp.exp((idx + 1.0)[None, :] * log_gamma[:, None])
    tail = jnp.exp((c - 1.0 - idx)[None, :] * log_gamma[:, None])
    gamma_c = jnp.exp(c * log_gamma)[None, :, None, None]

    def step(S, inp):
        qc, kc, vc = inp
        scores = jnp.einsum('bthd,bjhd->bhtj', qc, kc) * D[None]
        intra = jnp.einsum('bhtj,bjhv->bthv', scores, vc)
        inter = jnp.einsum('bthk,bhkv->bthv', qc, S) * cross.T[None, :, :, None]
        S_new = gamma_c * S + jnp.einsum('bjhk,hj,bjhv->bhkv', kc, tail, vc)
        return S_new, intra + inter

    S, o = lax.scan(step, s0.astype(F32), (chunks(q), chunks(k), chunks(v)))
    o = jnp.moveaxis(o, 0, 1).reshape(B, T, H, dv)
    return o.astype(v.dtype), S.astype(s0.dtype)


def swiglu(x, wg, wu, wd):
    return (jax.nn.silu(x @ wg) * (x @ wu)) @ wd


def moe_swiglu(x2d, w_router, wg, wu, wd):
    logits = (x2d @ w_router).astype(F32)
    top_v, top_i = lax.top_k(logits, TOP_K)
    wts = jax.nn.softmax(top_v, axis=-1)
    comb = jnp.sum(jax.nn.one_hot(top_i, N_EXPERTS, dtype=F32) * wts[..., None], axis=1)
    comb = comb.astype(x2d.dtype)
    out = jnp.zeros_like(x2d)
    for e in range(N_EXPERTS):
        out = out + comb[:, e:e + 1] * swiglu(x2d, wg[e], wu[e], wd[e])
    return out


def decoder_layer(x, pos, l, k_past, v_past, s_gla0, s_ret0, W):
    B, T, _ = x.shape
    h = x @ W['w_in'][l]
    (qa, ka, va, qg, kg, vg, lrg, rg, qr, kr, vr, gr, za, zb, zc) = jnp.split(h, IN_SPLITS, axis=-1)

    qa = partial_rope(qa.reshape(B, T, A_HEADS, 2, A_QK), pos)
    ka = partial_rope(ka.reshape(B, T, A_HEADS, 2, A_QK), pos)
    va = va.reshape(B, T, A_HEADS, A_V)
    if k_past is None:
        k_all, v_all, k_pos = ka, va, pos
    else:
        k_all = jnp.concatenate([k_past, ka], axis=1)
        v_all = jnp.concatenate([v_past, va], axis=1)
        k_pos = jnp.concatenate([jnp.arange(k_past.shape[1], dtype=pos.dtype), pos])
    lam_init = 0.8 - 0.6 * math.exp(-0.3 * l)
    lam = (jnp.exp(jnp.sum(W['lam_q1'][l].astype(F32) * W['lam_k1'][l].astype(F32)))
           - jnp.exp(jnp.sum(W['lam_q2'][l].astype(F32) * W['lam_k2'][l].astype(F32))) + lam_init)
    oa = diff_attention(qa, k_all, v_all, pos, k_pos, lam)
    oa = rms_norm(oa, W['subln_w'][l]) * (1.0 - lam_init)

    qg = qg.reshape(B, T, G_HEADS, G_QK) * (G_QK ** -0.5)
    kg = kg.reshape(B, T, G_HEADS, G_QK)
    vg = vg.reshape(B, T, G_HEADS, G_V)
    gk = lrg @ W['w_gla_up'][l] + W['b_gla_up'][l]
    gg = (jax.nn.log_sigmoid(gk.astype(F32)) / G_NORMALIZER).reshape(B, T, G_HEADS, G_QK)
    og, s_gla = gla_scan(qg, kg, vg, gg, s_gla0)
    og = rms_norm(og, W['gla_norm_w'][l]) * jax.nn.silu(rg.reshape(B, T, G_HEADS, G_V))

    qr = retention_rotate(qr.reshape(B, T, R_HEADS, R_QK), pos)
    kr = retention_rotate(kr.reshape(B, T, R_HEADS, R_QK), pos) * (R_QK ** -0.5)
    vr = vr.reshape(B, T, R_HEADS, R_V)
    log_gamma = jnp.log(1.0 - jnp.exp2(-5.0 - jnp.arange(R_HEADS, dtype=F32)))
    orr, s_ret = retention_scan(qr, kr, vr, log_gamma, s_ret0)
    orr = rms_norm(orr, W['ret_norm_w'][l]) * jax.nn.silu(gr.reshape(B, T, R_HEADS, R_V))

    merged = (jax.nn.sigmoid(za) * (oa.reshape(B, T, BRANCH_W) @ W['w_pa'][l])
              + jax.nn.sigmoid(zb) * (og.reshape(B, T, BRANCH_W) @ W['w_pb'][l])
              + jax.nn.sigmoid(zc) * (orr.reshape(B, T, BRANCH_W) @ W['w_pc'][l]))
    x = layer_norm(DEEPNORM_ALPHA * x + merged @ W['w_out'][l], W['ln1_g'][l], W['ln1_b'][l])

    if l % 2 == 0:
        i = l // 2
        f = swiglu(x, W['w_ff_gate'][i], W['w_ff_up'][i], W['w_ff_down'][i])
    else:
        i = l // 2
        f = moe_swiglu(x.reshape(B * T, D_MODEL), W['w_router'][i], W['w_exp_gate'][i],
                       W['w_exp_up'][i], W['w_exp_down'][i]).reshape(B, T, D_MODEL)
    x = layer_norm(DEEPNORM_ALPHA * x + f, W['ln2_g'][l], W['ln2_b'][l])
    return x, ka.reshape(B, T, A_HEADS, 2 * A_QK), va, s_gla, s_ret


def setup_inputs(seed: int = 0) -> dict:
    key = jax.random.key(seed)
    ks = jax.random.split(key, 40)
    def nrm(k, shape, scale):
        return jax.random.normal(k, shape, F32) * scale
    n_pages = PAST_LEN // PAGE_SIZE
    used = DEC_BATCH * n_pages
    n_pool = used + max(1, used // 4)
    page_table = jax.random.permutation(ks[0], n_pool)[:used].reshape(DEC_BATCH, n_pages).astype(jnp.int32)
    return {
        'x_prompt': nrm(ks[1], (BATCH, SEQ, D_MODEL), 1.0),
        'x_sample': nrm(ks[2], (DEC_BATCH, DEC_SEQ, D_MODEL), 1.0),
        'cache_k': nrm(ks[3], (DEPTH, n_pool, PAGE_SIZE, A_HEADS, 2 * A_QK), 1.0),
        'cache_v': nrm(ks[4], (DEPTH, n_pool, PAGE_SIZE, A_HEADS, A_V), 1.0),
        'state_gla': nrm(ks[5], (DEPTH, DEC_BATCH, G_HEADS, G_QK, G_V), 0.5),
        'state_ret': nrm(ks[6], (DEPTH, DEC_BATCH, R_HEADS, R_QK, R_V), 0.5),
        'page_table': page_table,
        'w_in': nrm(ks[7], (DEPTH, D_MODEL, IN_W), D_MODEL ** -0.5),
        'lam_q1': nrm(ks[8], (DEPTH, A_QK), 0.1),
        'lam_k1': nrm(ks[9], (DEPTH, A_QK), 0.1),
        'lam_q2': nrm(ks[10], (DEPTH, A_QK), 0.1),
        'lam_k2': nrm(ks[11], (DEPTH, A_QK), 0.1),
        'subln_w': 1.0 + nrm(ks[12], (DEPTH, A_V), 0.02),
        'w_gla_up': nrm(ks[13], (DEPTH, G_RANK, G_HEADS * G_QK), G_RANK ** -0.5),
        'b_gla_up': nrm(ks[14], (DEPTH, G_HEADS * G_QK), 0.1),
        'gla_norm_w': 1.0 + nrm(ks[15], (DEPTH, G_V), 0.02),
        'ret_norm_w': 1.0 + nrm(ks[16], (DEPTH, R_V), 0.02),
        'w_pa': nrm(ks[17], (DEPTH, BRANCH_W, D_MODEL), BRANCH_W ** -0.5 * DEEPNORM_BETA),
        'w_pb': nrm(ks[18], (DEPTH, BRANCH_W, D_MODEL), BRANCH_W ** -0.5 * DEEPNORM_BETA),
        'w_pc': nrm(ks[19], (DEPTH, BRANCH_W, D_MODEL), BRANCH_W ** -0.5 * DEEPNORM_BETA),
        'w_out': nrm(ks[20], (DEPTH, D_MODEL, D_MODEL), D_MODEL ** -0.5 * DEEPNORM_BETA),
        'ln1_g': 1.0 + nrm(ks[21], (DEPTH, D_MODEL), 0.02),
        'ln1_b': nrm(ks[22], (DEPTH, D_MODEL), 0.02),
        'w_ff_gate': nrm(ks[23], (N_DENSE, D_MODEL, D_FF), D_MODEL ** -0.5),
        'w_ff_up': nrm(ks[24], (N_DENSE, D_MODEL, D_FF), D_MODEL ** -0.5),
        'w_ff_down': nrm(ks[25], (N_DENSE, D_FF, D_MODEL), D_FF ** -0.5 * DEEPNORM_BETA),
        'w_router': nrm(ks[26], (N_MOE, D_MODEL, N_EXPERTS), D_MODEL ** -0.5),
        'w_exp_gate': nrm(ks[27], (N_MOE, N_EXPERTS, D_MODEL, D_FF_EXPERT), D_MODEL ** -0.5),
        'w_exp_up': nrm(ks[28], (N_MOE, N_EXPERTS, D_MODEL, D_FF_EXPERT), D_MODEL ** -0.5),
        'w_exp_down': nrm(ks[29], (N_MOE, N_EXPERTS, D_FF_EXPERT, D_MODEL), D_FF_EXPERT ** -0.5 * DEEPNORM_BETA),
        'ln2_g': 1.0 + nrm(ks[30], (DEPTH, D_MODEL), 0.02),
        'ln2_b': nrm(ks[31], (DEPTH, D_MODEL), 0.02),
    }


def reference(x_prompt, x_sample, cache_k, cache_v, state_gla, state_ret, page_table,
              w_in, lam_q1, lam_k1, lam_q2, lam_k2, subln_w, w_gla_up, b_gla_up,
              gla_norm_w, ret_norm_w, w_pa, w_pb, w_pc, w_out, ln1_g, ln1_b,
              w_ff_gate, w_ff_up, w_ff_down, w_router, w_exp_gate, w_exp_up, w_exp_down,
              ln2_g, ln2_b):
    W = {'w_in': w_in, 'lam_q1': lam_q1, 'lam_k1': lam_k1, 'lam_q2': lam_q2, 'lam_k2': lam_k2,
         'subln_w': subln_w, 'w_gla_up': w_gla_up, 'b_gla_up': b_gla_up,
         'gla_norm_w': gla_norm_w, 'ret_norm_w': ret_norm_w, 'w_pa': w_pa, 'w_pb': w_pb,
         'w_pc': w_pc, 'w_out': w_out, 'ln1_g': ln1_g, 'ln1_b': ln1_b,
         'w_ff_gate': w_ff_gate, 'w_ff_up': w_ff_up, 'w_ff_down': w_ff_down,
         'w_router': w_router, 'w_exp_gate': w_exp_gate, 'w_exp_up': w_exp_up,
         'w_exp_down': w_exp_down, 'ln2_g': ln2_g, 'ln2_b': ln2_b}
    Bp, Tp, _ = x_prompt.shape
    Bs, Ts, _ = x_sample.shape
    past_len = page_table.shape[1] * cache_k.shape[2]
    pos_p = jnp.arange(Tp, dtype=jnp.int32)
    pos_s = past_len + jnp.arange(Ts, dtype=jnp.int32)
    zero_gla = jnp.zeros((Bp, G_HEADS, G_QK, G_V), x_prompt.dtype)
    zero_ret = jnp.zeros((Bp, R_HEADS, R_QK, R_V), x_prompt.dtype)

    yp, ys = x_prompt, x_sample
    kp_l, vp_l, gp_l, rp_l = [], [], [], []
    ks_l, vs_l, gs_l, rs_l = [], [], [], []
    for l in range(DEPTH):
        yp, kp, vp, gp, rp = decoder_layer(yp, pos_p, l, None, None, zero_gla, zero_ret, W)
        k_past = cache_k[l, page_table].reshape(Bs, past_len, A_HEADS, 2, A_QK)
        v_past = cache_v[l, page_table].reshape(Bs, past_len, A_HEADS, A_V)
        ys, kn, vn, gn, rn = decoder_layer(ys, pos_s, l, k_past, v_past, state_gla[l], state_ret[l], W)
        kp_l.append(kp); vp_l.append(vp); gp_l.append(gp); rp_l.append(rp)
        ks_l.append(kn); vs_l.append(vn); gs_l.append(gn); rs_l.append(rn)

    new_k_prompt = jnp.stack(kp_l)
    new_v_prompt = jnp.stack(vp_l)
    new_gla_prompt = jnp.stack(gp_l)
    new_ret_prompt = jnp.stack(rp_l)
    new_k_sample = jnp.stack(ks_l)
    new_v_sample = jnp.stack(vs_l)
    new_gla_sample = jnp.stack(gs_l)
    new_ret_sample = jnp.stack(rs_l)
    return (yp, ys, new_k_prompt, new_v_prompt, new_gla_prompt, new_ret_prompt,
            new_k_sample, new_v_sample, new_gla_sample, new_ret_sample)
```

```python
import functools
import math

import numpy as np
import jax
import jax.numpy as jnp
from jax import lax
from jax.experimental import pallas as pl
from jax.experimental.pallas import tpu as pltpu

F32 = jnp.float32
BF16 = jnp.bfloat16

D_MODEL = 1024
A_HEADS = 4
A_QK = 64
ROPE_DIM = 16
ROPE_THETA = 500000.0
G_HEADS = 4
G_QK = 64
G_V = 128
G_RANK = 16
G_NORMALIZER = 16.0
R_HEADS = 4
R_QK = 64
R_ANGLE_BASE = 10000.0
CHUNK = 64
N_EXPERTS = 8
NORM_EPS = 1e-5
NEG_BIG = -1e30

LANES = 128
VMEM_LIMIT = 56 * 1024 * 1024

ZA, ZB, ZC, QA, KA, VA, VG, RG, VR, GR, QG, KG, QR, KR = 0, 8, 16, 24, 28, 32, 36, 40, 44, 48, 52, 54, 56, 58
PACK_ORDER = ((4624, 7696), (0, 1536), (2048, 2560), (2576, 3088), (3600, 4112), (4112, 4624),
              (1536, 2048), (3088, 3600))
LRG_OFF = 2560


def _cparams(sem):
    return pltpu.CompilerParams(dimension_semantics=sem, vmem_limit_bytes=VMEM_LIMIT)


def _tile(n, cap):
    c = min(n, cap)
    while n % c:
        c -= 1
    return c


def _layer_norm(y, g, b):
    mu = jnp.mean(y, axis=-1, keepdims=True)
    d = y - mu
    var = jnp.mean(d * d, axis=-1, keepdims=True)
    return d * lax.rsqrt(var + NORM_EPS) * g + b


def _rms(o, w):
    return o * lax.rsqrt(jnp.mean(o * o, axis=-1, keepdims=True) + NORM_EPS) * w


def _silu(x):
    return x * (1.0 / (1.0 + jnp.exp(-x)))


def _sigmoid(x):
    return 1.0 / (1.0 + jnp.exp(-x))


def _log_sigmoid(x):
    return jnp.minimum(x, 0.0) - jnp.log(1.0 + jnp.exp(-jnp.abs(x)))


def _split3(x):
    a = x.astype(BF16)
    r = x - a.astype(F32)
    b = r.astype(BF16)
    c = (r - b.astype(F32)).astype(BF16)
    return a, b, c


def _mm_kernel(x_ref, w_ref, o_ref):
    o_ref[...] = jnp.dot(x_ref[...].astype(BF16), w_ref[...],
                         preferred_element_type=F32).astype(o_ref.dtype)


def _mm_bias_kernel(x_ref, w_ref, b_ref, o_ref):
    o_ref[...] = (jnp.dot(x_ref[...].astype(BF16), w_ref[...],
                          preferred_element_type=F32) + b_ref[...]).astype(o_ref.dtype)


def matmul(x, w, bias=None, out_dtype=F32, tm_cap=1024, tn_cap=768):
    m, k = x.shape
    n = w.shape[1]
    tm, tn = _tile(m, tm_cap), _tile(n, tn_cap)
    in_specs = [pl.BlockSpec((tm, k), lambda i, j: (i, 0)),
                pl.BlockSpec((k, tn), lambda i, j: (0, j))]
    args = [x, w]
    kern = _mm_kernel
    if bias is not None:
        in_specs.append(pl.BlockSpec((1, tn), lambda i, j: (0, j)))
        args.append(bias)
        kern = _mm_bias_kernel
    return pl.pallas_call(
        kern, grid=(m // tm, n // tn), in_specs=in_specs,
        out_specs=pl.BlockSpec((tm, tn), lambda i, j: (i, j)),
        out_shape=jax.ShapeDtypeStruct((m, n), out_dtype),
        compiler_params=_cparams(("parallel", "parallel")))(*args)


def _rot_kernel(x_ref, c_ref, s1_ref, s2_ref, oq_ref, ok_ref, *, shift, q_scale, k_scale):
    c, s1, s2 = c_ref[...], s1_ref[...], s2_ref[...]
    nq = oq_ref.shape[1] // LANES
    nk = ok_ref.shape[1] // LANES
    for b in range(nq + nk):
        x = x_ref[:, b * LANES:(b + 1) * LANES]
        y = x * c + pltpu.roll(x, LANES - shift, 1) * s1 + pltpu.roll(x, shift, 1) * s2
        if b < nq:
            oq_ref[:, b * LANES:(b + 1) * LANES] = (y * q_scale).astype(oq_ref.dtype)
        else:
            ok_ref[:, (b - nq) * LANES:(b - nq + 1) * LANES] = (y * k_scale).astype(ok_ref.dtype)


def rotary(h, col_blk, width, tables, shift, q_scale, k_scale, q_dtype):
    m = h.shape[0]
    tm = _tile(m, 512)
    blk = col_blk * LANES // (2 * width)
    tspec = pl.BlockSpec((tm, LANES), lambda i: (i, 0))
    return pl.pallas_call(
        functools.partial(_rot_kernel, shift=shift, q_scale=q_scale, k_scale=k_scale),
        grid=(m // tm,),
        in_specs=[pl.BlockSpec((tm, 2 * width), lambda i: (i, blk)), tspec, tspec, tspec],
        out_specs=[pl.BlockSpec((tm, width), lambda i: (i, 0)),
                   pl.BlockSpec((tm, width), lambda i: (i, 0))],
        out_shape=[jax.ShapeDtypeStruct((m, width), q_dtype),
                   jax.ShapeDtypeStruct((m, width), F32)],
        compiler_params=_cparams(("parallel",)))(h, *tables)


def _rope_tables(pos):
    half = ROPE_DIM // 2
    inv = ROPE_THETA ** (-jnp.arange(half, dtype=F32) * 2.0 / ROPE_DIM)
    ang = pos.astype(F32)[:, None] * inv[None, :]
    cos, sin = jnp.cos(ang), jnp.sin(ang)
    lane = np.arange(LANES) % A_QK
    fi = lane % half
    cos_l, sin_l = cos[:, fi], sin[:, fi]
    in_rot = (lane < ROPE_DIM)[None, :]
    lo = (lane < half)[None, :]
    c = jnp.where(in_rot, cos_l, 1.0)
    s1 = jnp.where(lo, -sin_l, 0.0)
    s2 = jnp.where(in_rot & ~lo, sin_l, 0.0)
    return c, s1, s2


def _ret_tables(pos):
    inv = 1.0 / (R_ANGLE_BASE ** jnp.linspace(0.0, 1.0, R_QK // 2, dtype=F32))
    ang = pos.astype(F32)[:, None] * inv[None, :]
    cos, sin = jnp.cos(ang), jnp.sin(ang)
    lane = np.arange(LANES) % R_QK
    cos_l, sin_l = cos[:, lane // 2], sin[:, lane // 2]
    even = (lane % 2 == 0)[None, :]
    return cos_l, jnp.where(even, -sin_l, 0.0), jnp.where(even, 0.0, sin_l)


def _lam_from(lam_ref, lam_init):
    v = lam_ref[...]
    t1 = jnp.sum(v[0:1] * v[1:2], axis=-1, keepdims=True)
    t2 = jnp.sum(v[2:3] * v[3:4], axis=-1, keepdims=True)
    return jnp.exp(t1) - jnp.exp(t2) + lam_init


def _flash_kernel(lam_ref, q_ref, k_ref, v_ref, w_ref, o_ref, m_ref, l_ref, acc_ref, *, tq, tk, lam_init):
    i, j = pl.program_id(1), pl.program_id(2)

    @pl.when(j == 0)
    def _():
        m_ref[...] = jnp.full(m_ref.shape, NEG_BIG, F32)
        l_ref[...] = jnp.zeros(l_ref.shape, F32)
        acc_ref[...] = jnp.zeros(acc_ref.shape, F32)

    @pl.when(j * tk <= i * tq + tq - 1)
    def _():
        q = q_ref[...]
        k = k_ref[...].astype(BF16)
        v = v_ref[...].astype(BF16)
        lane = lax.broadcasted_iota(jnp.int32, q.shape, 1)
        zero = jnp.zeros_like(q)
        row = i * tq + lax.broadcasted_iota(jnp.int32, (tq, tk), 0)
        col = j * tk + lax.broadcasted_iota(jnp.int32, (tq, tk), 1)
        causal = col <= row
        for c in range(2):
            qc = jnp.where((lane < A_QK) if c == 0 else (lane >= A_QK), q, zero)
            s = lax.dot_general(qc, k, (((1,), (1,)), ((), ())), preferred_element_type=F32)
            s = jnp.where(causal, s, NEG_BIG)
            m_prev = m_ref[c]
            m_new = jnp.maximum(m_prev, jnp.max(s, axis=-1, keepdims=True))
            p = jnp.exp(s - m_new)
            alpha = jnp.exp(m_prev - m_new)
            l_ref[c] = alpha * l_ref[c] + jnp.sum(p, axis=-1, keepdims=True)
            acc_ref[c] = alpha * acc_ref[c] + jnp.dot(p.astype(BF16), v, preferred_element_type=F32)
            m_ref[c] = m_new

    @pl.when(j == pl.num_programs(2) - 1)
    def _():
        lam = _lam_from(lam_ref, lam_init)
        o = acc_ref[0] / l_ref[0] - lam * (acc_ref[1] / l_ref[1])
        o_ref[...] = _rms(o, w_ref[...]) * (1.0 - lam_init)


def flash_diff_attention(q, k, h, lamvec, subln_w, lam_init):
    t = q.shape[0]
    tq = tk = _tile(t, 512)
    nq, nk = t // tq, t // tk

    def kv_idx(off):
        return lambda hh, i, j: (jnp.minimum(j, (i * tq + tq - 1) // tk), off + hh)

    return pl.pallas_call(
        functools.partial(_flash_kernel, tq=tq, tk=tk, lam_init=lam_init),
        grid=(A_HEADS, nq, nk),
        in_specs=[pl.BlockSpec((8, LANES), lambda hh, i, j: (0, 0)),
                  pl.BlockSpec((tq, LANES), lambda hh, i, j: (i, hh)),
                  pl.BlockSpec((tk, LANES), kv_idx(0)),
                  pl.BlockSpec((tk, LANES), kv_idx(VA)),
                  pl.BlockSpec((1, LANES), lambda hh, i, j: (0, 0))],
        out_specs=pl.BlockSpec((tq, LANES), lambda hh, i, j: (i, hh)),
        out_shape=jax.ShapeDtypeStruct((t, A_HEADS * LANES), F32),
        scratch_shapes=[pltpu.VMEM((2, tq, 1), F32), pltpu.VMEM((2, tq, 1), F32),
                        pltpu.VMEM((2, tq, LANES), F32)],
        compiler_params=_cparams(("parallel", "parallel", "arbitrary")))(lamvec, q, k, h, subln_w)


def _decode_kernel(pt_ref, lam_ref, q_ref, kn_ref, vn_ref, w_ref, *rest, pp, lam_init):
    k_refs, v_refs = rest[:pp], rest[pp:2 * pp]
    o_ref, m_ref, l_ref, acc_ref = rest[2 * pp:]
    c = pl.program_id(1)
    width = A_HEADS * LANES
    rows = 2 * A_HEADS

    @pl.when(c == 0)
    def _():
        m_ref[...] = jnp.full(m_ref.shape, NEG_BIG, F32)
        l_ref[...] = jnp.zeros(l_ref.shape, F32)
        acc_ref[...] = jnp.zeros(acc_ref.shape, F32)

    q = q_ref[...]
    rid = lax.broadcasted_iota(jnp.int32, (rows, width), 0)
    lid = lax.broadcasted_iota(jnp.int32, (rows, width), 1)
    qbd = jnp.where(lid // A_QK == rid, jnp.broadcast_to(q, (rows, width)), 0.0)
    qb = qbd.astype(BF16)
    s = jnp.concatenate(
        [lax.dot_general(qb, k_refs[p][...].astype(BF16), (((1,), (1,)), ((), ())),
                         preferred_element_type=F32) for p in range(pp)], axis=-1)
    m_prev = m_ref[...]
    m_new = jnp.maximum(m_prev, jnp.max(s, axis=-1, keepdims=True))
    p_ = jnp.exp(s - m_new)
    alpha = jnp.exp(m_prev - m_new)
    l_new = alpha * l_ref[...] + jnp.sum(p_, axis=-1, keepdims=True)
    acc = alpha * acc_ref[...]
    for p in range(pp):
        acc = acc + jnp.dot(p_[:, p * LANES:(p + 1) * LANES].astype(BF16), v_refs[p][...].astype(BF16),
                            preferred_element_type=F32)
    m_ref[...] = m_new
    l_ref[...] = l_new
    acc_ref[...] = acc

    @pl.when(c == pl.num_programs(1) - 1)
    def _():
        kn = kn_ref[...].astype(BF16).astype(F32)
        vn = vn_ref[...].astype(BF16).astype(F32)
        s_self = jnp.sum(qb.astype(F32) * kn, axis=-1, keepdims=True)
        m_fin = jnp.maximum(m_new, s_self)
        a2 = jnp.exp(m_new - m_fin)
        p_self = jnp.exp(s_self - m_fin)
        l_fin = a2 * l_new + p_self
        acc_fin = a2 * acc + p_self.astype(BF16).astype(F32) * vn
        on = acc_fin / l_fin
        lam = _lam_from(lam_ref, lam_init)
        w = w_ref[...]
        for hh in range(A_HEADS):
            sl = slice(hh * LANES, (hh + 1) * LANES)
            o = on[2 * hh:2 * hh + 1, sl] - lam * on[2 * hh + 1:2 * hh + 2, sl]
            o_ref[:, sl] = _rms(o, w) * (1.0 - lam_init)


def decode_diff_attention(q, k_new, v_new, cache_k_l, cache_v_l, page_table, lamvec, subln_w, lam_init):
    b, n_pages = page_table.shape
    page = cache_k_l.shape[1]
    width = A_HEADS * LANES
    pp = _tile(n_pages, 8)

    def row(bb, c, pt):
        return (bb, 0, 0)

    def page_idx(p):
        return lambda bb, c, pt: (pt[bb, c * pp + p], 0, 0)

    row_spec = pl.BlockSpec((None, 1, width), row)
    page_specs = [pl.BlockSpec((None, page, width), page_idx(p)) for p in range(pp)]
    grid_spec = pltpu.PrefetchScalarGridSpec(
        num_scalar_prefetch=1, grid=(b, n_pages // pp),
        in_specs=[pl.BlockSpec((8, LANES), lambda bb, c, pt: (0, 0)), row_spec, row_spec, row_spec,
                  pl.BlockSpec((1, LANES), lambda bb, c, pt: (0, 0))] + page_specs + page_specs,
        out_specs=row_spec,
        scratch_shapes=[pltpu.VMEM((2 * A_HEADS, 1), F32), pltpu.VMEM((2 * A_HEADS, 1), F32),
                        pltpu.VMEM((2 * A_HEADS, width), F32)])
    r3 = lambda a: a.reshape(b, 1, width)
    out = pl.pallas_call(
        functools.partial(_decode_kernel, pp=pp, lam_init=lam_init),
        grid_spec=grid_spec,
        out_shape=jax.ShapeDtypeStruct((b, 1, width), F32),
        compiler_params=_cparams(("parallel", "arbitrary")))(
            page_table, lamvec, r3(q), r3(k_new), r3(v_new), subln_w,
            *([cache_k_l] * pp), *([cache_v_l] * pp))
    return out.reshape(b, width)


HK = G_HEADS * G_QK
HV = G_HEADS * G_V
GLA_LEVELS = (1, 2, 4, 8, 16, 32)


def _gla_constants():
    c = CHUNK
    t = np.arange(c)[:, None]
    i = np.arange(c)[None, :]
    mats = []
    for s in GLA_LEVELS[1:] + (c,):
        mats.append(((i // s == t // s) & (i <= t)).astype(np.float32))
    for s in GLA_LEVELS[1:] + (c,):
        mats.append(((i // s == t // s) & (i > t)).astype(np.float32))
    tri = np.concatenate(mats, axis=0)
    j = i
    level = np.full((c, c), -1, np.int32)
    level[t == j] = 0
    for n, s in enumerate(GLA_LEVELS):
        sel = (t // (2 * s) == j // (2 * s)) & (t % (2 * s) >= s) & (j % (2 * s) < s)
        level[sel] = n + 1
    level = np.tile(level, (G_HEADS, 1))
    headmask = (np.arange(HK)[None, :] // G_QK == np.arange(G_HEADS * c)[:, None] // c).astype(np.float32)
    return jnp.asarray(tri, BF16), jnp.asarray(level), jnp.asarray(headmask)


def _stack_heads(x, hm):
    return jnp.concatenate([x] * G_HEADS, axis=0) * hm


def _col_bcast(row, width):
    n = row.shape[1]
    eye = lax.broadcasted_iota(jnp.int32, (n, n), 0) == lax.broadcasted_iota(jnp.int32, (n, n), 1)
    ones = jnp.ones((n, width), BF16)
    out = jnp.zeros((n, width), F32)
    for part in _split3(row):
        d = jnp.where(eye, jnp.broadcast_to(part.astype(F32), (n, n)), 0.0)
        out = out + jnp.dot(d.astype(BF16), ones, preferred_element_type=F32)
    return out


def _gla_kernel(q_ref, k_ref, v_ref, rg_ref, lr_ref, wup_ref, bup_ref, nw_ref, tri_ref, lvl_ref, hm_ref,
                o_ref, s_out_ref, s_ref, *, n_chunks):
    c = CHUNK

    @pl.when(pl.program_id(0) == 0)
    def _():
        s_ref[...] = jnp.zeros(s_ref.shape, F32)

    tri = tri_ref[...]
    lvl = lvl_ref[...]
    hm = hm_ref[...]
    nl = len(GLA_LEVELS)

    def chunk(ci, carry):
        r0 = pl.multiple_of(ci * c, c)
        rows = pl.ds(r0, c)
        q = q_ref[rows, :] * (G_QK ** -0.5)
        k = k_ref[rows, :]
        v = v_ref[rows, :].astype(BF16)
        gk = jnp.dot(lr_ref[rows, :].astype(BF16), wup_ref[...], preferred_element_type=F32) + bup_ref[...]
        g = _log_sigmoid(gk) / G_NORMALIZER
        ps = jnp.zeros((2 * nl * c, HK), F32)
        for part in _split3(g):
            ps = ps + jnp.dot(tri, part, preferred_element_type=F32)
        pre = [g] + [ps[n * c:(n + 1) * c] for n in range(nl)]
        suf = [jnp.zeros_like(g)] + [ps[(nl + n) * c:(nl + n + 1) * c] for n in range(nl)]
        gcum, gsuf = pre[nl], suf[nl]
        a = jnp.zeros((G_HEADS * c, c), F32)
        for n in range(nl + 1):
            if n == 0:
                qq, kk = q, k
            else:
                qq, kk = q * jnp.exp(pre[n - 1]), k * jnp.exp(suf[n - 1])
            d = lax.dot_general(_stack_heads(qq, hm).astype(BF16), kk.astype(BF16),
                                (((1,), (1,)), ((), ())), preferred_element_type=F32)
            a = jnp.where(lvl == n, d, a)
        s_old = s_ref[...]
        inter = jnp.dot(_stack_heads(q * jnp.exp(gcum), hm).astype(BF16), s_old.astype(BF16),
                        preferred_element_type=F32)
        ab = a.astype(BF16)
        nw = nw_ref[...]
        kv = lax.dot_general((k * jnp.exp(gsuf)).astype(BF16), v, (((0,), (0,)), ((), ())),
                             preferred_element_type=F32)
        decay = jnp.exp(_col_bcast(gcum[c - 1:c, :], G_V))
        for hh in range(G_HEADS):
            vs = slice(hh * G_V, (hh + 1) * G_V)
            o = inter[hh * c:(hh + 1) * c] + jnp.dot(ab[hh * c:(hh + 1) * c], v[:, vs],
                                                     preferred_element_type=F32)
            o_ref[rows, vs] = _rms(o, nw) * _silu(rg_ref[rows, vs])
            ks = slice(hh * G_QK, (hh + 1) * G_QK)
            s_ref[ks, :] = decay[ks] * s_old[ks] + kv[ks, vs]
        return carry

    lax.fori_loop(0, n_chunks, chunk, 0)

    @pl.when(pl.program_id(0) == pl.num_programs(0) - 1)
    def _():
        s_out_ref[...] = s_ref[...]


def gla_prompt(h, lrg, w_up, b_up, norm_w):
    t = h.shape[0]
    tb = _tile(t, 512)
    tri, lvl, hm = _gla_constants()
    full = lambda a: pl.BlockSpec(a.shape, lambda i: (0,) * a.ndim)
    o, s = pl.pallas_call(
        functools.partial(_gla_kernel, n_chunks=tb // CHUNK),
        grid=(t // tb,),
        in_specs=[pl.BlockSpec((tb, HK), lambda i: (i, QG // 2)),
                  pl.BlockSpec((tb, HK), lambda i: (i, KG // 2)),
                  pl.BlockSpec((tb, HV), lambda i: (i, VG // 4)),
                  pl.BlockSpec((tb, HV), lambda i: (i, RG // 4)),
                  pl.BlockSpec((tb, LANES), lambda i: (i, 0)),
                  full(w_up), full(b_up), full(norm_w), full(tri), full(lvl), full(hm)],
        out_specs=[pl.BlockSpec((tb, HV), lambda i: (i, 0)),
                   pl.BlockSpec((HK, G_V), lambda i: (0, 0))],
        out_shape=[jax.ShapeDtypeStruct((t, HV), F32), jax.ShapeDtypeStruct((HK, G_V), F32)],
        scratch_shapes=[pltpu.VMEM((HK, G_V), F32)],
        compiler_params=_cparams(("arbitrary",)))(h, h, h, h, lrg, w_up, b_up, norm_w, tri, lvl, hm)
    return o, s.reshape(G_HEADS, G_QK, G_V)


def _ret_kernel(q_ref, k_ref, v_ref, gr_ref, nw_ref, dm_ref, cross_ref, tail_ref, gc_ref, hm_ref,
                o_ref, s_out_ref, s_ref, *, n_chunks):
    c = CHUNK

    @pl.when(pl.program_id(0) == 0)
    def _():
        s_ref[...] = jnp.zeros(s_ref.shape, F32)

    hm = hm_ref[...]
    dm = dm_ref[...]
    cross = cross_ref[...]
    tail = tail_ref[...]
    gc = gc_ref[...]
    nw = nw_ref[...]

    def chunk(ci, carry):
        r0 = pl.multiple_of(ci * c, c)
        rows = pl.ds(r0, c)
        q = q_ref[rows, :]
        k = k_ref[rows, :]
        v = v_ref[rows, :].astype(BF16)
        qs = _stack_heads(q, hm).astype(BF16)
        a = lax.dot_general(qs, k.astype(BF16), (((1,), (1,)), ((), ())), preferred_element_type=F32) * dm
        s_old = s_ref[...]
        inter = jnp.dot(qs, s_old.astype(BF16), preferred_element_type=F32) * cross
        kv = lax.dot_general((k * tail).astype(BF16), v, (((0,), (0,)), ((), ())), preferred_element_type=F32)
        ab = a.astype(BF16)
        for hh in range(R_HEADS):
            vs = slice(hh * G_V, (hh + 1) * G_V)
            o = inter[hh * c:(hh + 1) * c] + jnp.dot(ab[hh * c:(hh + 1) * c], v[:, vs],
                                                     preferred_element_type=F32)
            o_ref[rows, vs] = _rms(o, nw) * _silu(gr_ref[rows, vs])
            ks = slice(hh * R_QK, (hh + 1) * R_QK)
            s_ref[ks, :] = gc[ks] * s_old[ks] + kv[ks, vs]
        return carry

    lax.fori_loop(0, n_chunks, chunk, 0)

    @pl.when(pl.program_id(0) == pl.num_programs(0) - 1)
    def _():
        s_out_ref[...] = s_ref[...]


def _ret_log_gamma():
    return jnp.log(1.0 - jnp.exp2(-5.0 - jnp.arange(R_HEADS, dtype=F32)))


def ret_prompt(h, qr, kr, norm_w):
    t = h.shape[0]
    tb = _tile(t, 512)
    c = CHUNK
    lg = _ret_log_gamma()
    idx = jnp.arange(c, dtype=F32)
    rel = idx[:, None] - idx[None, :]
    dmat = jnp.where(rel[None] >= 0, jnp.exp(jnp.maximum(rel, 0.0)[None] * lg[:, None, None]), 0.0)
    dmat = dmat.reshape(R_HEADS * c, c)
    cross = jnp.exp((idx + 1.0)[None, :] * lg[:, None]).reshape(R_HEADS * c, 1)
    cross = jnp.broadcast_to(cross, (R_HEADS * c, G_V))
    tail = jnp.exp((c - 1.0 - idx)[None, :] * lg[:, None])
    tail = jnp.repeat(tail.T, R_QK, axis=1)
    gc = jnp.broadcast_to(jnp.repeat(jnp.exp(c * lg), R_QK)[:, None], (HK, G_V))
    _, _, hm = _gla_constants()
    full = lambda a: pl.BlockSpec(a.shape, lambda i: (0,) * a.ndim)
    o, s = pl.pallas_call(
        functools.partial(_ret_kernel, n_chunks=tb // c),
        grid=(t // tb,),
        in_specs=[pl.BlockSpec((tb, HK), lambda i: (i, 0)),
                  pl.BlockSpec((tb, HK), lambda i: (i, 0)),
                  pl.BlockSpec((tb, HV), lambda i: (i, VR // 4)),
                  pl.BlockSpec((tb, HV), lambda i: (i, GR // 4)),
                  full(norm_w), full(dmat), full(cross), full(tail), full(gc), full(hm)],
        out_specs=[pl.BlockSpec((tb, HV), lambda i: (i, 0)),
                   pl.BlockSpec((HK, G_V), lambda i: (0, 0))],
        out_shape=[jax.ShapeDtypeStruct((t, HV), F32), jax.ShapeDtypeStruct((HK, G_V), F32)],
        scratch_shapes=[pltpu.VMEM((HK, G_V), F32)],
        compiler_params=_cparams(("arbitrary",)))(qr, kr, h, h, norm_w, dmat, cross, tail, gc, hm)
    return o, s.reshape(R_HEADS, R_QK, G_V)


def _step_kernel(s_ref, q_ref, k_ref, d_ref, v_ref, gate_ref, nw_ref, s_out_ref, o_ref, *, is_gla, q_scale):
    d = d_ref[...]
    if is_gla:
        d = _log_sigmoid(d) / G_NORMALIZER
    s_new = jnp.exp(d) * s_ref[...] + k_ref[...] * v_ref[...]
    s_out_ref[...] = s_new
    o = jnp.sum((q_ref[...] * q_scale) * s_new, axis=1, keepdims=True)
    o_ref[...] = _rms(o, nw_ref[...]) * _silu(gate_ref[...])


def recurrent_step(state, q, k, dlog, v, gate, norm_w, is_gla, q_scale):
    b = state.shape[0]
    col = lambda a: a.reshape(b, G_HEADS, G_QK, 1)
    rowv = lambda a: a.reshape(b, G_HEADS, 1, G_V)
    cspec = pl.BlockSpec((None, G_HEADS, G_QK, 1), lambda i: (i, 0, 0, 0))
    rspec = pl.BlockSpec((None, G_HEADS, 1, G_V), lambda i: (i, 0, 0, 0))
    sspec = pl.BlockSpec((None, G_HEADS, G_QK, G_V), lambda i: (i, 0, 0, 0))
    s_new, o = pl.pallas_call(
        functools.partial(_step_kernel, is_gla=is_gla, q_scale=q_scale),
        grid=(b,),
        in_specs=[sspec, cspec, cspec, cspec, rspec, rspec, pl.BlockSpec((1, G_V), lambda i: (0, 0))],
        out_specs=[sspec, rspec],
        out_shape=[jax.ShapeDtypeStruct(state.shape, F32), jax.ShapeDtypeStruct((b, G_HEADS, 1, G_V), F32)],
        compiler_params=_cparams(("parallel",)))(state, col(q), col(k), col(dlog), rowv(v), rowv(gate), norm_w)
    return o.reshape(b, HV), s_new


def _mixer_out_kernel(oa_ref, og_ref, or_ref, za_ref, zb_ref, zc_ref, x_ref, wpa_ref, wpb_ref, wpc_ref,
                      wo_ref, g_ref, b_ref, y_ref, *, alpha):
    def branch(o_ref, z_ref, w_ref):
        return _sigmoid(z_ref[...]) * jnp.dot(o_ref[...].astype(BF16), w_ref[...], preferred_element_type=F32)

    merged = branch(oa_ref, za_ref, wpa_ref) + branch(og_ref, zb_ref, wpb_ref) + branch(or_ref, zc_ref, wpc_ref)
    y = alpha * x_ref[...] + jnp.dot(merged.astype(BF16), wo_ref[...], preferred_element_type=F32)
    y_ref[...] = _layer_norm(y, g_ref[...], b_ref[...])


def mixer_out(oa, og, orr, h, x, wpa, wpb, wpc, wo, g, b, alpha):
    m = x.shape[0]
    tm = _tile(m, 256)
    bw = oa.shape[1]
    ospec = pl.BlockSpec((tm, bw), lambda i: (i, 0))
    zspec = lambda blk: pl.BlockSpec((tm, D_MODEL), lambda i: (i, blk // 8))
    xspec = pl.BlockSpec((tm, D_MODEL), lambda i: (i, 0))
    full = lambda a: pl.BlockSpec(a.shape, lambda i: (0,) * a.ndim)
    return pl.pallas_call(
        functools.partial(_mixer_out_kernel, alpha=alpha),
        grid=(m // tm,),
        in_specs=[ospec, ospec, ospec, zspec(ZA), zspec(ZB), zspec(ZC), xspec,
                  full(wpa), full(wpb), full(wpc), full(wo), full(g), full(b)],
        out_specs=xspec,
        out_shape=jax.ShapeDtypeStruct((m, D_MODEL), F32),
        compiler_params=_cparams(("parallel",)))(oa, og, orr, h, h, h, x, wpa, wpb, wpc, wo, g, b)


def _ffn_up_kernel(x_ref, wg_ref, wu_ref, h_ref):
    x = x_ref[...].astype(BF16)
    a = jnp.dot(x, wg_ref[...], preferred_element_type=F32)
    u = jnp.dot(x, wu_ref[...], preferred_element_type=F32)
    h_ref[...] = (_silu(a) * u).astype(h_ref.dtype)


def _ffn_down_kernel(h_ref, wd_ref, x_ref, g_ref, b_ref, y_ref, *, alpha):
    y = alpha * x_ref[...] + jnp.dot(h_ref[...], wd_ref[...], preferred_element_type=F32)
    y_ref[...] = _layer_norm(y, g_ref[...], b_ref[...])


def dense_ffn(x, wg, wu, wd, g, b, alpha):
    m = x.shape[0]
    f = wg.shape[1]
    tm, tf = _tile(m, 512), _tile(f, 1408)
    hmid = pl.pallas_call(
        _ffn_up_kernel, grid=(m // tm, f // tf),
        in_specs=[pl.BlockSpec((tm, D_MODEL), lambda i, j: (i, 0)),
                  pl.BlockSpec((D_MODEL, tf), lambda i, j: (0, j)),
                  pl.BlockSpec((D_MODEL, tf), lambda i, j: (0, j))],
        out_specs=pl.BlockSpec((tm, tf), lambda i, j: (i, j)),
        out_shape=jax.ShapeDtypeStruct((m, f), BF16),
        compiler_params=_cparams(("parallel", "parallel")))(x, wg, wu)
    full = lambda a: pl.BlockSpec(a.shape, lambda i: (0,) * a.ndim)
    xspec = pl.BlockSpec((tm, D_MODEL), lambda i: (i, 0))
    return pl.pallas_call(
        functools.partial(_ffn_down_kernel, alpha=alpha), grid=(m // tm,),
        in_specs=[pl.BlockSpec((tm, f), lambda i: (i, 0)), full(wd), xspec, full(g), full(b)],
        out_specs=xspec,
        out_shape=jax.ShapeDtypeStruct((m, D_MODEL), F32),
        compiler_params=_cparams(("parallel",)))(hmid, wd, x, g, b)


def _router_kernel(x_ref, wh_ref, wl_ref, comb_ref):
    x = x_ref[...]
    xh = x.astype(BF16)
    xl = (x - xh.astype(F32)).astype(BF16)
    wh, wl = wh_ref[...], wl_ref[...]
    logits = (jnp.dot(xh, wh, preferred_element_type=F32) + jnp.dot(xh, wl, preferred_element_type=F32)
              + jnp.dot(xl, wh, preferred_element_type=F32))
    lane = lax.broadcasted_iota(jnp.int32, logits.shape, 1)
    logits = jnp.where(lane < N_EXPERTS, logits, NEG_BIG)
    m1 = jnp.max(logits, axis=-1, keepdims=True)
    i1 = jnp.min(jnp.where(logits == m1, lane, LANES), axis=-1, keepdims=True)
    rest = jnp.where(lane == i1, NEG_BIG, logits)
    m2 = jnp.max(rest, axis=-1, keepdims=True)
    i2 = jnp.min(jnp.where(rest == m2, lane, LANES), axis=-1, keepdims=True)
    e2 = jnp.exp(m2 - m1)
    w1 = 1.0 / (1.0 + e2)
    w2 = e2 / (1.0 + e2)
    comb_ref[...] = jnp.where(lane == i1, w1, jnp.where(lane == i2, w2, 0.0))


def router(x, w_router):
    m = x.shape[0]
    tm = _tile(m, 512)
    wpad = jnp.zeros((D_MODEL, LANES), F32).at[:, :N_EXPERTS].set(w_router)
    wh = wpad.astype(BF16)
    wl = (wpad - wh.astype(F32)).astype(BF16)
    full = lambda a: pl.BlockSpec(a.shape, lambda i: (0,) * a.ndim)
    return pl.pallas_call(
        _router_kernel, grid=(m // tm,),
        in_specs=[pl.BlockSpec((tm, D_MODEL), lambda i: (i, 0)), full(wh), full(wl)],
        out_specs=pl.BlockSpec((tm, LANES), lambda i: (i, 0)),
        out_shape=jax.ShapeDtypeStruct((m, LANES), F32),
        compiler_params=_cparams(("parallel",)))(x, wh, wl)


def _moe_kernel(x_ref, comb_ref, wg_ref, wu_ref, wd_ref, g_ref, b_ref, y_ref, acc_ref, *, alpha):
    e, f = pl.program_id(1), pl.program_id(2)

    @pl.when((e == 0) & (f == 0))
    def _():
        acc_ref[...] = jnp.zeros(acc_ref.shape, F32)

    x = x_ref[...].astype(BF16)
    a = jnp.dot(x, wg_ref[...], preferred_element_type=F32)
    u = jnp.dot(x, wu_ref[...], preferred_element_type=F32)
    hmid = (_silu(a) * u).astype(BF16)
    comb = comb_ref[...]
    lane = lax.broadcasted_iota(jnp.int32, comb.shape, 1)
    ce = jnp.sum(jnp.where(lane == e, comb, 0.0), axis=-1, keepdims=True)
    acc_ref[...] += ce * jnp.dot(hmid, wd_ref[...], preferred_element_type=F32)

    @pl.when((e == pl.num_programs(1) - 1) & (f == pl.num_programs(2) - 1))
    def _():
        y_ref[...] = _layer_norm(alpha * x_ref[...] + acc_ref[...], g_ref[...], b_ref[...])


def moe_ffn(x, comb, wg, wu, wd, g, b, alpha):
    m = x.shape[0]
    ne, _, f = wg.shape
    tm, tf = _tile(m, 1024), _tile(f, 512)
    xspec = pl.BlockSpec((tm, D_MODEL), lambda i, e, j: (i, 0))
    full = lambda a: pl.BlockSpec(a.shape, lambda i, e, j: (0,) * a.ndim)
    return pl.pallas_call(
        functools.partial(_moe_kernel, alpha=alpha), grid=(m // tm, ne, f // tf),
        in_specs=[xspec, pl.BlockSpec((tm, LANES), lambda i, e, j: (i, 0)),
                  pl.BlockSpec((None, D_MODEL, tf), lambda i, e, j: (e, 0, j)),
                  pl.BlockSpec((None, D_MODEL, tf), lambda i, e, j: (e, 0, j)),
                  pl.BlockSpec((None, tf, D_MODEL), lambda i, e, j: (e, j, 0)),
                  full(g), full(b)],
        out_specs=xspec,
        out_shape=jax.ShapeDtypeStruct((m, D_MODEL), F32),
        scratch_shapes=[pltpu.VMEM((tm, D_MODEL), F32)],
        compiler_params=_cparams(("parallel", "arbitrary", "arbitrary")))(x, comb, wg, wu, wd, g, b)


def _layer_weights(l, w_in, lam_q1, lam_k1, lam_q2, lam_k2, subln_w, w_gla_up, b_gla_up, gla_norm_w,
                   ret_norm_w, w_pa, w_pb, w_pc, w_out, ln1_g, ln1_b, ln2_g, ln2_b):
    wl = w_in[l]
    w_main = jnp.concatenate([wl[:, a:b] for a, b in PACK_ORDER], axis=1).astype(BF16)
    w_lrg = jnp.zeros((D_MODEL, LANES), F32).at[:, :G_RANK].set(wl[:, LRG_OFF:LRG_OFF + G_RANK]).astype(BF16)
    w_up = jnp.zeros((LANES, HK), F32).at[:G_RANK].set(w_gla_up[l]).astype(BF16)
    lamvec = jnp.zeros((8, LANES), F32)
    for r, vec in enumerate((lam_q1, lam_k1, lam_q2, lam_k2)):
        lamvec = lamvec.at[r, :A_QK].set(vec[l].astype(F32))
    row = lambda a: a[l].reshape(1, -1)
    return dict(w_main=w_main, w_lrg=w_lrg, w_up=w_up, b_up=row(b_gla_up), lamvec=lamvec,
                subln=row(subln_w), gla_nw=row(gla_norm_w), ret_nw=row(ret_norm_w),
                wpa=w_pa[l].astype(BF16), wpb=w_pb[l].astype(BF16), wpc=w_pc[l].astype(BF16),
                wo=w_out[l].astype(BF16), ln1_g=row(ln1_g), ln1_b=row(ln1_b),
                ln2_g=row(ln2_g), ln2_b=row(ln2_b), lam_init=0.8 - 0.6 * math.exp(-0.3 * l))


def _project(x, pos, lw, q_dtype):
    h = matmul(x, lw['w_main'])
    lrg = matmul(x, lw['w_lrg'], tn_cap=LANES)
    qa, ka = rotary(h, QA, A_HEADS * LANES, _rope_tables(pos), ROPE_DIM // 2, A_QK ** -0.5, 1.0, q_dtype)
    qr, kr = rotary(h, QR, HK, _ret_tables(pos), 1, 1.0, R_QK ** -0.5, F32)
    return h, lrg, qa, ka, qr, kr


def _channel_mix(x1, l, lw, ffn_w, alpha):
    if l % 2 == 0:
        wg, wu, wd = ffn_w['dense'][l // 2]
        return dense_ffn(x1, wg, wu, wd, lw['ln2_g'], lw['ln2_b'], alpha)
    w_r, wg, wu, wd = ffn_w['moe'][l // 2]
    comb = router(x1, w_r)
    return moe_ffn(x1, comb, wg, wu, wd, lw['ln2_g'], lw['ln2_b'], alpha)


def kernel(x_prompt, x_sample, cache_k, cache_v, state_gla, state_ret, page_table, w_in, lam_q1, lam_k1, lam_q2, lam_k2, subln_w, w_gla_up, b_gla_up, gla_norm_w, ret_norm_w, w_pa, w_pb, w_pc, w_out, ln1_g, ln1_b, w_ff_gate, w_ff_up, w_ff_down, w_router, w_exp_gate, w_exp_up, w_exp_down, ln2_g, ln2_b):
    bp, tp, _ = x_prompt.shape
    bs, ts, _ = x_sample.shape
    assert bp == 1 and ts == 1
    depth = w_in.shape[0]
    alpha = (2 * depth) ** 0.25
    n_pool, page = cache_k.shape[1], cache_k.shape[2]
    past_len = page_table.shape[1] * page
    pos_p = jnp.arange(tp, dtype=jnp.int32)
    pos_s = jnp.full((bs,), past_len, jnp.int32)
    ffn_w = dict(
        dense=[(w_ff_gate[i].astype(BF16), w_ff_up[i].astype(BF16), w_ff_down[i].astype(BF16))
               for i in range(w_ff_gate.shape[0])],
        moe=[(w_router[i], w_exp_gate[i].astype(BF16), w_exp_up[i].astype(BF16), w_exp_down[i].astype(BF16))
             for i in range(w_router.shape[0])])
    width = A_HEADS * LANES
    ck = cache_k.reshape(depth, n_pool, page, width)
    cv = cache_v.reshape(depth, n_pool, page, width)
    lg_col = jnp.broadcast_to(jnp.repeat(_ret_log_gamma(), R_QK)[None, :], (bs, HK))

    yp = x_prompt.reshape(tp, D_MODEL)
    ys = x_sample.reshape(bs, D_MODEL)
    outs = {n: [] for n in ('kp', 'vp', 'gp', 'rp', 'ks', 'vs', 'gs', 'rs')}
    cols = lambda a, blk, n: a[:, blk * LANES:(blk + n) * LANES]
    for l in range(depth):
        lw = _layer_weights(l, w_in, lam_q1, lam_k1, lam_q2, lam_k2, subln_w, w_gla_up, b_gla_up,
                            gla_norm_w, ret_norm_w, w_pa, w_pb, w_pc, w_out, ln1_g, ln1_b, ln2_g, ln2_b)
        h, lrg, qa, ka, qr, kr = _project(yp, pos_p, lw, BF16)
        oa = flash_diff_attention(qa, ka, h, lw['lamvec'], lw['subln'], lw['lam_init'])
        og, s_gla = gla_prompt(h, lrg, lw['w_up'], lw['b_up'], lw['gla_nw'])
        orr, s_ret = ret_prompt(h, qr, kr, lw['ret_nw'])
        x1 = mixer_out(oa, og, orr, h, yp, lw['wpa'], lw['wpb'], lw['wpc'], lw['wo'],
                       lw['ln1_g'], lw['ln1_b'], alpha)
        yp = _channel_mix(x1, l, lw, ffn_w, alpha)
        outs['kp'].append(ka.reshape(1, tp, A_HEADS, 2 * A_QK))
        outs['vp'].append(cols(h, VA, 4).reshape(1, tp, A_HEADS, 2 * A_QK))
        outs['gp'].append(s_gla[None])
        outs['rp'].append(s_ret[None])
        h, lrg, qa, ka, qr, kr = _project(ys, pos_s, lw, F32)
        va = cols(h, VA, 4)
        oa = decode_diff_attention(qa, ka, va, ck[l], cv[l], page_table, lw['lamvec'], lw['subln'],
                                   lw['lam_init'])
        gk = matmul(lrg, lw['w_up'], bias=lw['b_up'])
        og, s_gla = recurrent_step(state_gla[l], cols(h, QG, 2), cols(h, KG, 2), gk, cols(h, VG, 4),
                                   cols(h, RG, 4), lw['gla_nw'], True, G_QK ** -0.5)
        orr, s_ret = recurrent_step(state_ret[l], qr, kr, lg_col, cols(h, VR, 4), cols(h, GR, 4),
                                    lw['ret_nw'], False, 1.0)
        x1 = mixer_out(oa, og, orr, h, ys, lw['wpa'], lw['wpb'], lw['wpc'], lw['wo'],
                       lw['ln1_g'], lw['ln1_b'], alpha)
        ys = _channel_mix(x1, l, lw, ffn_w, alpha)
        outs['ks'].append(ka.reshape(bs, 1, A_HEADS, 2 * A_QK))
        outs['vs'].append(va.reshape(bs, 1, A_HEADS, 2 * A_QK))
        outs['gs'].append(s_gla)
        outs['rs'].append(s_ret)

    st = lambda n: jnp.stack(outs[n])
    return (yp.reshape(bp, tp, D_MODEL), ys.reshape(bs, ts, D_MODEL), st('kp'), st('vp'), st('gp'),
            st('rp'), st('ks'), st('vs'), st('gs'), st('rs'))
```

```python
import functools
import math

import numpy as np
import jax
import jax.numpy as jnp
from jax import lax
from jax.experimental import pallas as pl
from jax.experimental.pallas import tpu as pltpu

F32 = jnp.float32
BF16 = jnp.bfloat16

D_MODEL = 1024
A_HEADS = 4
A_QK = 64
ROPE_DIM = 16
ROPE_THETA = 500000.0
G_HEADS = 4
G_QK = 64
G_V = 128
G_RANK = 16
G_NORMALIZER = 16.0
R_HEADS = 4
R_QK = 64
R_ANGLE_BASE = 10000.0
CHUNK = 64
N_EXPERTS = 8
NORM_EPS = 1e-5
NEG_BIG = -1e30
LOG2E = math.log2(math.e)
FLASH_TQ, FLASH_TK = 1024, 1024
FLASH_RC = 32
DECODE_PAGES = 8

LANES = 128
VMEM_LIMIT = 56 * 1024 * 1024

ZA, ZB, ZC, QA, KA, VA, VG, RG, VR, GR, QG, KG, QR, KR = 0, 8, 16, 24, 28, 32, 36, 40, 44, 48, 52, 54, 56, 58
PACK_ORDER = ((4624, 7696), (0, 1536), (2048, 2560), (2576, 3088), (3600, 4112), (4112, 4624),
              (1536, 2048), (3088, 3600))
LRG_OFF = 2560


def _cparams(sem):
    return pltpu.CompilerParams(dimension_semantics=sem, vmem_limit_bytes=VMEM_LIMIT)


def _tile(n, cap):
    c = min(n, cap)
    while n % c:
        c -= 1
    return c


def _layer_norm(y, g, b):
    mu = jnp.mean(y, axis=-1, keepdims=True)
    d = y - mu
    var = jnp.mean(d * d, axis=-1, keepdims=True)
    return d * lax.rsqrt(var + NORM_EPS) * g + b


def _rms(o, w):
    return o * lax.rsqrt(jnp.mean(o * o, axis=-1, keepdims=True) + NORM_EPS) * w


def _silu(x):
    return x * (1.0 / (1.0 + jnp.exp(-x)))


def _sigmoid(x):
    return 1.0 / (1.0 + jnp.exp(-x))


def _log_sigmoid(x):
    return jnp.minimum(x, 0.0) - jnp.log(1.0 + jnp.exp(-jnp.abs(x)))


def _split3(x):
    a = x.astype(BF16)
    r = x - a.astype(F32)
    b = r.astype(BF16)
    c = (r - b.astype(F32)).astype(BF16)
    return a, b, c


def _mm_kernel(x_ref, w_ref, o_ref):
    o_ref[...] = jnp.dot(x_ref[...].astype(BF16), w_ref[...],
                         preferred_element_type=F32).astype(o_ref.dtype)


def _mm_bias_kernel(x_ref, w_ref, b_ref, o_ref):
    o_ref[...] = (jnp.dot(x_ref[...].astype(BF16), w_ref[...],
                          preferred_element_type=F32) + b_ref[...]).astype(o_ref.dtype)


def matmul(x, w, bias=None, out_dtype=F32, tm_cap=1024, tn_cap=768):
    m, k = x.shape
    n = w.shape[1]
    tm, tn = _tile(m, tm_cap), _tile(n, tn_cap)
    in_specs = [pl.BlockSpec((tm, k), lambda i, j: (i, 0)),
                pl.BlockSpec((k, tn), lambda i, j: (0, j))]
    args = [x, w]
    kern = _mm_kernel
    if bias is not None:
        in_specs.append(pl.BlockSpec((1, tn), lambda i, j: (0, j)))
        args.append(bias)
        kern = _mm_bias_kernel
    return pl.pallas_call(
        kern, grid=(m // tm, n // tn), in_specs=in_specs,
        out_specs=pl.BlockSpec((tm, tn), lambda i, j: (i, j)),
        out_shape=jax.ShapeDtypeStruct((m, n), out_dtype),
        compiler_params=_cparams(("parallel", "parallel")))(*args)


def _rot_kernel(x_ref, c_ref, s1_ref, s2_ref, oq_ref, ok_ref, *, shift, q_scale, k_scale):
    c, s1, s2 = c_ref[...], s1_ref[...], s2_ref[...]
    nq = oq_ref.shape[1] // LANES
    nk = ok_ref.shape[1] // LANES
    for b in range(nq + nk):
        x = x_ref[:, b * LANES:(b + 1) * LANES]
        y = x * c + pltpu.roll(x, LANES - shift, 1) * s1 + pltpu.roll(x, shift, 1) * s2
        if b < nq:
            oq_ref[:, b * LANES:(b + 1) * LANES] = (y * q_scale).astype(oq_ref.dtype)
        else:
            ok_ref[:, (b - nq) * LANES:(b - nq + 1) * LANES] = (y * k_scale).astype(ok_ref.dtype)


def rotary(h, col_blk, width, tables, shift, q_scale, k_scale, q_dtype):
    m = h.shape[0]
    tm = _tile(m, 512)
    blk = col_blk * LANES // (2 * width)
    tspec = pl.BlockSpec((tm, LANES), lambda i: (i, 0))
    return pl.pallas_call(
        functools.partial(_rot_kernel, shift=shift, q_scale=q_scale, k_scale=k_scale),
        grid=(m // tm,),
        in_specs=[pl.BlockSpec((tm, 2 * width), lambda i: (i, blk)), tspec, tspec, tspec],
        out_specs=[pl.BlockSpec((tm, width), lambda i: (i, 0)),
                   pl.BlockSpec((tm, width), lambda i: (i, 0))],
        out_shape=[jax.ShapeDtypeStruct((m, width), q_dtype),
                   jax.ShapeDtypeStruct((m, width), F32)],
        compiler_params=_cparams(("parallel",)))(h, *tables)


def _rope_tables(pos):
    half = ROPE_DIM // 2
    inv = ROPE_THETA ** (-jnp.arange(half, dtype=F32) * 2.0 / ROPE_DIM)
    ang = pos.astype(F32)[:, None] * inv[None, :]
    cos, sin = jnp.cos(ang), jnp.sin(ang)
    lane = np.arange(LANES) % A_QK
    fi = lane % half
    cos_l, sin_l = cos[:, fi], sin[:, fi]
    in_rot = (lane < ROPE_DIM)[None, :]
    lo = (lane < half)[None, :]
    c = jnp.where(in_rot, cos_l, 1.0)
    s1 = jnp.where(lo, -sin_l, 0.0)
    s2 = jnp.where(in_rot & ~lo, sin_l, 0.0)
    return c, s1, s2


def _ret_tables(pos):
    inv = 1.0 / (R_ANGLE_BASE ** jnp.linspace(0.0, 1.0, R_QK // 2, dtype=F32))
    ang = pos.astype(F32)[:, None] * inv[None, :]
    cos, sin = jnp.cos(ang), jnp.sin(ang)
    lane = np.arange(LANES) % R_QK
    cos_l, sin_l = cos[:, lane // 2], sin[:, lane // 2]
    even = (lane % 2 == 0)[None, :]
    return cos_l, jnp.where(even, -sin_l, 0.0), jnp.where(even, 0.0, sin_l)


def _lam_from(lam_ref, lam_init):
    v = lam_ref[...]
    t1 = jnp.sum(v[0:1] * v[1:2], axis=-1, keepdims=True)
    t2 = jnp.sum(v[2:3] * v[3:4], axis=-1, keepdims=True)
    return jnp.exp(t1) - jnp.exp(t2) + lam_init


def _flash_kernel(lam_ref, q_ref, k_ref, v_ref, w_ref, o_ref, m_ref, l_ref, acc_ref, s_ref, p_ref, *,
                  tq, tk, rc, lam_init):
    i, j = pl.program_id(1), pl.program_id(2)

    @pl.when(j == 0)
    def _():
        m_ref[...] = jnp.full(m_ref.shape, NEG_BIG, F32)
        l_ref[...] = jnp.zeros(l_ref.shape, F32)
        acc_ref[...] = jnp.zeros(acc_ref.shape, F32)

    def block(masked):
        k = k_ref[...].astype(BF16)
        v = v_ref[...].astype(BF16)
        q = q_ref[...]
        lane = lax.broadcasted_iota(jnp.int32, q.shape, 1)
        zero = jnp.zeros_like(q)
        for c in range(2):
            qc = jnp.where((lane < A_QK) if c == 0 else (lane >= A_QK), q, zero)
            s_ref[c] = lax.dot_general(qc, k, (((1,), (1,)), ((), ())), preferred_element_type=F32)

        def chunk(r, carry):
            r0 = r * rc
            rows = slice(r0, r0 + rc)
            if masked:
                row = i * tq + r0 + lax.broadcasted_iota(jnp.int32, (rc, tk), 0)
                col = j * tk + lax.broadcasted_iota(jnp.int32, (rc, tk), 1)
                causal = col <= row
            for c in range(2):
                s = s_ref[c, rows, :]
                if masked:
                    s = jnp.where(causal, s, NEG_BIG)
                m_prev = m_ref[c, rows, :]
                m_new = jnp.maximum(m_prev, jnp.max(s, axis=-1, keepdims=True))
                p = jnp.exp2(s - pltpu.repeat(m_new, tk // LANES, axis=1))
                alpha = jnp.exp2(m_prev - m_new)
                l_ref[c, rows, :] = alpha * l_ref[c, rows, :] + jnp.sum(p, axis=-1, keepdims=True)
                acc_ref[c, rows, :] = alpha * acc_ref[c, rows, :]
                m_ref[c, rows, :] = m_new
                p_ref[c, rows, :] = p.astype(BF16)
            return carry

        for r in range(tq // rc):
            chunk(r, 0)
        for c in range(2):
            acc_ref[c] += jnp.dot(p_ref[c], v, preferred_element_type=F32)

    visible = (j + 1) * tk - 1 <= i * tq
    needed = j * tk <= i * tq + tq - 1
    pl.when(visible)(functools.partial(block, False))
    pl.when(needed & jnp.logical_not(visible))(functools.partial(block, True))

    @pl.when(j == pl.num_programs(2) - 1)
    def _():
        lam = _lam_from(lam_ref, lam_init)
        o = acc_ref[0] / l_ref[0] - lam * (acc_ref[1] / l_ref[1])
        o_ref[...] = _rms(o, w_ref[...]) * (1.0 - lam_init)


def flash_diff_attention(q, k, h, lamvec, subln_w, lam_init, tq_cap=FLASH_TQ, tk_cap=FLASH_TK, rc=FLASH_RC):
    t = q.shape[0]
    tq, tk = _tile(t, tq_cap), _tile(t, tk_cap)
    rc = _tile(tq, rc)
    nq, nk = t // tq, t // tk

    def kv_idx(off):
        return lambda hh, i, j: (jnp.minimum(j, (i * tq + tq - 1) // tk), off + hh)

    return pl.pallas_call(
        functools.partial(_flash_kernel, tq=tq, tk=tk, rc=rc, lam_init=lam_init),
        grid=(A_HEADS, nq, nk),
        in_specs=[pl.BlockSpec((8, LANES), lambda hh, i, j: (0, 0)),
                  pl.BlockSpec((tq, LANES), lambda hh, i, j: (i, hh)),
                  pl.BlockSpec((tk, LANES), kv_idx(0)),
                  pl.BlockSpec((tk, LANES), kv_idx(VA)),
                  pl.BlockSpec((1, LANES), lambda hh, i, j: (0, 0))],
        out_specs=pl.BlockSpec((tq, LANES), lambda hh, i, j: (i, hh)),
        out_shape=jax.ShapeDtypeStruct((t, A_HEADS * LANES), F32),
        scratch_shapes=[pltpu.VMEM((2, tq, LANES), F32), pltpu.VMEM((2, tq, LANES), F32),
                        pltpu.VMEM((2, tq, LANES), F32), pltpu.VMEM((2, tq, tk), F32),
                        pltpu.VMEM((2, tq, tk), BF16)],
        compiler_params=_cparams(("parallel", "parallel", "arbitrary")))(lamvec, q, k, h, subln_w)


def _decode_kernel(pt_ref, lam_ref, q_ref, kn_ref, vn_ref, w_ref, *rest, pp, lam_init):
    k_refs, v_refs = rest[:pp], rest[pp:2 * pp]
    o_ref, m_ref, l_ref, acc_ref = rest[2 * pp:]
    c = pl.program_id(1)
    rows = 2 * A_HEADS
    prow = k_refs[0].shape[0]

    @pl.when(c == 0)
    def _():
        m_ref[...] = jnp.full(m_ref.shape, NEG_BIG, F32)
        l_ref[...] = jnp.zeros(l_ref.shape, F32)
        acc_ref[...] = jnp.zeros(acc_ref.shape, F32)

    rid = lax.broadcasted_iota(jnp.int32, (rows, LANES), 0)
    lid = lax.broadcasted_iota(jnp.int32, (rows, LANES), 1)
    qb = jnp.where(lid // A_QK == rid % 2, q_ref[...], 0.0).astype(BF16)
    s = jnp.concatenate(
        [lax.dot_general(qb, k_refs[p][...].astype(BF16), (((1,), (1,)), ((), ())),
                         preferred_element_type=F32) for p in range(pp)], axis=-1)
    srow = lax.broadcasted_iota(jnp.int32, s.shape, 0)
    scol = lax.broadcasted_iota(jnp.int32, s.shape, 1)
    s = jnp.where(scol % A_HEADS == srow // 2, s, NEG_BIG)
    m_prev = m_ref[...]
    m_new = jnp.maximum(m_prev, jnp.max(s, axis=-1, keepdims=True))
    p_ = jnp.exp(s - m_new)
    alpha = jnp.exp(m_prev - m_new)
    l_new = alpha * l_ref[...] + jnp.sum(p_, axis=-1, keepdims=True)
    acc = alpha * acc_ref[...]
    for p in range(pp):
        acc = acc + jnp.dot(p_[:, p * prow:(p + 1) * prow].astype(BF16), v_refs[p][...].astype(BF16),
                            preferred_element_type=F32)
    m_ref[...] = m_new
    l_ref[...] = l_new
    acc_ref[...] = acc

    @pl.when(c == pl.num_programs(1) - 1)
    def _():
        kn = kn_ref[...].astype(BF16).astype(F32)
        vn = vn_ref[...].astype(BF16).astype(F32)
        s_self = jnp.sum(qb.astype(F32) * kn, axis=-1, keepdims=True)
        m_fin = jnp.maximum(m_new, s_self)
        a2 = jnp.exp(m_new - m_fin)
        p_self = jnp.exp(s_self - m_fin)
        l_fin = a2 * l_new + p_self
        acc_fin = a2 * acc + p_self.astype(BF16).astype(F32) * vn
        on = acc_fin / l_fin
        lam = _lam_from(lam_ref, lam_init)
        w = w_ref[...]
        for hh in range(A_HEADS):
            o = on[2 * hh:2 * hh + 1] - lam * on[2 * hh + 1:2 * hh + 2]
            o_ref[hh:hh + 1, :] = _rms(o, w) * (1.0 - lam_init)


def decode_diff_attention(q, k_new, v_new, cache_k, cache_v, layer, page_table, lamvec, subln_w, lam_init):
    b, n_pages = page_table.shape
    prow = cache_k.shape[2]
    pp = _tile(n_pages, DECODE_PAGES)
    rows = 2 * A_HEADS

    def row(bb, c, pt):
        return (bb, 0, 0)

    def page_idx(p):
        return lambda bb, c, pt: (layer, pt[bb, c * pp + p], 0, 0)

    row_spec = pl.BlockSpec((None, rows, LANES), row)
    page_specs = [pl.BlockSpec((None, None, prow, LANES), page_idx(p)) for p in range(pp)]
    grid_spec = pltpu.PrefetchScalarGridSpec(
        num_scalar_prefetch=1, grid=(b, n_pages // pp),
        in_specs=[pl.BlockSpec((8, LANES), lambda bb, c, pt: (0, 0)), row_spec, row_spec, row_spec,
                  pl.BlockSpec((1, LANES), lambda bb, c, pt: (0, 0))] + page_specs + page_specs,
        out_specs=pl.BlockSpec((None, A_HEADS, LANES), row),
        scratch_shapes=[pltpu.VMEM((rows, 1), F32), pltpu.VMEM((rows, 1), F32),
                        pltpu.VMEM((rows, LANES), F32)])
    r8 = lambda a: jnp.repeat(a.reshape(b, A_HEADS, LANES), 2, axis=1)
    out = pl.pallas_call(
        functools.partial(_decode_kernel, pp=pp, lam_init=lam_init),
        grid_spec=grid_spec,
        out_shape=jax.ShapeDtypeStruct((b, A_HEADS, LANES), F32),
        compiler_params=_cparams(("parallel", "arbitrary")))(
            page_table, lamvec, r8(q), r8(k_new), r8(v_new), subln_w,
            *([cache_k] * pp), *([cache_v] * pp))
    return out.reshape(b, A_HEADS * LANES)


HK = G_HEADS * G_QK
HV = G_HEADS * G_V
GLA_LEVELS = (1, 2, 4, 8, 16, 32)


def _gla_constants():
    c = CHUNK
    t = np.arange(c)[:, None]
    i = np.arange(c)[None, :]
    mats = []
    for s in GLA_LEVELS[1:] + (c,):
        mats.append(((i // s == t // s) & (i <= t)).astype(np.float32))
    for s in GLA_LEVELS[1:] + (c,):
        mats.append(((i // s == t // s) & (i > t)).astype(np.float32))
    tri = np.concatenate(mats, axis=0)
    j = i
    level = np.full((c, c), -1, np.int32)
    level[t == j] = 0
    for n, s in enumerate(GLA_LEVELS):
        sel = (t // (2 * s) == j // (2 * s)) & (t % (2 * s) >= s) & (j % (2 * s) < s)
        level[sel] = n + 1
    level = np.tile(level, (G_HEADS, 1))
    headmask = (np.arange(HK)[None, :] // G_QK == np.arange(G_HEADS * c)[:, None] // c).astype(np.float32)
    return jnp.asarray(tri, BF16), jnp.asarray(level), jnp.asarray(headmask)


def _stack_heads(x, hm):
    return jnp.concatenate([x] * G_HEADS, axis=0) * hm


def _col_bcast(row, width):
    n = row.shape[1]
    eye = lax.broadcasted_iota(jnp.int32, (n, n), 0) == lax.broadcasted_iota(jnp.int32, (n, n), 1)
    ones = jnp.ones((n, width), BF16)
    out = jnp.zeros((n, width), F32)
    for part in _split3(row):
        d = jnp.where(eye, jnp.broadcast_to(part.astype(F32), (n, n)), 0.0)
        out = out + jnp.dot(d.astype(BF16), ones, preferred_element_type=F32)
    return out


def _gla_kernel(q_ref, k_ref, v_ref, rg_ref, lr_ref, wup_ref, bup_ref, nw_ref, tri_ref, lvl_ref, hm_ref,
                o_ref, s_out_ref, s_ref, *, n_chunks):
    c = CHUNK

    @pl.when(pl.program_id(0) == 0)
    def _():
        s_ref[...] = jnp.zeros(s_ref.shape, F32)

    tri = tri_ref[...]
    lvl = lvl_ref[...]
    hm = hm_ref[...]
    nl = len(GLA_LEVELS)

    def chunk(ci, carry):
        r0 = pl.multiple_of(ci * c, c)
        rows = pl.ds(r0, c)
        q = q_ref[rows, :] * (G_QK ** -0.5)
        k = k_ref[rows, :]
        v = v_ref[rows, :].astype(BF16)
        gk = jnp.dot(lr_ref[rows, :].astype(BF16), wup_ref[...], preferred_element_type=F32) + bup_ref[...]
        g = _log_sigmoid(gk) / G_NORMALIZER
        ps = jnp.zeros((2 * nl * c, HK), F32)
        for part in _split3(g):
            ps = ps + jnp.dot(tri, part, preferred_element_type=F32)
        pre = [g] + [ps[n * c:(n + 1) * c] for n in range(nl)]
        suf = [jnp.zeros_like(g)] + [ps[(nl + n) * c:(nl + n + 1) * c] for n in range(nl)]
        gcum, gsuf = pre[nl], suf[nl]
        a = jnp.zeros((G_HEADS * c, c), F32)
        for n in range(nl + 1):
            if n == 0:
                qq, kk = q, k
            else:
                qq, kk = q * jnp.exp(pre[n - 1]), k * jnp.exp(suf[n - 1])
            d = lax.dot_general(_stack_heads(qq, hm).astype(BF16), kk.astype(BF16),
                                (((1,), (1,)), ((), ())), preferred_element_type=F32)
            a = jnp.where(lvl == n, d, a)
        s_old = s_ref[...]
        inter = jnp.dot(_stack_heads(q * jnp.exp(gcum), hm).astype(BF16), s_old.astype(BF16),
                        preferred_element_type=F32)
        ab = a.astype(BF16)
        nw = nw_ref[...]
        kv = lax.dot_general((k * jnp.exp(gsuf)).astype(BF16), v, (((0,), (0,)), ((), ())),
                             preferred_element_type=F32)
        decay = jnp.exp(_col_bcast(gcum[c - 1:c, :], G_V))
        for hh in range(G_HEADS):
            vs = slice(hh * G_V, (hh + 1) * G_V)
            o = inter[hh * c:(hh + 1) * c] + jnp.dot(ab[hh * c:(hh + 1) * c], v[:, vs],
                                                     preferred_element_type=F32)
            o_ref[rows, vs] = _rms(o, nw) * _silu(rg_ref[rows, vs])
            ks = slice(hh * G_QK, (hh + 1) * G_QK)
            s_ref[ks, :] = decay[ks] * s_old[ks] + kv[ks, vs]
        return carry

    lax.fori_loop(0, n_chunks, chunk, 0)

    @pl.when(pl.program_id(0) == pl.num_programs(0) - 1)
    def _():
        s_out_ref[...] = s_ref[...]


def gla_prompt(h, lrg, w_up, b_up, norm_w):
    t = h.shape[0]
    tb = _tile(t, 512)
    tri, lvl, hm = _gla_constants()
    full = lambda a: pl.BlockSpec(a.shape, lambda i: (0,) * a.ndim)
    o, s = pl.pallas_call(
        functools.partial(_gla_kernel, n_chunks=tb // CHUNK),
        grid=(t // tb,),
        in_specs=[pl.BlockSpec((tb, HK), lambda i: (i, QG // 2)),
                  pl.BlockSpec((tb, HK), lambda i: (i, KG // 2)),
                  pl.BlockSpec((tb, HV), lambda i: (i, VG // 4)),
                  pl.BlockSpec((tb, HV), lambda i: (i, RG // 4)),
                  pl.BlockSpec((tb, LANES), lambda i: (i, 0)),
                  full(w_up), full(b_up), full(norm_w), full(tri), full(lvl), full(hm)],
        out_specs=[pl.BlockSpec((tb, HV), lambda i: (i, 0)),
                   pl.BlockSpec((HK, G_V), lambda i: (0, 0))],
        out_shape=[jax.ShapeDtypeStruct((t, HV), F32), jax.ShapeDtypeStruct((HK, G_V), F32)],
        scratch_shapes=[pltpu.VMEM((HK, G_V), F32)],
        compiler_params=_cparams(("arbitrary",)))(h, h, h, h, lrg, w_up, b_up, norm_w, tri, lvl, hm)
    return o, s.reshape(G_HEADS, G_QK, G_V)


def _ret_kernel(q_ref, k_ref, v_ref, gr_ref, nw_ref, dm_ref, cross_ref, tail_ref, gc_ref, hm_ref,
                o_ref, s_out_ref, s_ref, *, n_chunks):
    c = CHUNK

    @pl.when(pl.program_id(0) == 0)
    def _():
        s_ref[...] = jnp.zeros(s_ref.shape, F32)

    hm = hm_ref[...]
    dm = dm_ref[...]
    cross = cross_ref[...]
    tail = tail_ref[...]
    gc = gc_ref[...]
    nw = nw_ref[...]

    def chunk(ci, carry):
        r0 = pl.multiple_of(ci * c, c)
        rows = pl.ds(r0, c)
        q = q_ref[rows, :]
        k = k_ref[rows, :]
        v = v_ref[rows, :].astype(BF16)
        qs = _stack_heads(q, hm).astype(BF16)
        a = lax.dot_general(qs, k.astype(BF16), (((1,), (1,)), ((), ())), preferred_element_type=F32) * dm
        s_old = s_ref[...]
        inter = jnp.dot(qs, s_old.astype(BF16), preferred_element_type=F32) * cross
        kv = lax.dot_general((k * tail).astype(BF16), v, (((0,), (0,)), ((), ())), preferred_element_type=F32)
        ab = a.astype(BF16)
        for hh in range(R_HEADS):
            vs = slice(hh * G_V, (hh + 1) * G_V)
            o = inter[hh * c:(hh + 1) * c] + jnp.dot(ab[hh * c:(hh + 1) * c], v[:, vs],
                                                     preferred_element_type=F32)
            o_ref[rows, vs] = _rms(o, nw) * _silu(gr_ref[rows, vs])
            ks = slice(hh * R_QK, (hh + 1) * R_QK)
            s_ref[ks, :] = gc[ks] * s_old[ks] + kv[ks, vs]
        return carry

    lax.fori_loop(0, n_chunks, chunk, 0)

    @pl.when(pl.program_id(0) == pl.num_programs(0) - 1)
    def _():
        s_out_ref[...] = s_ref[...]


def _ret_log_gamma():
    return jnp.log(1.0 - jnp.exp2(-5.0 - jnp.arange(R_HEADS, dtype=F32)))


def ret_prompt(h, qr, kr, norm_w):
    t = h.shape[0]
    tb = _tile(t, 512)
    c = CHUNK
    lg = _ret_log_gamma()
    idx = jnp.arange(c, dtype=F32)
    rel = idx[:, None] - idx[None, :]
    dmat = jnp.where(rel[None] >= 0, jnp.exp(jnp.maximum(rel, 0.0)[None] * lg[:, None, None]), 0.0)
    dmat = dmat.reshape(R_HEADS * c, c)
    cross = jnp.exp((idx + 1.0)[None, :] * lg[:, None]).reshape(R_HEADS * c, 1)
    cross = jnp.broadcast_to(cross, (R_HEADS * c, G_V))
    tail = jnp.exp((c - 1.0 - idx)[None, :] * lg[:, None])
    tail = jnp.repeat(tail.T, R_QK, axis=1)
    gc = jnp.broadcast_to(jnp.repeat(jnp.exp(c * lg), R_QK)[:, None], (HK, G_V))
    _, _, hm = _gla_constants()
    full = lambda a: pl.BlockSpec(a.shape, lambda i: (0,) * a.ndim)
    o, s = pl.pallas_call(
        functools.partial(_ret_kernel, n_chunks=tb // c),
        grid=(t // tb,),
        in_specs=[pl.BlockSpec((tb, HK), lambda i: (i, 0)),
                  pl.BlockSpec((tb, HK), lambda i: (i, 0)),
                  pl.BlockSpec((tb, HV), lambda i: (i, VR // 4)),
                  pl.BlockSpec((tb, HV), lambda i: (i, GR // 4)),
                  full(norm_w), full(dmat), full(cross), full(tail), full(gc), full(hm)],
        out_specs=[pl.BlockSpec((tb, HV), lambda i: (i, 0)),
                   pl.BlockSpec((HK, G_V), lambda i: (0, 0))],
        out_shape=[jax.ShapeDtypeStruct((t, HV), F32), jax.ShapeDtypeStruct((HK, G_V), F32)],
        scratch_shapes=[pltpu.VMEM((HK, G_V), F32)],
        compiler_params=_cparams(("arbitrary",)))(qr, kr, h, h, norm_w, dmat, cross, tail, gc, hm)
    return o, s.reshape(R_HEADS, R_QK, G_V)


def _step_kernel(s_ref, q_ref, k_ref, d_ref, v_ref, gate_ref, nw_ref, s_out_ref, o_ref, *, is_gla, q_scale):
    d = d_ref[...]
    if is_gla:
        d = _log_sigmoid(d) / G_NORMALIZER
    s_new = jnp.exp(d) * s_ref[...] + k_ref[...] * v_ref[...]
    s_out_ref[...] = s_new
    o = jnp.sum((q_ref[...] * q_scale) * s_new, axis=1, keepdims=True)
    o_ref[...] = _rms(o, nw_ref[...]) * _silu(gate_ref[...])


def recurrent_step(state, q, k, dlog, v, gate, norm_w, is_gla, q_scale):
    b = state.shape[0]
    col = lambda a: a.reshape(b, G_HEADS, G_QK, 1)
    rowv = lambda a: a.reshape(b, G_HEADS, 1, G_V)
    cspec = pl.BlockSpec((None, G_HEADS, G_QK, 1), lambda i: (i, 0, 0, 0))
    rspec = pl.BlockSpec((None, G_HEADS, 1, G_V), lambda i: (i, 0, 0, 0))
    sspec = pl.BlockSpec((None, G_HEADS, G_QK, G_V), lambda i: (i, 0, 0, 0))
    s_new, o = pl.pallas_call(
        functools.partial(_step_kernel, is_gla=is_gla, q_scale=q_scale),
        grid=(b,),
        in_specs=[sspec, cspec, cspec, cspec, rspec, rspec, pl.BlockSpec((1, G_V), lambda i: (0, 0))],
        out_specs=[sspec, rspec],
        out_shape=[jax.ShapeDtypeStruct(state.shape, F32), jax.ShapeDtypeStruct((b, G_HEADS, 1, G_V), F32)],
        compiler_params=_cparams(("parallel",)))(state, col(q), col(k), col(dlog), rowv(v), rowv(gate), norm_w)
    return o.reshape(b, HV), s_new


def _mixer_out_kernel(oa_ref, og_ref, or_ref, za_ref, zb_ref, zc_ref, x_ref, wpa_ref, wpb_ref, wpc_ref,
                      wo_ref, g_ref, b_ref, y_ref, *, alpha):
    def branch(o_ref, z_ref, w_ref):
        return _sigmoid(z_ref[...]) * jnp.dot(o_ref[...].astype(BF16), w_ref[...], preferred_element_type=F32)

    merged = branch(oa_ref, za_ref, wpa_ref) + branch(og_ref, zb_ref, wpb_ref) + branch(or_ref, zc_ref, wpc_ref)
    y = alpha * x_ref[...] + jnp.dot(merged.astype(BF16), wo_ref[...], preferred_element_type=F32)
    y_ref[...] = _layer_norm(y, g_ref[...], b_ref[...])


def mixer_out(oa, og, orr, h, x, wpa, wpb, wpc, wo, g, b, alpha):
    m = x.shape[0]
    tm = _tile(m, 256)
    bw = oa.shape[1]
    ospec = pl.BlockSpec((tm, bw), lambda i: (i, 0))
    zspec = lambda blk: pl.BlockSpec((tm, D_MODEL), lambda i: (i, blk // 8))
    xspec = pl.BlockSpec((tm, D_MODEL), lambda i: (i, 0))
    full = lambda a: pl.BlockSpec(a.shape, lambda i: (0,) * a.ndim)
    return pl.pallas_call(
        functools.partial(_mixer_out_kernel, alpha=alpha),
        grid=(m // tm,),
        in_specs=[ospec, ospec, ospec, zspec(ZA), zspec(ZB), zspec(ZC), xspec,
                  full(wpa), full(wpb), full(wpc), full(wo), full(g), full(b)],
        out_specs=xspec,
        out_shape=jax.ShapeDtypeStruct((m, D_MODEL), F32),
        compiler_params=_cparams(("parallel",)))(oa, og, orr, h, h, h, x, wpa, wpb, wpc, wo, g, b)


def _ffn_up_kernel(x_ref, wg_ref, wu_ref, h_ref):
    x = x_ref[...].astype(BF16)
    a = jnp.dot(x, wg_ref[...], preferred_element_type=F32)
    u = jnp.dot(x, wu_ref[...], preferred_element_type=F32)
    h_ref[...] = (_silu(a) * u).astype(h_ref.dtype)


def _ffn_down_kernel(h_ref, wd_ref, x_ref, g_ref, b_ref, y_ref, *, alpha):
    y = alpha * x_ref[...] + jnp.dot(h_ref[...], wd_ref[...], preferred_element_type=F32)
    y_ref[...] = _layer_norm(y, g_ref[...], b_ref[...])


def dense_ffn(x, wg, wu, wd, g, b, alpha):
    m = x.shape[0]
    f = wg.shape[1]
    tm, tf = _tile(m, 512), _tile(f, 1408)
    hmid = pl.pallas_call(
        _ffn_up_kernel, grid=(m // tm, f // tf),
        in_specs=[pl.BlockSpec((tm, D_MODEL), lambda i, j: (i, 0)),
                  pl.BlockSpec((D_MODEL, tf), lambda i, j: (0, j)),
                  pl.BlockSpec((D_MODEL, tf), lambda i, j: (0, j))],
        out_specs=pl.BlockSpec((tm, tf), lambda i, j: (i, j)),
        out_shape=jax.ShapeDtypeStruct((m, f), BF16),
        compiler_params=_cparams(("parallel", "parallel")))(x, wg, wu)
    full = lambda a: pl.BlockSpec(a.shape, lambda i: (0,) * a.ndim)
    xspec = pl.BlockSpec((tm, D_MODEL), lambda i: (i, 0))
    return pl.pallas_call(
        functools.partial(_ffn_down_kernel, alpha=alpha), grid=(m // tm,),
        in_specs=[pl.BlockSpec((tm, f), lambda i: (i, 0)), full(wd), xspec, full(g), full(b)],
        out_specs=xspec,
        out_shape=jax.ShapeDtypeStruct((m, D_MODEL), F32),
        compiler_params=_cparams(("parallel",)))(hmid, wd, x, g, b)


def _router_kernel(x_ref, wh_ref, wl_ref, comb_ref):
    x = x_ref[...]
    xh = x.astype(BF16)
    xl = (x - xh.astype(F32)).astype(BF16)
    wh, wl = wh_ref[...], wl_ref[...]
    logits = (jnp.dot(xh, wh, preferred_element_type=F32) + jnp.dot(xh, wl, preferred_element_type=F32)
              + jnp.dot(xl, wh, preferred_element_type=F32))
    lane = lax.broadcasted_iota(jnp.int32, logits.shape, 1)
    logits = jnp.where(lane < N_EXPERTS, logits, NEG_BIG)
    m1 = jnp.max(logits, axis=-1, keepdims=True)
    i1 = jnp.min(jnp.where(logits == m1, lane, LANES), axis=-1, keepdims=True)
    rest = jnp.where(lane == i1, NEG_BIG, logits)
    m2 = jnp.max(rest, axis=-1, keepdims=True)
    i2 = jnp.min(jnp.where(rest == m2, lane, LANES), axis=-1, keepdims=True)
    e2 = jnp.exp(m2 - m1)
    w1 = 1.0 / (1.0 + e2)
    w2 = e2 / (1.0 + e2)
    comb_ref[...] = jnp.where(lane == i1, w1, jnp.where(lane == i2, w2, 0.0))


def router(x, w_router):
    m = x.shape[0]
    tm = _tile(m, 512)
    wpad = jnp.zeros((D_MODEL, LANES), F32).at[:, :N_EXPERTS].set(w_router)
    wh = wpad.astype(BF16)
    wl = (wpad - wh.astype(F32)).astype(BF16)
    full = lambda a: pl.BlockSpec(a.shape, lambda i: (0,) * a.ndim)
    return pl.pallas_call(
        _router_kernel, grid=(m // tm,),
        in_specs=[pl.BlockSpec((tm, D_MODEL), lambda i: (i, 0)), full(wh), full(wl)],
        out_specs=pl.BlockSpec((tm, LANES), lambda i: (i, 0)),
        out_shape=jax.ShapeDtypeStruct((m, LANES), F32),
        compiler_params=_cparams(("parallel",)))(x, wh, wl)


def _moe_kernel(x_ref, comb_ref, wg_ref, wu_ref, wd_ref, g_ref, b_ref, y_ref, acc_ref, *, alpha):
    e, f = pl.program_id(1), pl.program_id(2)

    @pl.when((e == 0) & (f == 0))
    def _():
        acc_ref[...] = jnp.zeros(acc_ref.shape, F32)

    x = x_ref[...].astype(BF16)
    a = jnp.dot(x, wg_ref[...], preferred_element_type=F32)
    u = jnp.dot(x, wu_ref[...], preferred_element_type=F32)
    hmid = (_silu(a) * u).astype(BF16)
    comb = comb_ref[...]
    lane = lax.broadcasted_iota(jnp.int32, comb.shape, 1)
    ce = jnp.sum(jnp.where(lane == e, comb, 0.0), axis=-1, keepdims=True)
    acc_ref[...] += ce * jnp.dot(hmid, wd_ref[...], preferred_element_type=F32)

    @pl.when((e == pl.num_programs(1) - 1) & (f == pl.num_programs(2) - 1))
    def _():
        y_ref[...] = _layer_norm(alpha * x_ref[...] + acc_ref[...], g_ref[...], b_ref[...])


def moe_ffn(x, comb, wg, wu, wd, g, b, alpha):
    m = x.shape[0]
    ne, _, f = wg.shape
    tm, tf = _tile(m, 1024), _tile(f, 512)
    xspec = pl.BlockSpec((tm, D_MODEL), lambda i, e, j: (i, 0))
    full = lambda a: pl.BlockSpec(a.shape, lambda i, e, j: (0,) * a.ndim)
    return pl.pallas_call(
        functools.partial(_moe_kernel, alpha=alpha), grid=(m // tm, ne, f // tf),
        in_specs=[xspec, pl.BlockSpec((tm, LANES), lambda i, e, j: (i, 0)),
                  pl.BlockSpec((None, D_MODEL, tf), lambda i, e, j: (e, 0, j)),
                  pl.BlockSpec((None, D_MODEL, tf), lambda i, e, j: (e, 0, j)),
                  pl.BlockSpec((None, tf, D_MODEL), lambda i, e, j: (e, j, 0)),
                  full(g), full(b)],
        out_specs=xspec,
        out_shape=jax.ShapeDtypeStruct((m, D_MODEL), F32),
        scratch_shapes=[pltpu.VMEM((tm, D_MODEL), F32)],
        compiler_params=_cparams(("parallel", "arbitrary", "arbitrary")))(x, comb, wg, wu, wd, g, b)


def _layer_weights(l, w_in, lam_q1, lam_k1, lam_q2, lam_k2, subln_w, w_gla_up, b_gla_up, gla_norm_w,
                   ret_norm_w, w_pa, w_pb, w_pc, w_out, ln1_g, ln1_b, ln2_g, ln2_b):
    wl = w_in[l]
    w_main = jnp.concatenate([wl[:, a:b] for a, b in PACK_ORDER], axis=1).astype(BF16)
    w_lrg = jnp.zeros((D_MODEL, LANES), F32).at[:, :G_RANK].set(wl[:, LRG_OFF:LRG_OFF + G_RANK]).astype(BF16)
    w_up = jnp.zeros((LANES, HK), F32).at[:G_RANK].set(w_gla_up[l]).astype(BF16)
    lamvec = jnp.zeros((8, LANES), F32)
    for r, vec in enumerate((lam_q1, lam_k1, lam_q2, lam_k2)):
        lamvec = lamvec.at[r, :A_QK].set(vec[l].astype(F32))
    row = lambda a: a[l].reshape(1, -1)
    return dict(w_main=w_main, w_lrg=w_lrg, w_up=w_up, b_up=row(b_gla_up), lamvec=lamvec,
                subln=row(subln_w), gla_nw=row(gla_norm_w), ret_nw=row(ret_norm_w),
                wpa=w_pa[l].astype(BF16), wpb=w_pb[l].astype(BF16), wpc=w_pc[l].astype(BF16),
                wo=w_out[l].astype(BF16), ln1_g=row(ln1_g), ln1_b=row(ln1_b),
                ln2_g=row(ln2_g), ln2_b=row(ln2_b), lam_init=0.8 - 0.6 * math.exp(-0.3 * l))


def _project(x, pos, lw, q_dtype, q_scale):
    h = matmul(x, lw['w_main'])
    lrg = matmul(x, lw['w_lrg'], tn_cap=LANES)
    qa, ka = rotary(h, QA, A_HEADS * LANES, _rope_tables(pos), ROPE_DIM // 2, q_scale, 1.0, q_dtype)
    qr, kr = rotary(h, QR, HK, _ret_tables(pos), 1, 1.0, R_QK ** -0.5, F32)
    return h, lrg, qa, ka, qr, kr


def _channel_mix(x1, l, lw, ffn_w, alpha):
    if l % 2 == 0:
        wg, wu, wd = ffn_w['dense'][l // 2]
        return dense_ffn(x1, wg, wu, wd, lw['ln2_g'], lw['ln2_b'], alpha)
    w_r, wg, wu, wd = ffn_w['moe'][l // 2]
    comb = router(x1, w_r)
    return moe_ffn(x1, comb, wg, wu, wd, lw['ln2_g'], lw['ln2_b'], alpha)


def kernel(x_prompt, x_sample, cache_k, cache_v, state_gla, state_ret, page_table, w_in, lam_q1, lam_k1, lam_q2, lam_k2, subln_w, w_gla_up, b_gla_up, gla_norm_w, ret_norm_w, w_pa, w_pb, w_pc, w_out, ln1_g, ln1_b, w_ff_gate, w_ff_up, w_ff_down, w_router, w_exp_gate, w_exp_up, w_exp_down, ln2_g, ln2_b):
    bp, tp, _ = x_prompt.shape
    bs, ts, _ = x_sample.shape
    assert bp == 1 and ts == 1
    depth = w_in.shape[0]
    alpha = (2 * depth) ** 0.25
    n_pool, page = cache_k.shape[1], cache_k.shape[2]
    past_len = page_table.shape[1] * page
    pos_p = jnp.arange(tp, dtype=jnp.int32)
    pos_s = jnp.full((bs,), past_len, jnp.int32)
    ffn_w = dict(
        dense=[(w_ff_gate[i].astype(BF16), w_ff_up[i].astype(BF16), w_ff_down[i].astype(BF16))
               for i in range(w_ff_gate.shape[0])],
        moe=[(w_router[i], w_exp_gate[i].astype(BF16), w_exp_up[i].astype(BF16), w_exp_down[i].astype(BF16))
             for i in range(w_router.shape[0])])
    ck = cache_k.reshape(depth, n_pool, page * A_HEADS, LANES)
    cv = cache_v.reshape(depth, n_pool, page * A_HEADS, LANES)
    lg_col = jnp.broadcast_to(jnp.repeat(_ret_log_gamma(), R_QK)[None, :], (bs, HK))

    yp = x_prompt.reshape(tp, D_MODEL)
    ys = x_sample.reshape(bs, D_MODEL)
    outs = {n: [] for n in ('kp', 'vp', 'gp', 'rp', 'ks', 'vs', 'gs', 'rs')}
    cols = lambda a, blk, n: a[:, blk * LANES:(blk + n) * LANES]
    for l in range(depth):
        lw = _layer_weights(l, w_in, lam_q1, lam_k1, lam_q2, lam_k2, subln_w, w_gla_up, b_gla_up,
                            gla_norm_w, ret_norm_w, w_pa, w_pb, w_pc, w_out, ln1_g, ln1_b, ln2_g, ln2_b)
        h, lrg, qa, ka, qr, kr = _project(yp, pos_p, lw, BF16, A_QK ** -0.5 * LOG2E)
        oa = flash_diff_attention(qa, ka, h, lw['lamvec'], lw['subln'], lw['lam_init'])
        og, s_gla = gla_prompt(h, lrg, lw['w_up'], lw['b_up'], lw['gla_nw'])
        orr, s_ret = ret_prompt(h, qr, kr, lw['ret_nw'])
        x1 = mixer_out(oa, og, orr, h, yp, lw['wpa'], lw['wpb'], lw['wpc'], lw['wo'],
                       lw['ln1_g'], lw['ln1_b'], alpha)
        yp = _channel_mix(x1, l, lw, ffn_w, alpha)
        outs['kp'].append(ka.reshape(1, tp, A_HEADS, 2 * A_QK))
        outs['vp'].append(cols(h, VA, 4).reshape(1, tp, A_HEADS, 2 * A_QK))
        outs['gp'].append(s_gla[None])
        outs['rp'].append(s_ret[None])
        h, lrg, qa, ka, qr, kr = _project(ys, pos_s, lw, F32, A_QK ** -0.5)
        va = cols(h, VA, 4)
        oa = decode_diff_attention(qa, ka, va, ck, cv, l, page_table, lw['lamvec'], lw['subln'],
                                   lw['lam_init'])
        gk = matmul(lrg, lw['w_up'], bias=lw['b_up'])
        og, s_gla = recurrent_step(state_gla[l], cols(h, QG, 2), cols(h, KG, 2), gk, cols(h, VG, 4),
                                   cols(h, RG, 4), lw['gla_nw'], True, G_QK ** -0.5)
        orr, s_ret = recurrent_step(state_ret[l], qr, kr, lg_col, cols(h, VR, 4), cols(h, GR, 4),
                                    lw['ret_nw'], False, 1.0)
        x1 = mixer_out(oa, og, orr, h, ys, lw['wpa'], lw['wpb'], lw['wpc'], lw['wo'],
                       lw['ln1_g'], lw['ln1_b'], alpha)
        ys = _channel_mix(x1, l, lw, ffn_w, alpha)
        outs['ks'].append(ka.reshape(bs, 1, A_HEADS, 2 * A_QK))
        outs['vs'].append(va.reshape(bs, 1, A_HEADS, 2 * A_QK))
        outs['gs'].append(s_gla)
        outs['rs'].append(s_ret)

    st = lambda n: jnp.stack(outs[n])
    return (yp.reshape(bp, tp, D_MODEL), ys.reshape(bs, ts, D_MODEL), st('kp'), st('vp'), st('gp'),
            st('rp'), st('ks'), st('vs'), st('gs'), st('rs'))
```

```python
import functools
import math

import numpy as np
import jax
import jax.numpy as jnp
from jax import lax
from jax.experimental import pallas as pl
from jax.experimental.pallas import tpu as pltpu

F32 = jnp.float32
BF16 = jnp.bfloat16

D_MODEL = 1024
A_HEADS = 4
A_QK = 64
ROPE_DIM = 16
ROPE_THETA = 500000.0
G_HEADS = 4
G_QK = 64
G_V = 128
G_RANK = 16
G_NORMALIZER = 16.0
R_HEADS = 4
R_QK = 64
R_ANGLE_BASE = 10000.0
CHUNK = 64
N_EXPERTS = 8
NORM_EPS = 1e-5
NEG_BIG = -1e30
LOG2E = math.log2(math.e)
FLASH_TQ, FLASH_TK = 1024, 1024
FLASH_RC = 32
FLASH_PARTS = 2
DECODE_PAGES = 8
MOE_BLOCK = 1024
MOE_ROWS = 320
MOE_MIN_ROUTED = 256

LANES = 128
VMEM_LIMIT = 56 * 1024 * 1024

ZA, ZB, ZC, QA, KA, VA, VG, RG, VR, GR, QG, KG, QR, KR = 0, 8, 16, 24, 28, 32, 36, 40, 44, 48, 52, 54, 56, 58
PACK_ORDER = ((4624, 7696), (0, 1536), (2048, 2560), (2576, 3088), (3600, 4112), (4112, 4624),
              (1536, 2048), (3088, 3600))
LRG_OFF = 2560


def _cparams(sem):
    return pltpu.CompilerParams(dimension_semantics=sem, vmem_limit_bytes=VMEM_LIMIT)


def _tile(n, cap):
    c = min(n, cap)
    while n % c:
        c -= 1
    return c


def _layer_norm(y, g, b):
    mu = jnp.mean(y, axis=-1, keepdims=True)
    d = y - mu
    var = jnp.mean(d * d, axis=-1, keepdims=True)
    return d * lax.rsqrt(var + NORM_EPS) * g + b


def _rms(o, w):
    return o * lax.rsqrt(jnp.mean(o * o, axis=-1, keepdims=True) + NORM_EPS) * w


def _silu(x):
    return x * (1.0 / (1.0 + jnp.exp(-x)))


def _sigmoid(x):
    return 1.0 / (1.0 + jnp.exp(-x))


def _log_sigmoid(x):
    return jnp.minimum(x, 0.0) - jnp.log(1.0 + jnp.exp(-jnp.abs(x)))


def _split3(x):
    a = x.astype(BF16)
    r = x - a.astype(F32)
    b = r.astype(BF16)
    c = (r - b.astype(F32)).astype(BF16)
    return a, b, c


def _mm_kernel(x_ref, w_ref, o_ref):
    o_ref[...] = jnp.dot(x_ref[...].astype(BF16), w_ref[...],
                         preferred_element_type=F32).astype(o_ref.dtype)


def _mm_bias_kernel(x_ref, w_ref, b_ref, o_ref):
    o_ref[...] = (jnp.dot(x_ref[...].astype(BF16), w_ref[...],
                          preferred_element_type=F32) + b_ref[...]).astype(o_ref.dtype)


def matmul(x, w, bias=None, out_dtype=F32, tm_cap=1024, tn_cap=768):
    m, k = x.shape
    n = w.shape[1]
    tm, tn = _tile(m, tm_cap), _tile(n, tn_cap)
    in_specs = [pl.BlockSpec((tm, k), lambda i, j: (i, 0)),
                pl.BlockSpec((k, tn), lambda i, j: (0, j))]
    args = [x, w]
    kern = _mm_kernel
    if bias is not None:
        in_specs.append(pl.BlockSpec((1, tn), lambda i, j: (0, j)))
        args.append(bias)
        kern = _mm_bias_kernel
    return pl.pallas_call(
        kern, grid=(m // tm, n // tn), in_specs=in_specs,
        out_specs=pl.BlockSpec((tm, tn), lambda i, j: (i, j)),
        out_shape=jax.ShapeDtypeStruct((m, n), out_dtype),
        compiler_params=_cparams(("parallel", "parallel")))(*args)


def _rot_kernel(x_ref, c_ref, s1_ref, s2_ref, oq_ref, ok_ref, *, shift, q_scale, k_scale):
    c, s1, s2 = c_ref[...], s1_ref[...], s2_ref[...]
    nq = oq_ref.shape[1] // LANES
    nk = ok_ref.shape[1] // LANES
    for b in range(nq + nk):
        x = x_ref[:, b * LANES:(b + 1) * LANES]
        y = x * c + pltpu.roll(x, LANES - shift, 1) * s1 + pltpu.roll(x, shift, 1) * s2
        if b < nq:
            oq_ref[:, b * LANES:(b + 1) * LANES] = (y * q_scale).astype(oq_ref.dtype)
        else:
            ok_ref[:, (b - nq) * LANES:(b - nq + 1) * LANES] = (y * k_scale).astype(ok_ref.dtype)


def rotary(h, col_blk, width, tables, shift, q_scale, k_scale, q_dtype):
    m = h.shape[0]
    tm = _tile(m, 512)
    blk = col_blk * LANES // (2 * width)
    tspec = pl.BlockSpec((tm, LANES), lambda i: (i, 0))
    return pl.pallas_call(
        functools.partial(_rot_kernel, shift=shift, q_scale=q_scale, k_scale=k_scale),
        grid=(m // tm,),
        in_specs=[pl.BlockSpec((tm, 2 * width), lambda i: (i, blk)), tspec, tspec, tspec],
        out_specs=[pl.BlockSpec((tm, width), lambda i: (i, 0)),
                   pl.BlockSpec((tm, width), lambda i: (i, 0))],
        out_shape=[jax.ShapeDtypeStruct((m, width), q_dtype),
                   jax.ShapeDtypeStruct((m, width), F32)],
        compiler_params=_cparams(("parallel",)))(h, *tables)


def _rope_tables(pos):
    half = ROPE_DIM // 2
    inv = ROPE_THETA ** (-jnp.arange(half, dtype=F32) * 2.0 / ROPE_DIM)
    ang = pos.astype(F32)[:, None] * inv[None, :]
    cos, sin = jnp.cos(ang), jnp.sin(ang)
    lane = np.arange(LANES) % A_QK
    fi = lane % half
    cos_l, sin_l = cos[:, fi], sin[:, fi]
    in_rot = (lane < ROPE_DIM)[None, :]
    lo = (lane < half)[None, :]
    c = jnp.where(in_rot, cos_l, 1.0)
    s1 = jnp.where(lo, -sin_l, 0.0)
    s2 = jnp.where(in_rot & ~lo, sin_l, 0.0)
    return c, s1, s2


def _ret_tables(pos):
    inv = 1.0 / (R_ANGLE_BASE ** jnp.linspace(0.0, 1.0, R_QK // 2, dtype=F32))
    ang = pos.astype(F32)[:, None] * inv[None, :]
    cos, sin = jnp.cos(ang), jnp.sin(ang)
    lane = np.arange(LANES) % R_QK
    cos_l, sin_l = cos[:, lane // 2], sin[:, lane // 2]
    even = (lane % 2 == 0)[None, :]
    return cos_l, jnp.where(even, -sin_l, 0.0), jnp.where(even, 0.0, sin_l)


def _lam_from(lam_ref, lam_init):
    v = lam_ref[...]
    t1 = jnp.sum(v[0:1] * v[1:2], axis=-1, keepdims=True)
    t2 = jnp.sum(v[2:3] * v[3:4], axis=-1, keepdims=True)
    return jnp.exp(t1) - jnp.exp(t2) + lam_init


def _flash_kernel(lam_ref, q_ref, k_ref, v_ref, w_ref, o_ref, m_ref, acc_ref, s_ref, p_ref, *,
                  tq, tk, rc, lam_init):
    i, j = pl.program_id(1), pl.program_id(2)

    @pl.when(j == 0)
    def _():
        m_ref[...] = jnp.full(m_ref.shape, NEG_BIG, F32)
        acc_ref[...] = jnp.zeros(acc_ref.shape, F32)

    def block(masked):
        k = k_ref[...].astype(BF16)
        v = jnp.concatenate([v_ref[...].astype(BF16), jnp.ones((tk, LANES), BF16)], axis=1)
        q = q_ref[...]
        lane = lax.broadcasted_iota(jnp.int32, q.shape, 1)
        zero = jnp.zeros_like(q)
        def chunk(c, r):
            r0 = r * rc
            rows = slice(r0, r0 + rc)
            s = s_ref[c, rows, :]
            if masked:
                row = i * tq + r0 + lax.broadcasted_iota(jnp.int32, (rc, tk), 0)
                col = j * tk + lax.broadcasted_iota(jnp.int32, (rc, tk), 1)
                s = jnp.where(col <= row, s, NEG_BIG)
            m_prev = m_ref[c, rows, :]
            m_new = jnp.maximum(m_prev, jnp.max(s, axis=-1, keepdims=True))
            p = jnp.exp2(s - jnp.concatenate([m_new] * (tk // LANES), axis=1))
            alpha = jnp.exp2(m_prev - m_new)
            acc_ref[c, rows, :] = jnp.concatenate([alpha, alpha], axis=1) * acc_ref[c, rows, :]
            m_ref[c, rows, :] = m_new
            p_ref[c, rows, :] = p.astype(BF16)

        nparts = FLASH_PARTS if tq % (FLASH_PARTS * rc) == 0 else 1
        pr = tq // nparts
        units = [(c, h) for c in range(2) for h in range(nparts)]

        def scores(c, h):
            rows = slice(h * pr, (h + 1) * pr)
            qc = jnp.where((lane < A_QK) if c == 0 else (lane >= A_QK), q, zero)[rows]
            s_ref[c, rows, :] = lax.dot_general(qc, k, (((1,), (1,)), ((), ())), preferred_element_type=F32)

        def sweep(c, h):
            for r in range(h * pr // rc, (h + 1) * pr // rc):
                chunk(c, r)

        def pv(c, h):
            rows = slice(h * pr, (h + 1) * pr)
            acc_ref[c, rows, :] += jnp.dot(p_ref[c, rows, :], v, preferred_element_type=F32)

        for n in range(len(units) + 2):
            if n < len(units):
                scores(*units[n])
            if 1 <= n <= len(units):
                sweep(*units[n - 1])
            if n >= 2:
                pv(*units[n - 2])

    visible = (j + 1) * tk - 1 <= i * tq
    needed = j * tk <= i * tq + tq - 1
    pl.when(visible)(functools.partial(block, False))
    pl.when(needed & jnp.logical_not(visible))(functools.partial(block, True))

    @pl.when(j == pl.num_programs(2) - 1)
    def _():
        lam = _lam_from(lam_ref, lam_init)
        o = (acc_ref[0, :, :LANES] / acc_ref[0, :, LANES:]
             - lam * (acc_ref[1, :, :LANES] / acc_ref[1, :, LANES:]))
        o_ref[...] = _rms(o, w_ref[...]) * (1.0 - lam_init)


def flash_diff_attention(q, k, h, lamvec, subln_w, lam_init, tq_cap=FLASH_TQ, tk_cap=FLASH_TK, rc=FLASH_RC):
    t = q.shape[0]
    tq, tk = _tile(t, tq_cap), _tile(t, tk_cap)
    rc = _tile(tq, rc)
    nq, nk = t // tq, t // tk

    def kv_idx(off):
        return lambda hh, i, j: (jnp.minimum(j, (i * tq + tq - 1) // tk), off + hh)

    return pl.pallas_call(
        functools.partial(_flash_kernel, tq=tq, tk=tk, rc=rc, lam_init=lam_init),
        grid=(A_HEADS, nq, nk),
        in_specs=[pl.BlockSpec((8, LANES), lambda hh, i, j: (0, 0)),
                  pl.BlockSpec((tq, LANES), lambda hh, i, j: (i, hh)),
                  pl.BlockSpec((tk, LANES), kv_idx(0)),
                  pl.BlockSpec((tk, LANES), kv_idx(VA)),
                  pl.BlockSpec((1, LANES), lambda hh, i, j: (0, 0))],
        out_specs=pl.BlockSpec((tq, LANES), lambda hh, i, j: (i, hh)),
        out_shape=jax.ShapeDtypeStruct((t, A_HEADS * LANES), F32),
        scratch_shapes=[pltpu.VMEM((2, tq, LANES), F32), pltpu.VMEM((2, tq, 2 * LANES), F32),
                        pltpu.VMEM((2, tq, tk), F32), pltpu.VMEM((2, tq, tk), BF16)],
        compiler_params=_cparams(("parallel", "parallel", "arbitrary")))(lamvec, q, k, h, subln_w)


def _decode_kernel(pt_ref, lam_ref, q_ref, kn_ref, vn_ref, w_ref, *rest, pp, lam_init):
    k_refs, v_refs = rest[:pp], rest[pp:2 * pp]
    o_ref, m_ref, l_ref, acc_ref = rest[2 * pp:]
    c = pl.program_id(1)
    rows = 2 * A_HEADS
    prow = k_refs[0].shape[0]

    @pl.when(c == 0)
    def _():
        m_ref[...] = jnp.full(m_ref.shape, NEG_BIG, F32)
        l_ref[...] = jnp.zeros(l_ref.shape, F32)
        acc_ref[...] = jnp.zeros(acc_ref.shape, F32)

    rid = lax.broadcasted_iota(jnp.int32, (rows, LANES), 0)
    lid = lax.broadcasted_iota(jnp.int32, (rows, LANES), 1)
    qb = jnp.where(lid // A_QK == rid % 2, q_ref[...], 0.0).astype(BF16)
    s = jnp.concatenate(
        [lax.dot_general(qb, k_refs[p][...].astype(BF16), (((1,), (1,)), ((), ())),
                         preferred_element_type=F32) for p in range(pp)], axis=-1)
    srow = lax.broadcasted_iota(jnp.int32, s.shape, 0)
    scol = lax.broadcasted_iota(jnp.int32, s.shape, 1)
    s = jnp.where(scol % A_HEADS == srow // 2, s, NEG_BIG)
    m_prev = m_ref[...]
    m_new = jnp.maximum(m_prev, jnp.max(s, axis=-1, keepdims=True))
    p_ = jnp.exp(s - m_new)
    alpha = jnp.exp(m_prev - m_new)
    l_new = alpha * l_ref[...] + jnp.sum(p_, axis=-1, keepdims=True)
    acc = alpha * acc_ref[...]
    for p in range(pp):
        acc = acc + jnp.dot(p_[:, p * prow:(p + 1) * prow].astype(BF16), v_refs[p][...].astype(BF16),
                            preferred_element_type=F32)
    m_ref[...] = m_new
    l_ref[...] = l_new
    acc_ref[...] = acc

    @pl.when(c == pl.num_programs(1) - 1)
    def _():
        kn = kn_ref[...].astype(BF16).astype(F32)
        vn = vn_ref[...].astype(BF16).astype(F32)
        s_self = jnp.sum(qb.astype(F32) * kn, axis=-1, keepdims=True)
        m_fin = jnp.maximum(m_new, s_self)
        a2 = jnp.exp(m_new - m_fin)
        p_self = jnp.exp(s_self - m_fin)
        l_fin = a2 * l_new + p_self
        acc_fin = a2 * acc + p_self.astype(BF16).astype(F32) * vn
        on = acc_fin / l_fin
        lam = _lam_from(lam_ref, lam_init)
        w = w_ref[...]
        for hh in range(A_HEADS):
            o = on[2 * hh:2 * hh + 1] - lam * on[2 * hh + 1:2 * hh + 2]
            o_ref[hh:hh + 1, :] = _rms(o, w) * (1.0 - lam_init)


def decode_diff_attention(q, k_new, v_new, cache_k, cache_v, layer, page_table, lamvec, subln_w, lam_init):
    b, n_pages = page_table.shape
    prow = cache_k.shape[2]
    pp = _tile(n_pages, DECODE_PAGES)
    rows = 2 * A_HEADS

    def row(bb, c, pt):
        return (bb, 0, 0)

    def page_idx(p):
        return lambda bb, c, pt: (layer, pt[bb, c * pp + p], 0, 0)

    row_spec = pl.BlockSpec((None, rows, LANES), row)
    page_specs = [pl.BlockSpec((None, None, prow, LANES), page_idx(p)) for p in range(pp)]
    grid_spec = pltpu.PrefetchScalarGridSpec(
        num_scalar_prefetch=1, grid=(b, n_pages // pp),
        in_specs=[pl.BlockSpec((8, LANES), lambda bb, c, pt: (0, 0)), row_spec, row_spec, row_spec,
                  pl.BlockSpec((1, LANES), lambda bb, c, pt: (0, 0))] + page_specs + page_specs,
        out_specs=pl.BlockSpec((None, A_HEADS, LANES), row),
        scratch_shapes=[pltpu.VMEM((rows, 1), F32), pltpu.VMEM((rows, 1), F32),
                        pltpu.VMEM((rows, LANES), F32)])
    r8 = lambda a: jnp.repeat(a.reshape(b, A_HEADS, LANES), 2, axis=1)
    out = pl.pallas_call(
        functools.partial(_decode_kernel, pp=pp, lam_init=lam_init),
        grid_spec=grid_spec,
        out_shape=jax.ShapeDtypeStruct((b, A_HEADS, LANES), F32),
        compiler_params=_cparams(("parallel", "arbitrary")))(
            page_table, lamvec, r8(q), r8(k_new), r8(v_new), subln_w,
            *([cache_k] * pp), *([cache_v] * pp))
    return out.reshape(b, A_HEADS * LANES)


HK = G_HEADS * G_QK
HV = G_HEADS * G_V
GLA_LEVELS = (1, 2, 4, 8, 16, 32)


def _gla_constants():
    c = CHUNK
    t = np.arange(c)[:, None]
    i = np.arange(c)[None, :]
    mats = []
    for s in GLA_LEVELS[1:] + (c,):
        mats.append(((i // s == t // s) & (i <= t)).astype(np.float32))
    for s in GLA_LEVELS[1:] + (c,):
        mats.append(((i // s == t // s) & (i > t)).astype(np.float32))
    tri = np.concatenate(mats, axis=0)
    j = i
    level = np.full((c, c), -1, np.int32)
    level[t == j] = 0
    for n, s in enumerate(GLA_LEVELS):
        sel = (t // (2 * s) == j // (2 * s)) & (t % (2 * s) >= s) & (j % (2 * s) < s)
        level[sel] = n + 1
    level = np.tile(level, (G_HEADS, 1))
    headmask = (np.arange(HK)[None, :] // G_QK == np.arange(G_HEADS * c)[:, None] // c).astype(np.float32)
    return jnp.asarray(tri, BF16), jnp.asarray(level), jnp.asarray(headmask)


def _stack_heads(x, hm):
    return jnp.concatenate([x] * G_HEADS, axis=0) * hm


def _col_bcast(row, width):
    n = row.shape[1]
    eye = lax.broadcasted_iota(jnp.int32, (n, n), 0) == lax.broadcasted_iota(jnp.int32, (n, n), 1)
    ones = jnp.ones((n, width), BF16)
    out = jnp.zeros((n, width), F32)
    for part in _split3(row):
        d = jnp.where(eye, jnp.broadcast_to(part.astype(F32), (n, n)), 0.0)
        out = out + jnp.dot(d.astype(BF16), ones, preferred_element_type=F32)
    return out


def _gla_kernel(q_ref, k_ref, v_ref, rg_ref, lr_ref, wup_ref, bup_ref, nw_ref, tri_ref, lvl_ref, hm_ref,
                o_ref, s_out_ref, s_ref, *, n_chunks):
    c = CHUNK

    @pl.when(pl.program_id(0) == 0)
    def _():
        s_ref[...] = jnp.zeros(s_ref.shape, F32)

    tri = tri_ref[...]
    lvl = lvl_ref[...]
    hm = hm_ref[...]
    nl = len(GLA_LEVELS)

    def chunk(ci, carry):
        r0 = pl.multiple_of(ci * c, c)
        rows = pl.ds(r0, c)
        q = q_ref[rows, :] * (G_QK ** -0.5)
        k = k_ref[rows, :]
        v = v_ref[rows, :].astype(BF16)
        gk = jnp.dot(lr_ref[rows, :].astype(BF16), wup_ref[...], preferred_element_type=F32) + bup_ref[...]
        g = _log_sigmoid(gk) / G_NORMALIZER
        ps = jnp.zeros((2 * nl * c, HK), F32)
        for part in _split3(g):
            ps = ps + jnp.dot(tri, part, preferred_element_type=F32)
        pre = [g] + [ps[n * c:(n + 1) * c] for n in range(nl)]
        suf = [jnp.zeros_like(g)] + [ps[(nl + n) * c:(nl + n + 1) * c] for n in range(nl)]
        gcum, gsuf = pre[nl], suf[nl]
        a = jnp.zeros((G_HEADS * c, c), F32)
        for n in range(nl + 1):
            if n == 0:
                qq, kk = q, k
            else:
                qq, kk = q * jnp.exp(pre[n - 1]), k * jnp.exp(suf[n - 1])
            d = lax.dot_general(_stack_heads(qq, hm).astype(BF16), kk.astype(BF16),
                                (((1,), (1,)), ((), ())), preferred_element_type=F32)
            a = jnp.where(lvl == n, d, a)
        s_old = s_ref[...]
        inter = jnp.dot(_stack_heads(q * jnp.exp(gcum), hm).astype(BF16), s_old.astype(BF16),
                        preferred_element_type=F32)
        ab = a.astype(BF16)
        nw = nw_ref[...]
        kv = lax.dot_general((k * jnp.exp(gsuf)).astype(BF16), v, (((0,), (0,)), ((), ())),
                             preferred_element_type=F32)
        decay = jnp.exp(_col_bcast(gcum[c - 1:c, :], G_V))
        for hh in range(G_HEADS):
            vs = slice(hh * G_V, (hh + 1) * G_V)
            o = inter[hh * c:(hh + 1) * c] + jnp.dot(ab[hh * c:(hh + 1) * c], v[:, vs],
                                                     preferred_element_type=F32)
            o_ref[rows, vs] = _rms(o, nw) * _silu(rg_ref[rows, vs])
            ks = slice(hh * G_QK, (hh + 1) * G_QK)
            s_ref[ks, :] = decay[ks] * s_old[ks] + kv[ks, vs]
        return carry

    lax.fori_loop(0, n_chunks, chunk, 0)

    @pl.when(pl.program_id(0) == pl.num_programs(0) - 1)
    def _():
        s_out_ref[...] = s_ref[...]


def gla_prompt(h, lrg, w_up, b_up, norm_w):
    t = h.shape[0]
    tb = _tile(t, 512)
    tri, lvl, hm = _gla_constants()
    full = lambda a: pl.BlockSpec(a.shape, lambda i: (0,) * a.ndim)
    o, s = pl.pallas_call(
        functools.partial(_gla_kernel, n_chunks=tb // CHUNK),
        grid=(t // tb,),
        in_specs=[pl.BlockSpec((tb, HK), lambda i: (i, QG // 2)),
                  pl.BlockSpec((tb, HK), lambda i: (i, KG // 2)),
                  pl.BlockSpec((tb, HV), lambda i: (i, VG // 4)),
                  pl.BlockSpec((tb, HV), lambda i: (i, RG // 4)),
                  pl.BlockSpec((tb, LANES), lambda i: (i, 0)),
                  full(w_up), full(b_up), full(norm_w), full(tri), full(lvl), full(hm)],
        out_specs=[pl.BlockSpec((tb, HV), lambda i: (i, 0)),
                   pl.BlockSpec((HK, G_V), lambda i: (0, 0))],
        out_shape=[jax.ShapeDtypeStruct((t, HV), F32), jax.ShapeDtypeStruct((HK, G_V), F32)],
        scratch_shapes=[pltpu.VMEM((HK, G_V), F32)],
        compiler_params=_cparams(("arbitrary",)))(h, h, h, h, lrg, w_up, b_up, norm_w, tri, lvl, hm)
    return o, s.reshape(G_HEADS, G_QK, G_V)


def _ret_kernel(q_ref, k_ref, v_ref, gr_ref, nw_ref, dm_ref, cross_ref, tail_ref, gc_ref, hm_ref,
                o_ref, s_out_ref, s_ref, *, n_chunks):
    c = CHUNK

    @pl.when(pl.program_id(0) == 0)
    def _():
        s_ref[...] = jnp.zeros(s_ref.shape, F32)

    hm = hm_ref[...]
    dm = dm_ref[...]
    cross = cross_ref[...]
    tail = tail_ref[...]
    gc = gc_ref[...]
    nw = nw_ref[...]

    def chunk(ci, carry):
        r0 = pl.multiple_of(ci * c, c)
        rows = pl.ds(r0, c)
        q = q_ref[rows, :]
        k = k_ref[rows, :]
        v = v_ref[rows, :].astype(BF16)
        qs = _stack_heads(q, hm).astype(BF16)
        a = lax.dot_general(qs, k.astype(BF16), (((1,), (1,)), ((), ())), preferred_element_type=F32) * dm
        s_old = s_ref[...]
        inter = jnp.dot(qs, s_old.astype(BF16), preferred_element_type=F32) * cross
        kv = lax.dot_general((k * tail).astype(BF16), v, (((0,), (0,)), ((), ())), preferred_element_type=F32)
        ab = a.astype(BF16)
        for hh in range(R_HEADS):
            vs = slice(hh * G_V, (hh + 1) * G_V)
            o = inter[hh * c:(hh + 1) * c] + jnp.dot(ab[hh * c:(hh + 1) * c], v[:, vs],
                                                     preferred_element_type=F32)
            o_ref[rows, vs] = _rms(o, nw) * _silu(gr_ref[rows, vs])
            ks = slice(hh * R_QK, (hh + 1) * R_QK)
            s_ref[ks, :] = gc[ks] * s_old[ks] + kv[ks, vs]
        return carry

    lax.fori_loop(0, n_chunks, chunk, 0)

    @pl.when(pl.program_id(0) == pl.num_programs(0) - 1)
    def _():
        s_out_ref[...] = s_ref[...]


def _ret_log_gamma():
    return jnp.log(1.0 - jnp.exp2(-5.0 - jnp.arange(R_HEADS, dtype=F32)))


def ret_prompt(h, qr, kr, norm_w):
    t = h.shape[0]
    tb = _tile(t, 512)
    c = CHUNK
    lg = _ret_log_gamma()
    idx = jnp.arange(c, dtype=F32)
    rel = idx[:, None] - idx[None, :]
    dmat = jnp.where(rel[None] >= 0, jnp.exp(jnp.maximum(rel, 0.0)[None] * lg[:, None, None]), 0.0)
    dmat = dmat.reshape(R_HEADS * c, c)
    cross = jnp.exp((idx + 1.0)[None, :] * lg[:, None]).reshape(R_HEADS * c, 1)
    cross = jnp.broadcast_to(cross, (R_HEADS * c, G_V))
    tail = jnp.exp((c - 1.0 - idx)[None, :] * lg[:, None])
    tail = jnp.repeat(tail.T, R_QK, axis=1)
    gc = jnp.broadcast_to(jnp.repeat(jnp.exp(c * lg), R_QK)[:, None], (HK, G_V))
    _, _, hm = _gla_constants()
    full = lambda a: pl.BlockSpec(a.shape, lambda i: (0,) * a.ndim)
    o, s = pl.pallas_call(
        functools.partial(_ret_kernel, n_chunks=tb // c),
        grid=(t // tb,),
        in_specs=[pl.BlockSpec((tb, HK), lambda i: (i, 0)),
                  pl.BlockSpec((tb, HK), lambda i: (i, 0)),
                  pl.BlockSpec((tb, HV), lambda i: (i, VR // 4)),
                  pl.BlockSpec((tb, HV), lambda i: (i, GR // 4)),
                  full(norm_w), full(dmat), full(cross), full(tail), full(gc), full(hm)],
        out_specs=[pl.BlockSpec((tb, HV), lambda i: (i, 0)),
                   pl.BlockSpec((HK, G_V), lambda i: (0, 0))],
        out_shape=[jax.ShapeDtypeStruct((t, HV), F32), jax.ShapeDtypeStruct((HK, G_V), F32)],
        scratch_shapes=[pltpu.VMEM((HK, G_V), F32)],
        compiler_params=_cparams(("arbitrary",)))(qr, kr, h, h, norm_w, dmat, cross, tail, gc, hm)
    return o, s.reshape(R_HEADS, R_QK, G_V)


def _step_kernel(s_ref, q_ref, k_ref, d_ref, v_ref, gate_ref, nw_ref, s_out_ref, o_ref, *, is_gla, q_scale):
    d = d_ref[...]
    if is_gla:
        d = _log_sigmoid(d) / G_NORMALIZER
    s_new = jnp.exp(d) * s_ref[...] + k_ref[...] * v_ref[...]
    s_out_ref[...] = s_new
    o = jnp.sum((q_ref[...] * q_scale) * s_new, axis=1, keepdims=True)
    o_ref[...] = _rms(o, nw_ref[...]) * _silu(gate_ref[...])


def recurrent_step(state, q, k, dlog, v, gate, norm_w, is_gla, q_scale):
    b = state.shape[0]
    col = lambda a: a.reshape(b, G_HEADS, G_QK, 1)
    rowv = lambda a: a.reshape(b, G_HEADS, 1, G_V)
    cspec = pl.BlockSpec((None, G_HEADS, G_QK, 1), lambda i: (i, 0, 0, 0))
    rspec = pl.BlockSpec((None, G_HEADS, 1, G_V), lambda i: (i, 0, 0, 0))
    sspec = pl.BlockSpec((None, G_HEADS, G_QK, G_V), lambda i: (i, 0, 0, 0))
    s_new, o = pl.pallas_call(
        functools.partial(_step_kernel, is_gla=is_gla, q_scale=q_scale),
        grid=(b,),
        in_specs=[sspec, cspec, cspec, cspec, rspec, rspec, pl.BlockSpec((1, G_V), lambda i: (0, 0))],
        out_specs=[sspec, rspec],
        out_shape=[jax.ShapeDtypeStruct(state.shape, F32), jax.ShapeDtypeStruct((b, G_HEADS, 1, G_V), F32)],
        compiler_params=_cparams(("parallel",)))(state, col(q), col(k), col(dlog), rowv(v), rowv(gate), norm_w)
    return o.reshape(b, HV), s_new


def _mixer_out_kernel(oa_ref, og_ref, or_ref, za_ref, zb_ref, zc_ref, x_ref, wpa_ref, wpb_ref, wpc_ref,
                      wo_ref, g_ref, b_ref, y_ref, *, alpha):
    def branch(o_ref, z_ref, w_ref):
        return _sigmoid(z_ref[...]) * jnp.dot(o_ref[...].astype(BF16), w_ref[...], preferred_element_type=F32)

    merged = branch(oa_ref, za_ref, wpa_ref) + branch(og_ref, zb_ref, wpb_ref) + branch(or_ref, zc_ref, wpc_ref)
    y = alpha * x_ref[...] + jnp.dot(merged.astype(BF16), wo_ref[...], preferred_element_type=F32)
    y_ref[...] = _layer_norm(y, g_ref[...], b_ref[...])


def mixer_out(oa, og, orr, h, x, wpa, wpb, wpc, wo, g, b, alpha):
    m = x.shape[0]
    tm = _tile(m, 256)
    bw = oa.shape[1]
    ospec = pl.BlockSpec((tm, bw), lambda i: (i, 0))
    zspec = lambda blk: pl.BlockSpec((tm, D_MODEL), lambda i: (i, blk // 8))
    xspec = pl.BlockSpec((tm, D_MODEL), lambda i: (i, 0))
    full = lambda a: pl.BlockSpec(a.shape, lambda i: (0,) * a.ndim)
    return pl.pallas_call(
        functools.partial(_mixer_out_kernel, alpha=alpha),
        grid=(m // tm,),
        in_specs=[ospec, ospec, ospec, zspec(ZA), zspec(ZB), zspec(ZC), xspec,
                  full(wpa), full(wpb), full(wpc), full(wo), full(g), full(b)],
        out_specs=xspec,
        out_shape=jax.ShapeDtypeStruct((m, D_MODEL), F32),
        compiler_params=_cparams(("parallel",)))(oa, og, orr, h, h, h, x, wpa, wpb, wpc, wo, g, b)


def _ffn_up_kernel(x_ref, wg_ref, wu_ref, h_ref):
    x = x_ref[...].astype(BF16)
    a = jnp.dot(x, wg_ref[...], preferred_element_type=F32)
    u = jnp.dot(x, wu_ref[...], preferred_element_type=F32)
    h_ref[...] = (_silu(a) * u).astype(h_ref.dtype)


def _ffn_down_kernel(h_ref, wd_ref, x_ref, g_ref, b_ref, y_ref, *, alpha):
    y = alpha * x_ref[...] + jnp.dot(h_ref[...], wd_ref[...], preferred_element_type=F32)
    y_ref[...] = _layer_norm(y, g_ref[...], b_ref[...])


def dense_ffn(x, wg, wu, wd, g, b, alpha):
    m = x.shape[0]
    f = wg.shape[1]
    tm, tf = _tile(m, 512), _tile(f, 1408)
    hmid = pl.pallas_call(
        _ffn_up_kernel, grid=(m // tm, f // tf),
        in_specs=[pl.BlockSpec((tm, D_MODEL), lambda i, j: (i, 0)),
                  pl.BlockSpec((D_MODEL, tf), lambda i, j: (0, j)),
                  pl.BlockSpec((D_MODEL, tf), lambda i, j: (0, j))],
        out_specs=pl.BlockSpec((tm, tf), lambda i, j: (i, j)),
        out_shape=jax.ShapeDtypeStruct((m, f), BF16),
        compiler_params=_cparams(("parallel", "parallel")))(x, wg, wu)
    full = lambda a: pl.BlockSpec(a.shape, lambda i: (0,) * a.ndim)
    xspec = pl.BlockSpec((tm, D_MODEL), lambda i: (i, 0))
    return pl.pallas_call(
        functools.partial(_ffn_down_kernel, alpha=alpha), grid=(m // tm,),
        in_specs=[pl.BlockSpec((tm, f), lambda i: (i, 0)), full(wd), xspec, full(g), full(b)],
        out_specs=xspec,
        out_shape=jax.ShapeDtypeStruct((m, D_MODEL), F32),
        compiler_params=_cparams(("parallel",)))(hmid, wd, x, g, b)


def _top2_weights(x, wh, wl):
    xh = x.astype(BF16)
    xl = (x - xh.astype(F32)).astype(BF16)
    logits = (jnp.dot(xh, wh, preferred_element_type=F32) + jnp.dot(xh, wl, preferred_element_type=F32)
              + jnp.dot(xl, wh, preferred_element_type=F32))
    lane = lax.broadcasted_iota(jnp.int32, logits.shape, 1)
    logits = jnp.where(lane < N_EXPERTS, logits, NEG_BIG)
    m1 = jnp.max(logits, axis=-1, keepdims=True)
    i1 = jnp.min(jnp.where(logits == m1, lane, LANES), axis=-1, keepdims=True)
    rest = jnp.where(lane == i1, NEG_BIG, logits)
    m2 = jnp.max(rest, axis=-1, keepdims=True)
    i2 = jnp.min(jnp.where(rest == m2, lane, LANES), axis=-1, keepdims=True)
    e2 = jnp.exp(m2 - m1)
    w1 = 1.0 / (1.0 + e2)
    w2 = e2 / (1.0 + e2)
    return jnp.where(lane == i1, w1, jnp.where(lane == i2, w2, 0.0))


def _router_kernel(x_ref, wh_ref, wl_ref, comb_ref):
    comb_ref[...] = _top2_weights(x_ref[...], wh_ref[...], wl_ref[...])


def _router_rank_kernel(x_ref, wh_ref, wl_ref, comb_ref, rankc_ref, rankr_ref, cnt_ref):
    comb = _top2_weights(x_ref[...], wh_ref[...], wl_ref[...])
    tb = comb.shape[0]
    routed = comb > 0.0
    ones = jnp.where(routed, 1.0, 0.0)
    earlier = (lax.broadcasted_iota(jnp.int32, (tb, tb), 0) > lax.broadcasted_iota(jnp.int32, (tb, tb), 1))
    rank = jnp.dot(jnp.where(earlier, 1.0, 0.0).astype(BF16), ones.astype(BF16), preferred_element_type=F32)
    rankc = jnp.where(routed, rank, -1.0)
    comb_ref[...] = comb
    rankc_ref[...] = rankc
    cnt_ref[...] = jnp.broadcast_to(jnp.sum(ones, axis=0, keepdims=True), cnt_ref.shape)
    eye = (lax.broadcasted_iota(jnp.int32, (LANES, LANES), 0) == lax.broadcasted_iota(jnp.int32, (LANES, LANES), 1))
    eye = jnp.where(eye, 1.0, 0.0).astype(BF16)
    rt = jnp.zeros((LANES, tb), F32)
    for part in _split3(rankc):
        rt = rt + lax.dot_general(eye, part, (((1,), (1,)), ((), ())), preferred_element_type=F32)
    rankr_ref[...] = rt[:N_EXPERTS]


def _router_weights(w_router):
    wpad = jnp.zeros((D_MODEL, LANES), F32).at[:, :N_EXPERTS].set(w_router)
    wh = wpad.astype(BF16)
    return wh, (wpad - wh.astype(F32)).astype(BF16)


def router(x, w_router):
    m = x.shape[0]
    tm = _tile(m, 512)
    wh, wl = _router_weights(w_router)
    full = lambda a: pl.BlockSpec(a.shape, lambda i: (0,) * a.ndim)
    return pl.pallas_call(
        _router_kernel, grid=(m // tm,),
        in_specs=[pl.BlockSpec((tm, D_MODEL), lambda i: (i, 0)), full(wh), full(wl)],
        out_specs=pl.BlockSpec((tm, LANES), lambda i: (i, 0)),
        out_shape=jax.ShapeDtypeStruct((m, LANES), F32),
        compiler_params=_cparams(("parallel",)))(x, wh, wl)


def router_ranked(x, w_router, tb):
    m = x.shape[0]
    nb = m // tb
    wh, wl = _router_weights(w_router)
    full = lambda a: pl.BlockSpec(a.shape, lambda i: (0,) * a.ndim)
    tok = pl.BlockSpec((tb, LANES), lambda i: (i, 0))
    return pl.pallas_call(
        _router_rank_kernel, grid=(nb,),
        in_specs=[pl.BlockSpec((tb, D_MODEL), lambda i: (i, 0)), full(wh), full(wl)],
        out_specs=[tok, tok, pl.BlockSpec((None, N_EXPERTS, tb), lambda i: (i, 0, 0)),
                   pl.BlockSpec((None, N_EXPERTS, LANES), lambda i: (i, 0, 0))],
        out_shape=[jax.ShapeDtypeStruct((m, LANES), F32), jax.ShapeDtypeStruct((m, LANES), F32),
                   jax.ShapeDtypeStruct((nb, N_EXPERTS, tb), F32),
                   jax.ShapeDtypeStruct((nb, N_EXPERTS, LANES), F32)],
        compiler_params=_cparams(("parallel",)))(x, wh, wl)


def _moe_kernel(x_ref, comb_ref, wg_ref, wu_ref, wd_ref, g_ref, b_ref, y_ref, acc_ref, *, alpha):
    e, f = pl.program_id(1), pl.program_id(2)

    @pl.when((e == 0) & (f == 0))
    def _():
        acc_ref[...] = jnp.zeros(acc_ref.shape, F32)

    x = x_ref[...].astype(BF16)
    a = jnp.dot(x, wg_ref[...], preferred_element_type=F32)
    u = jnp.dot(x, wu_ref[...], preferred_element_type=F32)
    hmid = (_silu(a) * u).astype(BF16)
    comb = comb_ref[...]
    lane = lax.broadcasted_iota(jnp.int32, comb.shape, 1)
    ce = jnp.sum(jnp.where(lane == e, comb, 0.0), axis=-1, keepdims=True)
    acc_ref[...] += ce * jnp.dot(hmid, wd_ref[...], preferred_element_type=F32)

    @pl.when((e == pl.num_programs(1) - 1) & (f == pl.num_programs(2) - 1))
    def _():
        y_ref[...] = _layer_norm(alpha * x_ref[...] + acc_ref[...], g_ref[...], b_ref[...])


def moe_ffn(x, comb, wg, wu, wd, g, b, alpha):
    m = x.shape[0]
    ne, _, f = wg.shape
    tm, tf = _tile(m, 1024), _tile(f, 512)
    xspec = pl.BlockSpec((tm, D_MODEL), lambda i, e, j: (i, 0))
    full = lambda a: pl.BlockSpec(a.shape, lambda i, e, j: (0,) * a.ndim)
    return pl.pallas_call(
        functools.partial(_moe_kernel, alpha=alpha), grid=(m // tm, ne, f // tf),
        in_specs=[xspec, pl.BlockSpec((tm, LANES), lambda i, e, j: (i, 0)),
                  pl.BlockSpec((None, D_MODEL, tf), lambda i, e, j: (e, 0, j)),
                  pl.BlockSpec((None, D_MODEL, tf), lambda i, e, j: (e, 0, j)),
                  pl.BlockSpec((None, tf, D_MODEL), lambda i, e, j: (e, j, 0)),
                  full(g), full(b)],
        out_specs=xspec,
        out_shape=jax.ShapeDtypeStruct((m, D_MODEL), F32),
        scratch_shapes=[pltpu.VMEM((tm, D_MODEL), F32)],
        compiler_params=_cparams(("parallel", "arbitrary", "arbitrary")))(x, comb, wg, wu, wd, g, b)


def _moe_routed_kernel(cnt_ref, x_ref, comb_ref, rankc_ref, rankr_ref, wg_ref, wu_ref, wd_ref, g_ref, b_ref,
                       y_ref, acc_ref, xb_ref, xc_ref, yacc_ref, *, alpha, rows, cap):
    blk, e, f = pl.program_id(0), pl.program_id(1), pl.program_id(2)
    last_f = pl.num_programs(2) - 1
    n = cnt_ref[blk * N_EXPERTS + e]
    tb = x_ref.shape[0]

    @pl.when((e == 0) & (f == 0))
    def _():
        acc_ref[...] = jnp.zeros(acc_ref.shape, F32)
        xb_ref[...] = x_ref[...].astype(BF16)

    def column(ref):
        a = ref[...]
        lane = lax.broadcasted_iota(jnp.int32, a.shape, 1)
        return jnp.sum(jnp.where(lane == e, a, 0.0), axis=-1, keepdims=True)

    def expert(xs):
        a = jnp.dot(xs, wg_ref[...], preferred_element_type=F32)
        u = jnp.dot(xs, wu_ref[...], preferred_element_type=F32)
        return jnp.dot((_silu(a) * u).astype(BF16), wd_ref[...], preferred_element_type=F32)

    @pl.when((n > 0) & (n <= rows))
    def _():
        @pl.when(f == 0)
        def _():
            rr = rankr_ref[...]
            sub = lax.broadcasted_iota(jnp.int32, rr.shape, 0)
            rrow = jnp.sum(jnp.where(sub == e, rr, 0.0), axis=0, keepdims=True)
            slot = lax.broadcasted_iota(jnp.int32, (rows, tb), 0).astype(F32)
            gather = jnp.where(slot == rrow, 1.0, 0.0).astype(BF16)
            xc_ref[...] = jnp.dot(gather, xb_ref[...], preferred_element_type=F32).astype(BF16)
            yacc_ref[0:cap, :] = jnp.zeros((cap, D_MODEL), F32)

        yacc_ref[0:rows, :] += expert(xc_ref[...])

        @pl.when(f == last_f)
        def _():
            slot = lax.broadcasted_iota(jnp.int32, (tb, cap), 1).astype(F32)
            scatter = jnp.where(slot == column(rankc_ref), 1.0, 0.0).astype(BF16)
            acc_ref[...] += column(comb_ref) * jnp.dot(scatter, yacc_ref[0:cap, :].astype(BF16),
                                                       preferred_element_type=F32)

    @pl.when(n > rows)
    def _():
        @pl.when(f == 0)
        def _():
            yacc_ref[...] = jnp.zeros(yacc_ref.shape, F32)

        yacc_ref[...] += expert(xb_ref[...])

        @pl.when(f == last_f)
        def _():
            acc_ref[...] += column(comb_ref) * yacc_ref[...]

    @pl.when((e == pl.num_programs(1) - 1) & (f == last_f))
    def _():
        y_ref[...] = _layer_norm(alpha * x_ref[...] + acc_ref[...], g_ref[...], b_ref[...])


def moe_routed_ffn(x, w_router, wg, wu, wd, g, b, alpha):
    m = x.shape[0]
    ne, _, f = wg.shape
    tb, tf = _tile(m, MOE_BLOCK), _tile(f, 512)
    rows = min(tb, MOE_ROWS)
    cap = min(tb, -(-rows // LANES) * LANES)
    comb, rankc, rankr, cnt = router_ranked(x, w_router, tb)
    counts = cnt[:, 0, :N_EXPERTS].astype(jnp.int32).reshape(-1)
    xspec = pl.BlockSpec((tb, D_MODEL), lambda i, e, j, c: (i, 0))
    tok = pl.BlockSpec((tb, LANES), lambda i, e, j, c: (i, 0))
    full = lambda a: pl.BlockSpec(a.shape, lambda i, e, j, c: (0,) * a.ndim)
    grid_spec = pltpu.PrefetchScalarGridSpec(
        num_scalar_prefetch=1, grid=(m // tb, ne, f // tf),
        in_specs=[xspec, tok, tok, pl.BlockSpec((None, N_EXPERTS, tb), lambda i, e, j, c: (i, 0, 0)),
                  pl.BlockSpec((None, D_MODEL, tf), lambda i, e, j, c: (e, 0, j)),
                  pl.BlockSpec((None, D_MODEL, tf), lambda i, e, j, c: (e, 0, j)),
                  pl.BlockSpec((None, tf, D_MODEL), lambda i, e, j, c: (e, j, 0)),
                  full(g), full(b)],
        out_specs=xspec,
        scratch_shapes=[pltpu.VMEM((tb, D_MODEL), F32), pltpu.VMEM((tb, D_MODEL), BF16),
                        pltpu.VMEM((rows, D_MODEL), BF16), pltpu.VMEM((tb, D_MODEL), F32)])
    return pl.pallas_call(
        functools.partial(_moe_routed_kernel, alpha=alpha, rows=rows, cap=cap),
        grid_spec=grid_spec,
        out_shape=jax.ShapeDtypeStruct((m, D_MODEL), F32),
        compiler_params=_cparams(("parallel", "arbitrary", "arbitrary")))(
            counts, x, comb, rankc, rankr, wg, wu, wd, g, b)


def _layer_weights(l, w_in, lam_q1, lam_k1, lam_q2, lam_k2, subln_w, w_gla_up, b_gla_up, gla_norm_w,
                   ret_norm_w, w_pa, w_pb, w_pc, w_out, ln1_g, ln1_b, ln2_g, ln2_b):
    wl = w_in[l]
    w_main = jnp.concatenate([wl[:, a:b] for a, b in PACK_ORDER], axis=1).astype(BF16)
    w_lrg = jnp.zeros((D_MODEL, LANES), F32).at[:, :G_RANK].set(wl[:, LRG_OFF:LRG_OFF + G_RANK]).astype(BF16)
    w_up = jnp.zeros((LANES, HK), F32).at[:G_RANK].set(w_gla_up[l]).astype(BF16)
    lamvec = jnp.zeros((8, LANES), F32)
    for r, vec in enumerate((lam_q1, lam_k1, lam_q2, lam_k2)):
        lamvec = lamvec.at[r, :A_QK].set(vec[l].astype(F32))
    row = lambda a: a[l].reshape(1, -1)
    return dict(w_main=w_main, w_lrg=w_lrg, w_up=w_up, b_up=row(b_gla_up), lamvec=lamvec,
                subln=row(subln_w), gla_nw=row(gla_norm_w), ret_nw=row(ret_norm_w),
                wpa=w_pa[l].astype(BF16), wpb=w_pb[l].astype(BF16), wpc=w_pc[l].astype(BF16),
                wo=w_out[l].astype(BF16), ln1_g=row(ln1_g), ln1_b=row(ln1_b),
                ln2_g=row(ln2_g), ln2_b=row(ln2_b), lam_init=0.8 - 0.6 * math.exp(-0.3 * l))


def _project(x, pos, lw, q_dtype, q_scale):
    h = matmul(x, lw['w_main'])
    lrg = matmul(x, lw['w_lrg'], tn_cap=LANES)
    qa, ka = rotary(h, QA, A_HEADS * LANES, _rope_tables(pos), ROPE_DIM // 2, q_scale, 1.0, q_dtype)
    qr, kr = rotary(h, QR, HK, _ret_tables(pos), 1, 1.0, R_QK ** -0.5, F32)
    return h, lrg, qa, ka, qr, kr


def _channel_mix(x1, l, lw, ffn_w, alpha):
    if l % 2 == 0:
        wg, wu, wd = ffn_w['dense'][l // 2]
        return dense_ffn(x1, wg, wu, wd, lw['ln2_g'], lw['ln2_b'], alpha)
    w_r, wg, wu, wd = ffn_w['moe'][l // 2]
    if x1.shape[0] >= MOE_MIN_ROUTED:
        return moe_routed_ffn(x1, w_r, wg, wu, wd, lw['ln2_g'], lw['ln2_b'], alpha)
    comb = router(x1, w_r)
    return moe_ffn(x1, comb, wg, wu, wd, lw['ln2_g'], lw['ln2_b'], alpha)


def kernel(x_prompt, x_sample, cache_k, cache_v, state_gla, state_ret, page_table, w_in, lam_q1, lam_k1, lam_q2, lam_k2, subln_w, w_gla_up, b_gla_up, gla_norm_w, ret_norm_w, w_pa, w_pb, w_pc, w_out, ln1_g, ln1_b, w_ff_gate, w_ff_up, w_ff_down, w_router, w_exp_gate, w_exp_up, w_exp_down, ln2_g, ln2_b):
    bp, tp, _ = x_prompt.shape
    bs, ts, _ = x_sample.shape
    assert bp == 1 and ts == 1
    depth = w_in.shape[0]
    alpha = (2 * depth) ** 0.25
    n_pool, page = cache_k.shape[1], cache_k.shape[2]
    past_len = page_table.shape[1] * page
    pos_p = jnp.arange(tp, dtype=jnp.int32)
    pos_s = jnp.full((bs,), past_len, jnp.int32)
    ffn_w = dict(
        dense=[(w_ff_gate[i].astype(BF16), w_ff_up[i].astype(BF16), w_ff_down[i].astype(BF16))
               for i in range(w_ff_gate.shape[0])],
        moe=[(w_router[i], w_exp_gate[i].astype(BF16), w_exp_up[i].astype(BF16), w_exp_down[i].astype(BF16))
             for i in range(w_router.shape[0])])
    ck = cache_k.reshape(depth, n_pool, page * A_HEADS, LANES)
    cv = cache_v.reshape(depth, n_pool, page * A_HEADS, LANES)
    lg_col = jnp.broadcast_to(jnp.repeat(_ret_log_gamma(), R_QK)[None, :], (bs, HK))

    yp = x_prompt.reshape(tp, D_MODEL)
    ys = x_sample.reshape(bs, D_MODEL)
    outs = {n: [] for n in ('kp', 'vp', 'gp', 'rp', 'ks', 'vs', 'gs', 'rs')}
    cols = lambda a, blk, n: a[:, blk * LANES:(blk + n) * LANES]
    for l in range(depth):
        lw = _layer_weights(l, w_in, lam_q1, lam_k1, lam_q2, lam_k2, subln_w, w_gla_up, b_gla_up,
                            gla_norm_w, ret_norm_w, w_pa, w_pb, w_pc, w_out, ln1_g, ln1_b, ln2_g, ln2_b)
        h, lrg, qa, ka, qr, kr = _project(yp, pos_p, lw, BF16, A_QK ** -0.5 * LOG2E)
        oa = flash_diff_attention(qa, ka, h, lw['lamvec'], lw['subln'], lw['lam_init'])
        og, s_gla = gla_prompt(h, lrg, lw['w_up'], lw['b_up'], lw['gla_nw'])
        orr, s_ret = ret_prompt(h, qr, kr, lw['ret_nw'])
        x1 = mixer_out(oa, og, orr, h, yp, lw['wpa'], lw['wpb'], lw['wpc'], lw['wo'],
                       lw['ln1_g'], lw['ln1_b'], alpha)
        yp = _channel_mix(x1, l, lw, ffn_w, alpha)
        outs['kp'].append(ka.reshape(1, tp, A_HEADS, 2 * A_QK))
        outs['vp'].append(cols(h, VA, 4).reshape(1, tp, A_HEADS, 2 * A_QK))
        outs['gp'].append(s_gla[None])
        outs['rp'].append(s_ret[None])
        h, lrg, qa, ka, qr, kr = _project(ys, pos_s, lw, F32, A_QK ** -0.5)
        va = cols(h, VA, 4)
        oa = decode_diff_attention(qa, ka, va, ck, cv, l, page_table, lw['lamvec'], lw['subln'],
                                   lw['lam_init'])
        gk = matmul(lrg, lw['w_up'], bias=lw['b_up'])
        og, s_gla = recurrent_step(state_gla[l], cols(h, QG, 2), cols(h, KG, 2), gk, cols(h, VG, 4),
                                   cols(h, RG, 4), lw['gla_nw'], True, G_QK ** -0.5)
        orr, s_ret = recurrent_step(state_ret[l], qr, kr, lg_col, cols(h, VR, 4), cols(h, GR, 4),
                                    lw['ret_nw'], False, 1.0)
        x1 = mixer_out(oa, og, orr, h, ys, lw['wpa'], lw['wpb'], lw['wpc'], lw['wo'],
                       lw['ln1_g'], lw['ln1_b'], alpha)
        ys = _channel_mix(x1, l, lw, ffn_w, alpha)
        outs['ks'].append(ka.reshape(bs, 1, A_HEADS, 2 * A_QK))
        outs['vs'].append(va.reshape(bs, 1, A_HEADS, 2 * A_QK))
        outs['gs'].append(s_gla)
        outs['rs'].append(s_ret)

    st = lambda n: jnp.stack(outs[n])
    return (yp.reshape(bp, tp, D_MODEL), ys.reshape(bs, ts, D_MODEL), st('kp'), st('vp'), st('gp'),
            st('rp'), st('ks'), st('vs'), st('gs'), st('rs'))
```

```python
import functools
import math

import numpy as np
import jax
import jax.numpy as jnp
from jax import lax
from jax.experimental import pallas as pl
from jax.experimental.pallas import tpu as pltpu

F32 = jnp.float32
BF16 = jnp.bfloat16

D_MODEL = 1024
A_HEADS = 4
A_QK = 64
ROPE_DIM = 16
ROPE_THETA = 500000.0
G_HEADS = 4
G_QK = 64
G_V = 128
G_RANK = 16
G_NORMALIZER = 16.0
R_HEADS = 4
R_QK = 64
R_ANGLE_BASE = 10000.0
CHUNK = 64
N_EXPERTS = 8
NORM_EPS = 1e-5
NEG_BIG = -1e30
LOG2E = math.log2(math.e)
FLASH_T = 1024
FLASH_RC = 32
FLASH_KCOLS = 256
FLASH_PVROWS = 128
DECODE_PAGES = 16
SCAN_UNROLL = 4
MOE_BLOCK = 1024
MOE_ROWS = 320
MOE_MIN_ROUTED = 256
MOE_TF = 896
INPROJ_TN = 1536

LANES = 128
VMEM_LIMIT = 56 * 1024 * 1024

ZA, ZB, ZC, QA, KA, VA, VG, RG, VR, GR, QG, KG, QR, KR = 0, 8, 16, 24, 28, 32, 36, 40, 44, 48, 52, 54, 56, 58
PACK_ORDER = ((4624, 7696), (0, 1536), (2048, 2560), (2576, 3088), (3600, 4112), (4112, 4624),
              (1536, 2048), (3088, 3600))
LRG_OFF = 2560


def _cparams(sem):
    return pltpu.CompilerParams(dimension_semantics=sem, vmem_limit_bytes=VMEM_LIMIT)


def _tile(n, cap):
    c = min(n, cap)
    while n % c:
        c -= 1
    return c


def _layer_norm(y, g, b):
    mu = jnp.mean(y, axis=-1, keepdims=True)
    d = y - mu
    var = jnp.mean(d * d, axis=-1, keepdims=True)
    return d * lax.rsqrt(var + NORM_EPS) * g + b


def _rms(o, w):
    return o * lax.rsqrt(jnp.mean(o * o, axis=-1, keepdims=True) + NORM_EPS) * w


def _silu(x):
    return x * (1.0 / (1.0 + jnp.exp(-x)))


def _sigmoid(x):
    return 1.0 / (1.0 + jnp.exp(-x))


def _log_sigmoid(x):
    return jnp.minimum(x, 0.0) - jnp.log(1.0 + jnp.exp(-jnp.abs(x)))


def _split3(x):
    a = x.astype(BF16)
    r = x - a.astype(F32)
    b = r.astype(BF16)
    c = (r - b.astype(F32)).astype(BF16)
    return a, b, c


def _mm_kernel(x_ref, w_ref, o_ref):
    o_ref[...] = jnp.dot(x_ref[...].astype(BF16), w_ref[...],
                         preferred_element_type=F32).astype(o_ref.dtype)


def _mm_bias_kernel(x_ref, w_ref, b_ref, o_ref):
    o_ref[...] = (jnp.dot(x_ref[...].astype(BF16), w_ref[...],
                          preferred_element_type=F32) + b_ref[...]).astype(o_ref.dtype)


def matmul(x, w, bias=None, out_dtype=F32, tm_cap=1024, tn_cap=768):
    m, k = x.shape
    n = w.shape[1]
    tm, tn = _tile(m, tm_cap), _tile(n, tn_cap)
    in_specs = [pl.BlockSpec((tm, k), lambda i, j: (i, 0)),
                pl.BlockSpec((k, tn), lambda i, j: (0, j))]
    args = [x, w]
    kern = _mm_kernel
    if bias is not None:
        in_specs.append(pl.BlockSpec((1, tn), lambda i, j: (0, j)))
        args.append(bias)
        kern = _mm_bias_kernel
    return pl.pallas_call(
        kern, grid=(m // tm, n // tn), in_specs=in_specs,
        out_specs=pl.BlockSpec((tm, tn), lambda i, j: (i, j)),
        out_shape=jax.ShapeDtypeStruct((m, n), out_dtype),
        compiler_params=_cparams(("parallel", "parallel")))(*args)


def _rot_kernel(x_ref, c_ref, s1_ref, s2_ref, oq_ref, ok_ref, *, shift, q_scale, k_scale):
    c, s1, s2 = c_ref[...], s1_ref[...], s2_ref[...]
    nq = oq_ref.shape[1] // LANES
    nk = ok_ref.shape[1] // LANES
    for b in range(nq + nk):
        x = x_ref[:, b * LANES:(b + 1) * LANES]
        y = x * c + pltpu.roll(x, LANES - shift, 1) * s1 + pltpu.roll(x, shift, 1) * s2
        if b < nq:
            oq_ref[:, b * LANES:(b + 1) * LANES] = (y * q_scale).astype(oq_ref.dtype)
        else:
            ok_ref[:, (b - nq) * LANES:(b - nq + 1) * LANES] = (y * k_scale).astype(ok_ref.dtype)


def rotary(h, col_blk, width, tables, shift, q_scale, k_scale, q_dtype):
    m = h.shape[0]
    tm = _tile(m, 512)
    blk = col_blk * LANES // (2 * width)
    tspec = pl.BlockSpec((tm, LANES), lambda i: (i, 0))
    return pl.pallas_call(
        functools.partial(_rot_kernel, shift=shift, q_scale=q_scale, k_scale=k_scale),
        grid=(m // tm,),
        in_specs=[pl.BlockSpec((tm, 2 * width), lambda i: (i, blk)), tspec, tspec, tspec],
        out_specs=[pl.BlockSpec((tm, width), lambda i: (i, 0)),
                   pl.BlockSpec((tm, width), lambda i: (i, 0))],
        out_shape=[jax.ShapeDtypeStruct((m, width), q_dtype),
                   jax.ShapeDtypeStruct((m, width), F32)],
        compiler_params=_cparams(("parallel",)))(h, *tables)


def _rope_tables(pos):
    half = ROPE_DIM // 2
    inv = ROPE_THETA ** (-jnp.arange(half, dtype=F32) * 2.0 / ROPE_DIM)
    ang = pos.astype(F32)[:, None] * inv[None, :]
    cos, sin = jnp.cos(ang), jnp.sin(ang)
    lane = np.arange(LANES) % A_QK
    fi = lane % half
    cos_l, sin_l = cos[:, fi], sin[:, fi]
    in_rot = (lane < ROPE_DIM)[None, :]
    lo = (lane < half)[None, :]
    c = jnp.where(in_rot, cos_l, 1.0)
    s1 = jnp.where(lo, -sin_l, 0.0)
    s2 = jnp.where(in_rot & ~lo, sin_l, 0.0)
    return c, s1, s2


def _ret_tables(pos):
    inv = 1.0 / (R_ANGLE_BASE ** jnp.linspace(0.0, 1.0, R_QK // 2, dtype=F32))
    ang = pos.astype(F32)[:, None] * inv[None, :]
    cos, sin = jnp.cos(ang), jnp.sin(ang)
    lane = np.arange(LANES) % R_QK
    cos_l, sin_l = cos[:, lane // 2], sin[:, lane // 2]
    even = (lane % 2 == 0)[None, :]
    return cos_l, jnp.where(even, -sin_l, 0.0), jnp.where(even, 0.0, sin_l)


def _lam_from(lam_ref, lam_init):
    v = lam_ref[...]
    t1 = jnp.sum(v[0:1] * v[1:2], axis=-1, keepdims=True)
    t2 = jnp.sum(v[2:3] * v[3:4], axis=-1, keepdims=True)
    return jnp.exp(t1) - jnp.exp(t2) + lam_init


def _flash_kernel(lam_ref, q_ref, k_ref, v_ref, w_ref, o_ref, m_ref, acc_ref, sa_ref, sb_ref, p_ref, *,
                       t, rc, lam_init):
    i, j = pl.program_id(1), pl.program_id(2)
    nch = t // LANES

    kw = min(t, FLASH_KCOLS)
    pvr = min(t, FLASH_PVROWS)

    def produce_items(s_dst):
        items = []
        for c in range(2):
            for kc in range(t // kw):
                def item(c=c, kc=kc):
                    q = q_ref[...]
                    lane = lax.broadcasted_iota(jnp.int32, q.shape, 1)
                    qc = jnp.where((lane < A_QK) if c == 0 else (lane >= A_QK), q, jnp.zeros_like(q))
                    kk = k_ref[kc * kw:(kc + 1) * kw, :].astype(BF16)
                    s_dst[c, :, kc * kw:(kc + 1) * kw] = lax.dot_general(
                        qc, kk, (((1,), (1,)), ((), ())), preferred_element_type=F32)
                items.append(item)
        return items

    def consume_items(s_src, masked):
        items = []
        for g in range(t // pvr):
            def sweep(g=g):
                for r in range(g * pvr // rc, (g + 1) * pvr // rc):
                    rows = slice(r * rc, (r + 1) * rc)
                    if masked:
                        keep = (lax.broadcasted_iota(jnp.int32, (rc, t), 1)
                                <= r * rc + lax.broadcasted_iota(jnp.int32, (rc, t), 0))
                    for c in range(2):
                        s = s_src[c, rows, :]
                        if masked:
                            s = jnp.where(keep, s, NEG_BIG)
                        m_prev = m_ref[c, rows, :]
                        m_new = jnp.maximum(m_prev, jnp.max(s, axis=-1, keepdims=True))
                        p = jnp.exp2(s - jnp.concatenate([m_new] * nch, axis=1))
                        alpha = jnp.exp2(m_prev - m_new)
                        acc_ref[c, rows, :] = jnp.concatenate([alpha, alpha], axis=1) * acc_ref[c, rows, :]
                        m_ref[c, rows, :] = m_new
                        p_ref[c, rows, :] = p.astype(BF16)

            def pv(g=g):
                v = jnp.concatenate([v_ref[...].astype(BF16), jnp.ones((t, LANES), BF16)], axis=1)
                rows = slice(g * pvr, (g + 1) * pvr)
                for c in range(2):
                    acc_ref[c, rows, :] += jnp.dot(p_ref[c, rows, :], v, preferred_element_type=F32)
            items += [sweep, pv]
        return items

    def run(*item_lists):
        longest = max(len(l) for l in item_lists)
        for n in range(longest):
            for l in item_lists:
                lo, hi = n * len(l) // longest, (n + 1) * len(l) // longest
                for it in l[lo:hi]:
                    it()

    def produce(s_dst):
        run(produce_items(s_dst))

    def consume(s_src, masked):
        run(consume_items(s_src, masked))

    even = j % 2 == 0

    @pl.when(j == 0)
    def _():
        m_ref[...] = jnp.full(m_ref.shape, NEG_BIG, F32)
        acc_ref[...] = jnp.zeros(acc_ref.shape, F32)
        produce(sa_ref)

    steady = (j >= 1) & (j <= i)

    @pl.when(steady & even)
    def _():
        run(consume_items(sb_ref, False), produce_items(sa_ref))

    @pl.when(steady & jnp.logical_not(even))
    def _():
        run(consume_items(sa_ref, False), produce_items(sb_ref))

    drain = j == i + 1
    pl.when(drain & even)(functools.partial(consume, sb_ref, True))
    pl.when(drain & jnp.logical_not(even))(functools.partial(consume, sa_ref, True))

    @pl.when(j == pl.num_programs(2) - 1)
    def _():
        lam = _lam_from(lam_ref, lam_init)
        o = (acc_ref[0, :, :LANES] / acc_ref[0, :, LANES:]
             - lam * (acc_ref[1, :, :LANES] / acc_ref[1, :, LANES:]))
        o_ref[...] = _rms(o, w_ref[...]) * (1.0 - lam_init)


def flash_diff_attention(q, k, h, lamvec, subln_w, lam_init, t_cap=FLASH_T, rc=FLASH_RC):
    n = q.shape[0]
    t = _tile(n, t_cap)
    rc = _tile(t, rc)
    nb = n // t
    return pl.pallas_call(
        functools.partial(_flash_kernel, t=t, rc=rc, lam_init=lam_init),
        grid=(A_HEADS, nb, nb + 1),
        in_specs=[pl.BlockSpec((8, LANES), lambda hh, i, j: (0, 0)),
                  pl.BlockSpec((t, LANES), lambda hh, i, j: (i, hh)),
                  pl.BlockSpec((t, LANES), lambda hh, i, j: (jnp.minimum(j, i), hh)),
                  pl.BlockSpec((t, LANES), lambda hh, i, j: (jnp.clip(j - 1, 0, i), VA + hh)),
                  pl.BlockSpec((1, LANES), lambda hh, i, j: (0, 0))],
        out_specs=pl.BlockSpec((t, LANES), lambda hh, i, j: (i, hh)),
        out_shape=jax.ShapeDtypeStruct((n, A_HEADS * LANES), F32),
        scratch_shapes=[pltpu.VMEM((2, t, LANES), F32), pltpu.VMEM((2, t, 2 * LANES), F32),
                        pltpu.VMEM((2, t, t), F32), pltpu.VMEM((2, t, t), F32),
                        pltpu.VMEM((2, t, t), BF16)],
        compiler_params=_cparams(("parallel", "parallel", "arbitrary")))(lamvec, q, k, h, subln_w)


def _decode_kernel(pt_ref, lam_ref, q_ref, kn_ref, vn_ref, w_ref, *rest, pp, lam_init):
    k_refs, v_refs = rest[:pp], rest[pp:2 * pp]
    o_ref, m_ref, l_ref, acc_ref = rest[2 * pp:]
    c = pl.program_id(1)
    rows = 2 * A_HEADS
    prow = k_refs[0].shape[0]

    @pl.when(c == 0)
    def _():
        m_ref[...] = jnp.full(m_ref.shape, NEG_BIG, F32)
        l_ref[...] = jnp.zeros(l_ref.shape, F32)
        acc_ref[...] = jnp.zeros(acc_ref.shape, F32)

    rid = lax.broadcasted_iota(jnp.int32, (rows, LANES), 0)
    lid = lax.broadcasted_iota(jnp.int32, (rows, LANES), 1)
    qb = jnp.where(lid // A_QK == rid % 2, q_ref[...], 0.0).astype(BF16)
    s = jnp.concatenate(
        [lax.dot_general(qb, k_refs[p][...].astype(BF16), (((1,), (1,)), ((), ())),
                         preferred_element_type=F32) for p in range(pp)], axis=-1)
    srow = lax.broadcasted_iota(jnp.int32, s.shape, 0)
    scol = lax.broadcasted_iota(jnp.int32, s.shape, 1)
    s = jnp.where(scol % A_HEADS == srow // 2, s, NEG_BIG)
    m_prev = m_ref[...]
    m_new = jnp.maximum(m_prev, jnp.max(s, axis=-1, keepdims=True))
    p_ = jnp.exp(s - m_new)
    alpha = jnp.exp(m_prev - m_new)
    l_new = alpha * l_ref[...] + jnp.sum(p_, axis=-1, keepdims=True)
    acc = alpha * acc_ref[...]
    for p in range(pp):
        acc = acc + jnp.dot(p_[:, p * prow:(p + 1) * prow].astype(BF16), v_refs[p][...].astype(BF16),
                            preferred_element_type=F32)
    m_ref[...] = m_new
    l_ref[...] = l_new
    acc_ref[...] = acc

    @pl.when(c == pl.num_programs(1) - 1)
    def _():
        kn = kn_ref[...].astype(BF16).astype(F32)
        vn = vn_ref[...].astype(BF16).astype(F32)
        s_self = jnp.sum(qb.astype(F32) * kn, axis=-1, keepdims=True)
        m_fin = jnp.maximum(m_new, s_self)
        a2 = jnp.exp(m_new - m_fin)
        p_self = jnp.exp(s_self - m_fin)
        l_fin = a2 * l_new + p_self
        acc_fin = a2 * acc + p_self.astype(BF16).astype(F32) * vn
        on = acc_fin / l_fin
        lam = _lam_from(lam_ref, lam_init)
        w = w_ref[...]
        for hh in range(A_HEADS):
            o = on[2 * hh:2 * hh + 1] - lam * on[2 * hh + 1:2 * hh + 2]
            o_ref[hh:hh + 1, :] = _rms(o, w) * (1.0 - lam_init)


def decode_diff_attention(q, k_new, v_new, cache_k, cache_v, layer, page_table, lamvec, subln_w, lam_init):
    b, n_pages = page_table.shape
    prow = cache_k.shape[2]
    pp = _tile(n_pages, DECODE_PAGES)
    rows = 2 * A_HEADS

    def row(bb, c, pt):
        return (bb, 0, 0)

    def page_idx(p):
        return lambda bb, c, pt: (layer, pt[bb, c * pp + p], 0, 0)

    row_spec = pl.BlockSpec((None, rows, LANES), row)
    page_specs = [pl.BlockSpec((None, None, prow, LANES), page_idx(p)) for p in range(pp)]
    grid_spec = pltpu.PrefetchScalarGridSpec(
        num_scalar_prefetch=1, grid=(b, n_pages // pp),
        in_specs=[pl.BlockSpec((8, LANES), lambda bb, c, pt: (0, 0)), row_spec, row_spec, row_spec,
                  pl.BlockSpec((1, LANES), lambda bb, c, pt: (0, 0))] + page_specs + page_specs,
        out_specs=pl.BlockSpec((None, A_HEADS, LANES), row),
        scratch_shapes=[pltpu.VMEM((rows, 1), F32), pltpu.VMEM((rows, 1), F32),
                        pltpu.VMEM((rows, LANES), F32)])
    r8 = lambda a: jnp.repeat(a.reshape(b, A_HEADS, LANES), 2, axis=1)
    out = pl.pallas_call(
        functools.partial(_decode_kernel, pp=pp, lam_init=lam_init),
        grid_spec=grid_spec,
        out_shape=jax.ShapeDtypeStruct((b, A_HEADS, LANES), F32),
        compiler_params=_cparams(("parallel", "arbitrary")))(
            page_table, lamvec, r8(q), r8(k_new), r8(v_new), subln_w,
            *([cache_k] * pp), *([cache_v] * pp))
    return out.reshape(b, A_HEADS * LANES)


HK = G_HEADS * G_QK
HV = G_HEADS * G_V
GLA_LEVELS = (1, 2, 4, 8, 16, 32)


def _gla_constants():
    c = CHUNK
    t = np.arange(c)[:, None]
    i = np.arange(c)[None, :]
    mats = []
    for s in GLA_LEVELS[1:] + (c,):
        mats.append(((i // s == t // s) & (i <= t)).astype(np.float32))
    for s in GLA_LEVELS[1:] + (c,):
        mats.append(((i // s == t // s) & (i > t)).astype(np.float32))
    tri = np.concatenate(mats, axis=0)
    j = i
    level = np.full((c, c), -1, np.int32)
    level[t == j] = 0
    for n, s in enumerate(GLA_LEVELS):
        sel = (t // (2 * s) == j // (2 * s)) & (t % (2 * s) >= s) & (j % (2 * s) < s)
        level[sel] = n + 1
    level = np.tile(level, (G_HEADS, 1))
    headmask = (np.arange(HK)[None, :] // G_QK == np.arange(G_HEADS * c)[:, None] // c).astype(np.float32)
    return jnp.asarray(tri, BF16), jnp.asarray(level), jnp.asarray(headmask)


def _stack_heads(x, hm):
    return jnp.concatenate([x] * G_HEADS, axis=0) * hm


def _col_bcast(row, width):
    n = row.shape[1]
    eye = lax.broadcasted_iota(jnp.int32, (n, n), 0) == lax.broadcasted_iota(jnp.int32, (n, n), 1)
    ones = jnp.ones((n, width), BF16)
    out = jnp.zeros((n, width), F32)
    for part in _split3(row):
        d = jnp.where(eye, jnp.broadcast_to(part.astype(F32), (n, n)), 0.0)
        out = out + jnp.dot(d.astype(BF16), ones, preferred_element_type=F32)
    return out


def _gla_kernel(q_ref, k_ref, v_ref, rg_ref, lr_ref, wup_ref, bup_ref, nw_ref, tri_ref, lvl_ref, hm_ref,
                o_ref, s_out_ref, s_ref, *, n_chunks):
    c = CHUNK

    @pl.when(pl.program_id(0) == 0)
    def _():
        s_ref[...] = jnp.zeros(s_ref.shape, F32)

    tri = tri_ref[...]
    lvl = lvl_ref[...]
    hm = hm_ref[...]
    nl = len(GLA_LEVELS)

    def chunk(ci, carry):
        r0 = pl.multiple_of(ci * c, c)
        rows = pl.ds(r0, c)
        q = q_ref[rows, :] * (G_QK ** -0.5)
        k = k_ref[rows, :]
        v = v_ref[rows, :].astype(BF16)
        gk = jnp.dot(lr_ref[rows, :].astype(BF16), wup_ref[...], preferred_element_type=F32) + bup_ref[...]
        g = _log_sigmoid(gk) / G_NORMALIZER
        ps = jnp.zeros((2 * nl * c, HK), F32)
        for part in _split3(g):
            ps = ps + jnp.dot(tri, part, preferred_element_type=F32)
        pre = [g] + [ps[n * c:(n + 1) * c] for n in range(nl)]
        suf = [jnp.zeros_like(g)] + [ps[(nl + n) * c:(nl + n + 1) * c] for n in range(nl)]
        gcum, gsuf = pre[nl], suf[nl]
        a = jnp.zeros((G_HEADS * c, c), F32)
        for n in range(nl + 1):
            if n == 0:
                qq, kk = q, k
            else:
                qq, kk = q * jnp.exp(pre[n - 1]), k * jnp.exp(suf[n - 1])
            d = lax.dot_general(_stack_heads(qq, hm).astype(BF16), kk.astype(BF16),
                                (((1,), (1,)), ((), ())), preferred_element_type=F32)
            a = jnp.where(lvl == n, d, a)
        s_old = s_ref[...]
        inter = jnp.dot(_stack_heads(q * jnp.exp(gcum), hm).astype(BF16), s_old.astype(BF16),
                        preferred_element_type=F32)
        ab = a.astype(BF16)
        nw = nw_ref[...]
        kv = lax.dot_general((k * jnp.exp(gsuf)).astype(BF16), v, (((0,), (0,)), ((), ())),
                             preferred_element_type=F32)
        decay = jnp.exp(_col_bcast(gcum[c - 1:c, :], G_V))
        for hh in range(G_HEADS):
            vs = slice(hh * G_V, (hh + 1) * G_V)
            o = inter[hh * c:(hh + 1) * c] + jnp.dot(ab[hh * c:(hh + 1) * c], v[:, vs],
                                                     preferred_element_type=F32)
            o_ref[rows, vs] = _rms(o, nw) * _silu(rg_ref[rows, vs])
            ks = slice(hh * G_QK, (hh + 1) * G_QK)
            s_ref[ks, :] = decay[ks] * s_old[ks] + kv[ks, vs]
        return carry

    lax.fori_loop(0, n_chunks, chunk, 0, unroll=SCAN_UNROLL)

    @pl.when(pl.program_id(0) == pl.num_programs(0) - 1)
    def _():
        s_out_ref[...] = s_ref[...]


def gla_prompt(h, lrg, w_up, b_up, norm_w):
    t = h.shape[0]
    tb = _tile(t, 512)
    tri, lvl, hm = _gla_constants()
    full = lambda a: pl.BlockSpec(a.shape, lambda i: (0,) * a.ndim)
    o, s = pl.pallas_call(
        functools.partial(_gla_kernel, n_chunks=tb // CHUNK),
        grid=(t // tb,),
        in_specs=[pl.BlockSpec((tb, HK), lambda i: (i, QG // 2)),
                  pl.BlockSpec((tb, HK), lambda i: (i, KG // 2)),
                  pl.BlockSpec((tb, HV), lambda i: (i, VG // 4)),
                  pl.BlockSpec((tb, HV), lambda i: (i, RG // 4)),
                  pl.BlockSpec((tb, LANES), lambda i: (i, 0)),
                  full(w_up), full(b_up), full(norm_w), full(tri), full(lvl), full(hm)],
        out_specs=[pl.BlockSpec((tb, HV), lambda i: (i, 0)),
                   pl.BlockSpec((HK, G_V), lambda i: (0, 0))],
        out_shape=[jax.ShapeDtypeStruct((t, HV), F32), jax.ShapeDtypeStruct((HK, G_V), F32)],
        scratch_shapes=[pltpu.VMEM((HK, G_V), F32)],
        compiler_params=_cparams(("arbitrary",)))(h, h, h, h, lrg, w_up, b_up, norm_w, tri, lvl, hm)
    return o, s.reshape(G_HEADS, G_QK, G_V)


def _ret_kernel(q_ref, k_ref, v_ref, gr_ref, nw_ref, dm_ref, cross_ref, tail_ref, gc_ref, hm_ref,
                o_ref, s_out_ref, s_ref, *, n_chunks):
    c = CHUNK

    @pl.when(pl.program_id(0) == 0)
    def _():
        s_ref[...] = jnp.zeros(s_ref.shape, F32)

    hm = hm_ref[...]
    dm = dm_ref[...]
    cross = cross_ref[...]
    tail = tail_ref[...]
    gc = gc_ref[...]
    nw = nw_ref[...]

    def chunk(ci, carry):
        r0 = pl.multiple_of(ci * c, c)
        rows = pl.ds(r0, c)
        q = q_ref[rows, :]
        k = k_ref[rows, :]
        v = v_ref[rows, :].astype(BF16)
        qs = _stack_heads(q, hm).astype(BF16)
        a = lax.dot_general(qs, k.astype(BF16), (((1,), (1,)), ((), ())), preferred_element_type=F32) * dm
        s_old = s_ref[...]
        inter = jnp.dot(qs, s_old.astype(BF16), preferred_element_type=F32) * cross
        kv = lax.dot_general((k * tail).astype(BF16), v, (((0,), (0,)), ((), ())), preferred_element_type=F32)
        ab = a.astype(BF16)
        for hh in range(R_HEADS):
            vs = slice(hh * G_V, (hh + 1) * G_V)
            o = inter[hh * c:(hh + 1) * c] + jnp.dot(ab[hh * c:(hh + 1) * c], v[:, vs],
                                                     preferred_element_type=F32)
            o_ref[rows, vs] = _rms(o, nw) * _silu(gr_ref[rows, vs])
            ks = slice(hh * R_QK, (hh + 1) * R_QK)
            s_ref[ks, :] = gc[ks] * s_old[ks] + kv[ks, vs]
        return carry

    lax.fori_loop(0, n_chunks, chunk, 0, unroll=SCAN_UNROLL)

    @pl.when(pl.program_id(0) == pl.num_programs(0) - 1)
    def _():
        s_out_ref[...] = s_ref[...]


def _ret_log_gamma():
    return jnp.log(1.0 - jnp.exp2(-5.0 - jnp.arange(R_HEADS, dtype=F32)))


def ret_prompt(h, qr, kr, norm_w):
    t = h.shape[0]
    tb = _tile(t, 512)
    c = CHUNK
    lg = _ret_log_gamma()
    idx = jnp.arange(c, dtype=F32)
    rel = idx[:, None] - idx[None, :]
    dmat = jnp.where(rel[None] >= 0, jnp.exp(jnp.maximum(rel, 0.0)[None] * lg[:, None, None]), 0.0)
    dmat = dmat.reshape(R_HEADS * c, c)
    cross = jnp.exp((idx + 1.0)[None, :] * lg[:, None]).reshape(R_HEADS * c, 1)
    cross = jnp.broadcast_to(cross, (R_HEADS * c, G_V))
    tail = jnp.exp((c - 1.0 - idx)[None, :] * lg[:, None])
    tail = jnp.repeat(tail.T, R_QK, axis=1)
    gc = jnp.broadcast_to(jnp.repeat(jnp.exp(c * lg), R_QK)[:, None], (HK, G_V))
    _, _, hm = _gla_constants()
    full = lambda a: pl.BlockSpec(a.shape, lambda i: (0,) * a.ndim)
    o, s = pl.pallas_call(
        functools.partial(_ret_kernel, n_chunks=tb // c),
        grid=(t // tb,),
        in_specs=[pl.BlockSpec((tb, HK), lambda i: (i, 0)),
                  pl.BlockSpec((tb, HK), lambda i: (i, 0)),
                  pl.BlockSpec((tb, HV), lambda i: (i, VR // 4)),
                  pl.BlockSpec((tb, HV), lambda i: (i, GR // 4)),
                  full(norm_w), full(dmat), full(cross), full(tail), full(gc), full(hm)],
        out_specs=[pl.BlockSpec((tb, HV), lambda i: (i, 0)),
                   pl.BlockSpec((HK, G_V), lambda i: (0, 0))],
        out_shape=[jax.ShapeDtypeStruct((t, HV), F32), jax.ShapeDtypeStruct((HK, G_V), F32)],
        scratch_shapes=[pltpu.VMEM((HK, G_V), F32)],
        compiler_params=_cparams(("arbitrary",)))(qr, kr, h, h, norm_w, dmat, cross, tail, gc, hm)
    return o, s.reshape(R_HEADS, R_QK, G_V)


def _step_kernel(s_ref, q_ref, k_ref, d_ref, v_ref, gate_ref, nw_ref, s_out_ref, o_ref, *, is_gla, q_scale):
    d = d_ref[...]
    if is_gla:
        d = _log_sigmoid(d) / G_NORMALIZER
    s_new = jnp.exp(d) * s_ref[...] + k_ref[...] * v_ref[...]
    s_out_ref[...] = s_new
    o = jnp.sum((q_ref[...] * q_scale) * s_new, axis=1, keepdims=True)
    o_ref[...] = _rms(o, nw_ref[...]) * _silu(gate_ref[...])


def recurrent_step(state, q, k, dlog, v, gate, norm_w, is_gla, q_scale):
    b = state.shape[0]
    col = lambda a: a.reshape(b, G_HEADS, G_QK, 1)
    rowv = lambda a: a.reshape(b, G_HEADS, 1, G_V)
    cspec = pl.BlockSpec((None, G_HEADS, G_QK, 1), lambda i: (i, 0, 0, 0))
    rspec = pl.BlockSpec((None, G_HEADS, 1, G_V), lambda i: (i, 0, 0, 0))
    sspec = pl.BlockSpec((None, G_HEADS, G_QK, G_V), lambda i: (i, 0, 0, 0))
    s_new, o = pl.pallas_call(
        functools.partial(_step_kernel, is_gla=is_gla, q_scale=q_scale),
        grid=(b,),
        in_specs=[sspec, cspec, cspec, cspec, rspec, rspec, pl.BlockSpec((1, G_V), lambda i: (0, 0))],
        out_specs=[sspec, rspec],
        out_shape=[jax.ShapeDtypeStruct(state.shape, F32), jax.ShapeDtypeStruct((b, G_HEADS, 1, G_V), F32)],
        compiler_params=_cparams(("parallel",)))(state, col(q), col(k), col(dlog), rowv(v), rowv(gate), norm_w)
    return o.reshape(b, HV), s_new


def _mixer_out_kernel(oa_ref, og_ref, or_ref, za_ref, zb_ref, zc_ref, x_ref, wpa_ref, wpb_ref, wpc_ref,
                      wo_ref, g_ref, b_ref, y_ref, *, alpha):
    def branch(o_ref, z_ref, w_ref):
        return _sigmoid(z_ref[...]) * jnp.dot(o_ref[...].astype(BF16), w_ref[...], preferred_element_type=F32)

    merged = branch(oa_ref, za_ref, wpa_ref) + branch(og_ref, zb_ref, wpb_ref) + branch(or_ref, zc_ref, wpc_ref)
    y = alpha * x_ref[...] + jnp.dot(merged.astype(BF16), wo_ref[...], preferred_element_type=F32)
    y_ref[...] = _layer_norm(y, g_ref[...], b_ref[...])


def mixer_out(oa, og, orr, h, x, wpa, wpb, wpc, wo, g, b, alpha):
    m = x.shape[0]
    tm = _tile(m, 256)
    bw = oa.shape[1]
    ospec = pl.BlockSpec((tm, bw), lambda i: (i, 0))
    zspec = lambda blk: pl.BlockSpec((tm, D_MODEL), lambda i: (i, blk // 8))
    xspec = pl.BlockSpec((tm, D_MODEL), lambda i: (i, 0))
    full = lambda a: pl.BlockSpec(a.shape, lambda i: (0,) * a.ndim)
    return pl.pallas_call(
        functools.partial(_mixer_out_kernel, alpha=alpha),
        grid=(m // tm,),
        in_specs=[ospec, ospec, ospec, zspec(ZA), zspec(ZB), zspec(ZC), xspec,
                  full(wpa), full(wpb), full(wpc), full(wo), full(g), full(b)],
        out_specs=xspec,
        out_shape=jax.ShapeDtypeStruct((m, D_MODEL), F32),
        compiler_params=_cparams(("parallel",)))(oa, og, orr, h, h, h, x, wpa, wpb, wpc, wo, g, b)


def _ffn_up_kernel(x_ref, wg_ref, wu_ref, h_ref):
    x = x_ref[...].astype(BF16)
    a = jnp.dot(x, wg_ref[...], preferred_element_type=F32)
    u = jnp.dot(x, wu_ref[...], preferred_element_type=F32)
    h_ref[...] = (_silu(a) * u).astype(h_ref.dtype)


def _ffn_down_kernel(h_ref, wd_ref, x_ref, g_ref, b_ref, y_ref, *, alpha):
    y = alpha * x_ref[...] + jnp.dot(h_ref[...], wd_ref[...], preferred_element_type=F32)
    y_ref[...] = _layer_norm(y, g_ref[...], b_ref[...])


def dense_ffn(x, wg, wu, wd, g, b, alpha):
    m = x.shape[0]
    f = wg.shape[1]
    tm, tf = _tile(m, 512), _tile(f, 1408)
    hmid = pl.pallas_call(
        _ffn_up_kernel, grid=(m // tm, f // tf),
        in_specs=[pl.BlockSpec((tm, D_MODEL), lambda i, j: (i, 0)),
                  pl.BlockSpec((D_MODEL, tf), lambda i, j: (0, j)),
                  pl.BlockSpec((D_MODEL, tf), lambda i, j: (0, j))],
        out_specs=pl.BlockSpec((tm, tf), lambda i, j: (i, j)),
        out_shape=jax.ShapeDtypeStruct((m, f), BF16),
        compiler_params=_cparams(("parallel", "parallel")))(x, wg, wu)
    full = lambda a: pl.BlockSpec(a.shape, lambda i: (0,) * a.ndim)
    xspec = pl.BlockSpec((tm, D_MODEL), lambda i: (i, 0))
    return pl.pallas_call(
        functools.partial(_ffn_down_kernel, alpha=alpha), grid=(m // tm,),
        in_specs=[pl.BlockSpec((tm, f), lambda i: (i, 0)), full(wd), xspec, full(g), full(b)],
        out_specs=xspec,
        out_shape=jax.ShapeDtypeStruct((m, D_MODEL), F32),
        compiler_params=_cparams(("parallel",)))(hmid, wd, x, g, b)


def _top2_weights(x, wh, wl):
    xh = x.astype(BF16)
    xl = (x - xh.astype(F32)).astype(BF16)
    logits = (jnp.dot(xh, wh, preferred_element_type=F32) + jnp.dot(xh, wl, preferred_element_type=F32)
              + jnp.dot(xl, wh, preferred_element_type=F32))
    lane = lax.broadcasted_iota(jnp.int32, logits.shape, 1)
    logits = jnp.where(lane < N_EXPERTS, logits, NEG_BIG)
    m1 = jnp.max(logits, axis=-1, keepdims=True)
    i1 = jnp.min(jnp.where(logits == m1, lane, LANES), axis=-1, keepdims=True)
    rest = jnp.where(lane == i1, NEG_BIG, logits)
    m2 = jnp.max(rest, axis=-1, keepdims=True)
    i2 = jnp.min(jnp.where(rest == m2, lane, LANES), axis=-1, keepdims=True)
    e2 = jnp.exp(m2 - m1)
    w1 = 1.0 / (1.0 + e2)
    w2 = e2 / (1.0 + e2)
    return jnp.where(lane == i1, w1, jnp.where(lane == i2, w2, 0.0))


def _router_kernel(x_ref, wh_ref, wl_ref, comb_ref):
    comb_ref[...] = _top2_weights(x_ref[...], wh_ref[...], wl_ref[...])


def _router_rank_kernel(x_ref, wh_ref, wl_ref, comb_ref, rankc_ref, rankr_ref, cnt_ref):
    comb = _top2_weights(x_ref[...], wh_ref[...], wl_ref[...])
    tb = comb.shape[0]
    routed = comb > 0.0
    ones = jnp.where(routed, 1.0, 0.0)
    earlier = (lax.broadcasted_iota(jnp.int32, (tb, tb), 0) > lax.broadcasted_iota(jnp.int32, (tb, tb), 1))
    rank = jnp.dot(jnp.where(earlier, 1.0, 0.0).astype(BF16), ones.astype(BF16), preferred_element_type=F32)
    rankc = jnp.where(routed, rank, -1.0)
    comb_ref[...] = comb
    rankc_ref[...] = rankc
    cnt_ref[...] = jnp.broadcast_to(jnp.sum(ones, axis=0, keepdims=True), cnt_ref.shape)
    eye = (lax.broadcasted_iota(jnp.int32, (LANES, LANES), 0) == lax.broadcasted_iota(jnp.int32, (LANES, LANES), 1))
    eye = jnp.where(eye, 1.0, 0.0).astype(BF16)
    rt = jnp.zeros((LANES, tb), F32)
    for part in _split3(rankc):
        rt = rt + lax.dot_general(eye, part, (((1,), (1,)), ((), ())), preferred_element_type=F32)
    rankr_ref[...] = rt[:N_EXPERTS]


def _router_weights(w_router):
    wpad = jnp.zeros((D_MODEL, LANES), F32).at[:, :N_EXPERTS].set(w_router)
    wh = wpad.astype(BF16)
    return wh, (wpad - wh.astype(F32)).astype(BF16)


def router(x, w_router):
    m = x.shape[0]
    tm = _tile(m, 512)
    wh, wl = _router_weights(w_router)
    full = lambda a: pl.BlockSpec(a.shape, lambda i: (0,) * a.ndim)
    return pl.pallas_call(
        _router_kernel, grid=(m // tm,),
        in_specs=[pl.BlockSpec((tm, D_MODEL), lambda i: (i, 0)), full(wh), full(wl)],
        out_specs=pl.BlockSpec((tm, LANES), lambda i: (i, 0)),
        out_shape=jax.ShapeDtypeStruct((m, LANES), F32),
        compiler_params=_cparams(("parallel",)))(x, wh, wl)


def router_ranked(x, w_router, tb):
    m = x.shape[0]
    nb = m // tb
    wh, wl = _router_weights(w_router)
    full = lambda a: pl.BlockSpec(a.shape, lambda i: (0,) * a.ndim)
    tok = pl.BlockSpec((tb, LANES), lambda i: (i, 0))
    return pl.pallas_call(
        _router_rank_kernel, grid=(nb,),
        in_specs=[pl.BlockSpec((tb, D_MODEL), lambda i: (i, 0)), full(wh), full(wl)],
        out_specs=[tok, tok, pl.BlockSpec((None, N_EXPERTS, tb), lambda i: (i, 0, 0)),
                   pl.BlockSpec((None, N_EXPERTS, LANES), lambda i: (i, 0, 0))],
        out_shape=[jax.ShapeDtypeStruct((m, LANES), F32), jax.ShapeDtypeStruct((m, LANES), F32),
                   jax.ShapeDtypeStruct((nb, N_EXPERTS, tb), F32),
                   jax.ShapeDtypeStruct((nb, N_EXPERTS, LANES), F32)],
        compiler_params=_cparams(("parallel",)))(x, wh, wl)


def _moe_kernel(x_ref, comb_ref, wg_ref, wu_ref, wd_ref, g_ref, b_ref, y_ref, acc_ref, *, alpha):
    e, f = pl.program_id(1), pl.program_id(2)

    @pl.when((e == 0) & (f == 0))
    def _():
        acc_ref[...] = jnp.zeros(acc_ref.shape, F32)

    x = x_ref[...].astype(BF16)
    a = jnp.dot(x, wg_ref[...], preferred_element_type=F32)
    u = jnp.dot(x, wu_ref[...], preferred_element_type=F32)
    hmid = (_silu(a) * u).astype(BF16)
    comb = comb_ref[...]
    lane = lax.broadcasted_iota(jnp.int32, comb.shape, 1)
    ce = jnp.sum(jnp.where(lane == e, comb, 0.0), axis=-1, keepdims=True)
    acc_ref[...] += ce * jnp.dot(hmid, wd_ref[...], preferred_element_type=F32)

    @pl.when((e == pl.num_programs(1) - 1) & (f == pl.num_programs(2) - 1))
    def _():
        y_ref[...] = _layer_norm(alpha * x_ref[...] + acc_ref[...], g_ref[...], b_ref[...])


def moe_ffn(x, comb, wg, wu, wd, g, b, alpha):
    m = x.shape[0]
    ne, _, f = wg.shape
    tm, tf = _tile(m, 1024), _tile(f, 512)
    xspec = pl.BlockSpec((tm, D_MODEL), lambda i, e, j: (i, 0))
    full = lambda a: pl.BlockSpec(a.shape, lambda i, e, j: (0,) * a.ndim)
    return pl.pallas_call(
        functools.partial(_moe_kernel, alpha=alpha), grid=(m // tm, ne, f // tf),
        in_specs=[xspec, pl.BlockSpec((tm, LANES), lambda i, e, j: (i, 0)),
                  pl.BlockSpec((None, D_MODEL, tf), lambda i, e, j: (e, 0, j)),
                  pl.BlockSpec((None, D_MODEL, tf), lambda i, e, j: (e, 0, j)),
                  pl.BlockSpec((None, tf, D_MODEL), lambda i, e, j: (e, j, 0)),
                  full(g), full(b)],
        out_specs=xspec,
        out_shape=jax.ShapeDtypeStruct((m, D_MODEL), F32),
        scratch_shapes=[pltpu.VMEM((tm, D_MODEL), F32)],
        compiler_params=_cparams(("parallel", "arbitrary", "arbitrary")))(x, comb, wg, wu, wd, g, b)


def _moe_routed_kernel(cnt_ref, x_ref, comb_ref, rankc_ref, rankr_ref, wg_ref, wu_ref, wd_ref, g_ref, b_ref,
                       y_ref, acc_ref, xb_ref, xc_ref, yacc_ref, *, alpha, rows, cap):
    blk, e, f = pl.program_id(0), pl.program_id(1), pl.program_id(2)
    last_f = pl.num_programs(2) - 1
    n = cnt_ref[blk * N_EXPERTS + e]
    tb = x_ref.shape[0]

    @pl.when((e == 0) & (f == 0))
    def _():
        acc_ref[...] = jnp.zeros(acc_ref.shape, F32)
        xb_ref[...] = x_ref[...].astype(BF16)

    def column(ref):
        a = ref[...]
        lane = lax.broadcasted_iota(jnp.int32, a.shape, 1)
        return jnp.sum(jnp.where(lane == e, a, 0.0), axis=-1, keepdims=True)

    def expert(xs):
        a = jnp.dot(xs, wg_ref[...], preferred_element_type=F32)
        u = jnp.dot(xs, wu_ref[...], preferred_element_type=F32)
        return jnp.dot((_silu(a) * u).astype(BF16), wd_ref[...], preferred_element_type=F32)

    @pl.when((n > 0) & (n <= rows))
    def _():
        @pl.when(f == 0)
        def _():
            rr = rankr_ref[...]
            sub = lax.broadcasted_iota(jnp.int32, rr.shape, 0)
            rrow = jnp.sum(jnp.where(sub == e, rr, 0.0), axis=0, keepdims=True)
            slot = lax.broadcasted_iota(jnp.int32, (rows, tb), 0).astype(F32)
            gather = jnp.where(slot == rrow, 1.0, 0.0).astype(BF16)
            xc_ref[...] = jnp.dot(gather, xb_ref[...], preferred_element_type=F32).astype(BF16)
            yacc_ref[0:cap, :] = jnp.zeros((cap, D_MODEL), F32)

        yacc_ref[0:rows, :] += expert(xc_ref[...])

        @pl.when(f == last_f)
        def _():
            slot = lax.broadcasted_iota(jnp.int32, (tb, cap), 1).astype(F32)
            scatter = jnp.where(slot == column(rankc_ref), 1.0, 0.0).astype(BF16)
            acc_ref[...] += column(comb_ref) * jnp.dot(scatter, yacc_ref[0:cap, :].astype(BF16),
                                                       preferred_element_type=F32)

    @pl.when(n > rows)
    def _():
        @pl.when(f == 0)
        def _():
            yacc_ref[...] = jnp.zeros(yacc_ref.shape, F32)

        yacc_ref[...] += expert(xb_ref[...])

        @pl.when(f == last_f)
        def _():
            acc_ref[...] += column(comb_ref) * yacc_ref[...]

    @pl.when((e == pl.num_programs(1) - 1) & (f == last_f))
    def _():
        y_ref[...] = _layer_norm(alpha * x_ref[...] + acc_ref[...], g_ref[...], b_ref[...])


def moe_routed_ffn(x, w_router, wg, wu, wd, g, b, alpha):
    m = x.shape[0]
    ne, _, f = wg.shape
    tb, tf = _tile(m, MOE_BLOCK), _tile(f, MOE_TF)
    rows = min(tb, MOE_ROWS)
    cap = min(tb, -(-rows // LANES) * LANES)
    comb, rankc, rankr, cnt = router_ranked(x, w_router, tb)
    counts = cnt[:, 0, :N_EXPERTS].astype(jnp.int32).reshape(-1)
    xspec = pl.BlockSpec((tb, D_MODEL), lambda i, e, j, c: (i, 0))
    tok = pl.BlockSpec((tb, LANES), lambda i, e, j, c: (i, 0))
    full = lambda a: pl.BlockSpec(a.shape, lambda i, e, j, c: (0,) * a.ndim)
    grid_spec = pltpu.PrefetchScalarGridSpec(
        num_scalar_prefetch=1, grid=(m // tb, ne, f // tf),
        in_specs=[xspec, tok, tok, pl.BlockSpec((None, N_EXPERTS, tb), lambda i, e, j, c: (i, 0, 0)),
                  pl.BlockSpec((None, D_MODEL, tf), lambda i, e, j, c: (e, 0, j)),
                  pl.BlockSpec((None, D_MODEL, tf), lambda i, e, j, c: (e, 0, j)),
                  pl.BlockSpec((None, tf, D_MODEL), lambda i, e, j, c: (e, j, 0)),
                  full(g), full(b)],
        out_specs=xspec,
        scratch_shapes=[pltpu.VMEM((tb, D_MODEL), F32), pltpu.VMEM((tb, D_MODEL), BF16),
                        pltpu.VMEM((rows, D_MODEL), BF16), pltpu.VMEM((tb, D_MODEL), F32)])
    return pl.pallas_call(
        functools.partial(_moe_routed_kernel, alpha=alpha, rows=rows, cap=cap),
        grid_spec=grid_spec,
        out_shape=jax.ShapeDtypeStruct((m, D_MODEL), F32),
        compiler_params=_cparams(("parallel", "arbitrary", "arbitrary")))(
            counts, x, comb, rankc, rankr, wg, wu, wd, g, b)


def _layer_weights(l, w_in, lam_q1, lam_k1, lam_q2, lam_k2, subln_w, w_gla_up, b_gla_up, gla_norm_w,
                   ret_norm_w, w_pa, w_pb, w_pc, w_out, ln1_g, ln1_b, ln2_g, ln2_b):
    wl = w_in[l]
    w_main = jnp.concatenate([wl[:, a:b] for a, b in PACK_ORDER], axis=1).astype(BF16)
    w_lrg = jnp.zeros((D_MODEL, LANES), F32).at[:, :G_RANK].set(wl[:, LRG_OFF:LRG_OFF + G_RANK]).astype(BF16)
    w_up = jnp.zeros((LANES, HK), F32).at[:G_RANK].set(w_gla_up[l]).astype(BF16)
    lamvec = jnp.zeros((8, LANES), F32)
    for r, vec in enumerate((lam_q1, lam_k1, lam_q2, lam_k2)):
        lamvec = lamvec.at[r, :A_QK].set(vec[l].astype(F32))
    row = lambda a: a[l].reshape(1, -1)
    return dict(w_main=w_main, w_lrg=w_lrg, w_up=w_up, b_up=row(b_gla_up), lamvec=lamvec,
                subln=row(subln_w), gla_nw=row(gla_norm_w), ret_nw=row(ret_norm_w),
                wpa=w_pa[l].astype(BF16), wpb=w_pb[l].astype(BF16), wpc=w_pc[l].astype(BF16),
                wo=w_out[l].astype(BF16), ln1_g=row(ln1_g), ln1_b=row(ln1_b),
                ln2_g=row(ln2_g), ln2_b=row(ln2_b), lam_init=0.8 - 0.6 * math.exp(-0.3 * l))


def _project(x, pos, lw, q_dtype, q_scale):
    h = matmul(x, lw['w_main'], tn_cap=INPROJ_TN)
    lrg = matmul(x, lw['w_lrg'], tn_cap=LANES)
    qa, ka = rotary(h, QA, A_HEADS * LANES, _rope_tables(pos), ROPE_DIM // 2, q_scale, 1.0, q_dtype)
    qr, kr = rotary(h, QR, HK, _ret_tables(pos), 1, 1.0, R_QK ** -0.5, F32)
    return h, lrg, qa, ka, qr, kr


def _channel_mix(x1, l, lw, ffn_w, alpha):
    if l % 2 == 0:
        wg, wu, wd = ffn_w['dense'][l // 2]
        return dense_ffn(x1, wg, wu, wd, lw['ln2_g'], lw['ln2_b'], alpha)
    w_r, wg, wu, wd = ffn_w['moe'][l // 2]
    if x1.shape[0] >= MOE_MIN_ROUTED:
        return moe_routed_ffn(x1, w_r, wg, wu, wd, lw['ln2_g'], lw['ln2_b'], alpha)
    comb = router(x1, w_r)
    return moe_ffn(x1, comb, wg, wu, wd, lw['ln2_g'], lw['ln2_b'], alpha)


def kernel(x_prompt, x_sample, cache_k, cache_v, state_gla, state_ret, page_table, w_in, lam_q1, lam_k1, lam_q2, lam_k2, subln_w, w_gla_up, b_gla_up, gla_norm_w, ret_norm_w, w_pa, w_pb, w_pc, w_out, ln1_g, ln1_b, w_ff_gate, w_ff_up, w_ff_down, w_router, w_exp_gate, w_exp_up, w_exp_down, ln2_g, ln2_b):
    bp, tp, _ = x_prompt.shape
    bs, ts, _ = x_sample.shape
    assert bp == 1 and ts == 1
    depth = w_in.shape[0]
    alpha = (2 * depth) ** 0.25
    n_pool, page = cache_k.shape[1], cache_k.shape[2]
    past_len = page_table.shape[1] * page
    pos_p = jnp.arange(tp, dtype=jnp.int32)
    pos_s = jnp.full((bs,), past_len, jnp.int32)
    ffn_w = dict(
        dense=[(w_ff_gate[i].astype(BF16), w_ff_up[i].astype(BF16), w_ff_down[i].astype(BF16))
               for i in range(w_ff_gate.shape[0])],
        moe=[(w_router[i], w_exp_gate[i].astype(BF16), w_exp_up[i].astype(BF16), w_exp_down[i].astype(BF16))
             for i in range(w_router.shape[0])])
    ck = cache_k.reshape(depth, n_pool, page * A_HEADS, LANES)
    cv = cache_v.reshape(depth, n_pool, page * A_HEADS, LANES)
    lg_col = jnp.broadcast_to(jnp.repeat(_ret_log_gamma(), R_QK)[None, :], (bs, HK))

    yp = x_prompt.reshape(tp, D_MODEL)
    ys = x_sample.reshape(bs, D_MODEL)
    outs = {n: [] for n in ('kp', 'vp', 'gp', 'rp', 'ks', 'vs', 'gs', 'rs')}
    cols = lambda a, blk, n: a[:, blk * LANES:(blk + n) * LANES]
    for l in range(depth):
        lw = _layer_weights(l, w_in, lam_q1, lam_k1, lam_q2, lam_k2, subln_w, w_gla_up, b_gla_up,
                            gla_norm_w, ret_norm_w, w_pa, w_pb, w_pc, w_out, ln1_g, ln1_b, ln2_g, ln2_b)
        h, lrg, qa, ka, qr, kr = _project(yp, pos_p, lw, BF16, A_QK ** -0.5 * LOG2E)
        oa = flash_diff_attention(qa, ka, h, lw['lamvec'], lw['subln'], lw['lam_init'])
        og, s_gla = gla_prompt(h, lrg, lw['w_up'], lw['b_up'], lw['gla_nw'])
        orr, s_ret = ret_prompt(h, qr, kr, lw['ret_nw'])
        x1 = mixer_out(oa, og, orr, h, yp, lw['wpa'], lw['wpb'], lw['wpc'], lw['wo'],
                       lw['ln1_g'], lw['ln1_b'], alpha)
        yp = _channel_mix(x1, l, lw, ffn_w, alpha)
        outs['kp'].append(ka.reshape(1, tp, A_HEADS, 2 * A_QK))
        outs['vp'].append(cols(h, VA, 4).reshape(1, tp, A_HEADS, 2 * A_QK))
        outs['gp'].append(s_gla[None])
        outs['rp'].append(s_ret[None])
        h, lrg, qa, ka, qr, kr = _project(ys, pos_s, lw, F32, A_QK ** -0.5)
        va = cols(h, VA, 4)
        oa = decode_diff_attention(qa, ka, va, ck, cv, l, page_table, lw['lamvec'], lw['subln'],
                                   lw['lam_init'])
        gk = matmul(lrg, lw['w_up'], bias=lw['b_up'])
        og, s_gla = recurrent_step(state_gla[l], cols(h, QG, 2), cols(h, KG, 2), gk, cols(h, VG, 4),
                                   cols(h, RG, 4), lw['gla_nw'], True, G_QK ** -0.5)
        orr, s_ret = recurrent_step(state_ret[l], qr, kr, lg_col, cols(h, VR, 4), cols(h, GR, 4),
                                    lw['ret_nw'], False, 1.0)
        x1 = mixer_out(oa, og, orr, h, ys, lw['wpa'], lw['wpb'], lw['wpc'], lw['wo'],
                       lw['ln1_g'], lw['ln1_b'], alpha)
        ys = _channel_mix(x1, l, lw, ffn_w, alpha)
        outs['ks'].append(ka.reshape(bs, 1, A_HEADS, 2 * A_QK))
        outs['vs'].append(va.reshape(bs, 1, A_HEADS, 2 * A_QK))
        outs['gs'].append(s_gla)
        outs['rs'].append(s_ret)

    st = lambda n: jnp.stack(outs[n])
    return (yp.reshape(bp, tp, D_MODEL), ys.reshape(bs, ts, D_MODEL), st('kp'), st('vp'), st('gp'),
            st('rp'), st('ks'), st('vs'), st('gs'), st('rs'))
```

```python
import functools
import math

import numpy as np
import jax
import jax.numpy as jnp
from jax import lax
from jax.experimental import pallas as pl
from jax.experimental.pallas import tpu as pltpu

F32 = jnp.float32
BF16 = jnp.bfloat16

D_MODEL = 1024
A_HEADS = 4
A_QK = 64
ROPE_DIM = 16
ROPE_THETA = 500000.0
G_HEADS = 4
G_QK = 64
G_V = 128
G_RANK = 16
G_NORMALIZER = 16.0
R_HEADS = 4
R_QK = 64
R_ANGLE_BASE = 10000.0
CHUNK = 64
N_EXPERTS = 8
NORM_EPS = 1e-5
NEG_BIG = -1e30
LOG2E = math.log2(math.e)
FLASH_T = 1024
FLASH_RC = 32
FLASH_KCOLS = 256
FLASH_PVROWS = 128
DECODE_PAGES = 16
SCAN_UNROLL = 4
MOE_BLOCK = 1024
MOE_ROWS = 320
MOE_MIN_ROUTED = 256
MOE_TF = 896
INPROJ_TN = 1536

LANES = 128
VMEM_LIMIT = 56 * 1024 * 1024

ZA, ZB, ZC, QA, KA, VA, VG, RG, VR, GR, QG, KG, QR, KR = 0, 8, 16, 24, 28, 32, 36, 40, 44, 48, 52, 54, 56, 58
PACK_ORDER = ((4624, 7696), (0, 1536), (2048, 2560), (2576, 3088), (3600, 4112), (4112, 4624),
              (1536, 2048), (3088, 3600))
LRG_OFF = 2560


def _cparams(sem):
    return pltpu.CompilerParams(dimension_semantics=sem, vmem_limit_bytes=VMEM_LIMIT)


def _tile(n, cap):
    c = min(n, cap)
    while n % c:
        c -= 1
    return c


def _layer_norm(y, g, b):
    mu = jnp.mean(y, axis=-1, keepdims=True)
    d = y - mu
    var = jnp.mean(d * d, axis=-1, keepdims=True)
    return d * lax.rsqrt(var + NORM_EPS) * g + b


def _rms(o, w):
    return o * lax.rsqrt(jnp.mean(o * o, axis=-1, keepdims=True) + NORM_EPS) * w


def _silu(x):
    return x * (1.0 / (1.0 + jnp.exp(-x)))


def _sigmoid(x):
    return 1.0 / (1.0 + jnp.exp(-x))


def _log_sigmoid(x):
    return jnp.minimum(x, 0.0) - jnp.log(1.0 + jnp.exp(-jnp.abs(x)))


def _split3(x):
    a = x.astype(BF16)
    r = x - a.astype(F32)
    b = r.astype(BF16)
    c = (r - b.astype(F32)).astype(BF16)
    return a, b, c


def _mm_kernel(x_ref, w_ref, o_ref):
    o_ref[...] = jnp.dot(x_ref[...].astype(BF16), w_ref[...],
                         preferred_element_type=F32).astype(o_ref.dtype)


def _mm_bias_kernel(x_ref, w_ref, b_ref, o_ref):
    o_ref[...] = (jnp.dot(x_ref[...].astype(BF16), w_ref[...],
                          preferred_element_type=F32) + b_ref[...]).astype(o_ref.dtype)


def matmul(x, w, bias=None, out_dtype=F32, tm_cap=1024, tn_cap=768):
    m, k = x.shape
    n = w.shape[1]
    tm, tn = _tile(m, tm_cap), _tile(n, tn_cap)
    in_specs = [pl.BlockSpec((tm, k), lambda i, j: (i, 0)),
                pl.BlockSpec((k, tn), lambda i, j: (0, j))]
    args = [x, w]
    kern = _mm_kernel
    if bias is not None:
        in_specs.append(pl.BlockSpec((1, tn), lambda i, j: (0, j)))
        args.append(bias)
        kern = _mm_bias_kernel
    return pl.pallas_call(
        kern, grid=(m // tm, n // tn), in_specs=in_specs,
        out_specs=pl.BlockSpec((tm, tn), lambda i, j: (i, j)),
        out_shape=jax.ShapeDtypeStruct((m, n), out_dtype),
        compiler_params=_cparams(("parallel", "parallel")))(*args)


def _rot_kernel(x_ref, c_ref, s1_ref, s2_ref, oq_ref, ok_ref, *, shift, q_scale, k_scale):
    c, s1, s2 = c_ref[...], s1_ref[...], s2_ref[...]
    nq = oq_ref.shape[1] // LANES
    nk = ok_ref.shape[1] // LANES
    for b in range(nq + nk):
        x = x_ref[:, b * LANES:(b + 1) * LANES]
        y = x * c + pltpu.roll(x, LANES - shift, 1) * s1 + pltpu.roll(x, shift, 1) * s2
        if b < nq:
            oq_ref[:, b * LANES:(b + 1) * LANES] = (y * q_scale).astype(oq_ref.dtype)
        else:
            ok_ref[:, (b - nq) * LANES:(b - nq + 1) * LANES] = (y * k_scale).astype(ok_ref.dtype)


def rotary(h, col_blk, width, tables, shift, q_scale, k_scale, q_dtype):
    m = h.shape[0]
    tm = _tile(m, 512)
    blk = col_blk * LANES // (2 * width)
    tspec = pl.BlockSpec((tm, LANES), lambda i: (i, 0))
    return pl.pallas_call(
        functools.partial(_rot_kernel, shift=shift, q_scale=q_scale, k_scale=k_scale),
        grid=(m // tm,),
        in_specs=[pl.BlockSpec((tm, 2 * width), lambda i: (i, blk)), tspec, tspec, tspec],
        out_specs=[pl.BlockSpec((tm, width), lambda i: (i, 0)),
                   pl.BlockSpec((tm, width), lambda i: (i, 0))],
        out_shape=[jax.ShapeDtypeStruct((m, width), q_dtype),
                   jax.ShapeDtypeStruct((m, width), F32)],
        compiler_params=_cparams(("parallel",)))(h, *tables)


def _rope_tables(pos):
    half = ROPE_DIM // 2
    inv = ROPE_THETA ** (-jnp.arange(half, dtype=F32) * 2.0 / ROPE_DIM)
    ang = pos.astype(F32)[:, None] * inv[None, :]
    cos, sin = jnp.cos(ang), jnp.sin(ang)
    lane = np.arange(LANES) % A_QK
    fi = lane % half
    cos_l, sin_l = cos[:, fi], sin[:, fi]
    in_rot = (lane < ROPE_DIM)[None, :]
    lo = (lane < half)[None, :]
    c = jnp.where(in_rot, cos_l, 1.0)
    s1 = jnp.where(lo, -sin_l, 0.0)
    s2 = jnp.where(in_rot & ~lo, sin_l, 0.0)
    return c, s1, s2


def _ret_tables(pos):
    inv = 1.0 / (R_ANGLE_BASE ** jnp.linspace(0.0, 1.0, R_QK // 2, dtype=F32))
    ang = pos.astype(F32)[:, None] * inv[None, :]
    cos, sin = jnp.cos(ang), jnp.sin(ang)
    lane = np.arange(LANES) % R_QK
    cos_l, sin_l = cos[:, lane // 2], sin[:, lane // 2]
    even = (lane % 2 == 0)[None, :]
    return cos_l, jnp.where(even, -sin_l, 0.0), jnp.where(even, 0.0, sin_l)


def _lam_from(lam_ref, lam_init):
    v = lam_ref[...]
    t1 = jnp.sum(v[0:1] * v[1:2], axis=-1, keepdims=True)
    t2 = jnp.sum(v[2:3] * v[3:4], axis=-1, keepdims=True)
    return jnp.exp(t1) - jnp.exp(t2) + lam_init


def _flash_kernel(qi_ref, kj_ref, lam_ref, q_ref, k_ref, v_ref, w_ref, o_ref, m_ref, acc_ref, sa_ref, sb_ref,
                  p_ref, *, t, rc, lam_init):
    step = pl.program_id(1)
    i, j = qi_ref[step], kj_ref[step]
    nch = t // LANES

    kw = min(t, FLASH_KCOLS)
    pvr = min(t, FLASH_PVROWS)

    def produce_items(s_dst):
        items = []
        for c in range(2):
            for kc in range(t // kw):
                def item(c=c, kc=kc):
                    q = q_ref[...]
                    lane = lax.broadcasted_iota(jnp.int32, q.shape, 1)
                    qc = jnp.where((lane < A_QK) if c == 0 else (lane >= A_QK), q, jnp.zeros_like(q))
                    kk = k_ref[kc * kw:(kc + 1) * kw, :].astype(BF16)
                    s_dst[c, :, kc * kw:(kc + 1) * kw] = lax.dot_general(
                        qc, kk, (((1,), (1,)), ((), ())), preferred_element_type=F32)
                items.append(item)
        return items

    def consume_items(s_src, masked):
        items = []
        for g in range(t // pvr):
            def sweep(g=g):
                for r in range(g * pvr // rc, (g + 1) * pvr // rc):
                    rows = slice(r * rc, (r + 1) * rc)
                    if masked:
                        keep = (lax.broadcasted_iota(jnp.int32, (rc, t), 1)
                                <= r * rc + lax.broadcasted_iota(jnp.int32, (rc, t), 0))
                    for c in range(2):
                        s = s_src[c, rows, :]
                        if masked:
                            s = jnp.where(keep, s, NEG_BIG)
                        m_prev = m_ref[c, rows, :]
                        m_new = jnp.maximum(m_prev, jnp.max(s, axis=-1, keepdims=True))
                        p = jnp.exp2(s - jnp.concatenate([m_new] * nch, axis=1))
                        alpha = jnp.exp2(m_prev - m_new)
                        acc_ref[c, rows, :] = jnp.concatenate([alpha, alpha], axis=1) * acc_ref[c, rows, :]
                        m_ref[c, rows, :] = m_new
                        p_ref[c, rows, :] = p.astype(BF16)

            def pv(g=g):
                v = jnp.concatenate([v_ref[...].astype(BF16), jnp.ones((t, LANES), BF16)], axis=1)
                rows = slice(g * pvr, (g + 1) * pvr)
                for c in range(2):
                    acc_ref[c, rows, :] += jnp.dot(p_ref[c, rows, :], v, preferred_element_type=F32)
            items += [sweep, pv]
        return items

    def run(*item_lists):
        longest = max(len(l) for l in item_lists)
        for n in range(longest):
            for l in item_lists:
                lo, hi = n * len(l) // longest, (n + 1) * len(l) // longest
                for it in l[lo:hi]:
                    it()

    def produce(s_dst):
        run(produce_items(s_dst))

    def consume(s_src, masked):
        run(consume_items(s_src, masked))

    even = j % 2 == 0

    @pl.when(j == 0)
    def _():
        m_ref[...] = jnp.full(m_ref.shape, NEG_BIG, F32)
        acc_ref[...] = jnp.zeros(acc_ref.shape, F32)
        produce(sa_ref)

    steady = (j >= 1) & (j <= i)

    @pl.when(steady & even)
    def _():
        run(consume_items(sb_ref, False), produce_items(sa_ref))

    @pl.when(steady & jnp.logical_not(even))
    def _():
        run(consume_items(sa_ref, False), produce_items(sb_ref))

    drain = j == i + 1
    pl.when(drain & even)(functools.partial(consume, sb_ref, True))
    pl.when(drain & jnp.logical_not(even))(functools.partial(consume, sa_ref, True))

    @pl.when(drain)
    def _():
        lam = _lam_from(lam_ref, lam_init)
        o = (acc_ref[0, :, :LANES] / acc_ref[0, :, LANES:]
             - lam * (acc_ref[1, :, :LANES] / acc_ref[1, :, LANES:]))
        o_ref[...] = _rms(o, w_ref[...]) * (1.0 - lam_init)


def flash_diff_attention(q, k, h, lamvec, subln_w, lam_init, t_cap=FLASH_T, rc=FLASH_RC):
    n = q.shape[0]
    t = _tile(n, t_cap)
    rc = _tile(t, rc)
    nb = n // t
    pairs = [(i, j) for i in range(nb) for j in range(i + 2)]
    qi = jnp.asarray([p[0] for p in pairs], jnp.int32)
    kj = jnp.asarray([p[1] for p in pairs], jnp.int32)
    grid_spec = pltpu.PrefetchScalarGridSpec(
        num_scalar_prefetch=2, grid=(A_HEADS, len(pairs)),
        in_specs=[pl.BlockSpec((8, LANES), lambda hh, s, qi, kj: (0, 0)),
                  pl.BlockSpec((t, LANES), lambda hh, s, qi, kj: (qi[s], hh)),
                  pl.BlockSpec((t, LANES), lambda hh, s, qi, kj: (jnp.minimum(kj[s], qi[s]), hh)),
                  pl.BlockSpec((t, LANES), lambda hh, s, qi, kj: (jnp.clip(kj[s] - 1, 0, qi[s]), VA + hh)),
                  pl.BlockSpec((1, LANES), lambda hh, s, qi, kj: (0, 0))],
        out_specs=pl.BlockSpec((t, LANES), lambda hh, s, qi, kj: (qi[s], hh)),
        scratch_shapes=[pltpu.VMEM((2, t, LANES), F32), pltpu.VMEM((2, t, 2 * LANES), F32),
                        pltpu.VMEM((2, t, t), F32), pltpu.VMEM((2, t, t), F32),
                        pltpu.VMEM((2, t, t), BF16)])
    return pl.pallas_call(
        functools.partial(_flash_kernel, t=t, rc=rc, lam_init=lam_init),
        grid_spec=grid_spec,
        out_shape=jax.ShapeDtypeStruct((n, A_HEADS * LANES), F32),
        compiler_params=_cparams(("parallel", "arbitrary")))(qi, kj, lamvec, q, k, h, subln_w)


def _decode_kernel(pt_ref, lam_ref, q_ref, kn_ref, vn_ref, w_ref, *rest, pp, lam_init):
    k_refs, v_refs = rest[:pp], rest[pp:2 * pp]
    o_ref, m_ref, l_ref, acc_ref = rest[2 * pp:]
    c = pl.program_id(1)
    rows = 2 * A_HEADS
    prow = k_refs[0].shape[0]

    @pl.when(c == 0)
    def _():
        m_ref[...] = jnp.full(m_ref.shape, NEG_BIG, F32)
        l_ref[...] = jnp.zeros(l_ref.shape, F32)
        acc_ref[...] = jnp.zeros(acc_ref.shape, F32)

    rid = lax.broadcasted_iota(jnp.int32, (rows, LANES), 0)
    lid = lax.broadcasted_iota(jnp.int32, (rows, LANES), 1)
    qb = jnp.where(lid // A_QK == rid % 2, q_ref[...], 0.0).astype(BF16)
    s = jnp.concatenate(
        [lax.dot_general(qb, k_refs[p][...].astype(BF16), (((1,), (1,)), ((), ())),
                         preferred_element_type=F32) for p in range(pp)], axis=-1)
    srow = lax.broadcasted_iota(jnp.int32, s.shape, 0)
    scol = lax.broadcasted_iota(jnp.int32, s.shape, 1)
    s = jnp.where(scol % A_HEADS == srow // 2, s, NEG_BIG)
    m_prev = m_ref[...]
    m_new = jnp.maximum(m_prev, jnp.max(s, axis=-1, keepdims=True))
    p_ = jnp.exp(s - m_new)
    alpha = jnp.exp(m_prev - m_new)
    l_new = alpha * l_ref[...] + jnp.sum(p_, axis=-1, keepdims=True)
    acc = alpha * acc_ref[...]
    for p in range(pp):
        acc = acc + jnp.dot(p_[:, p * prow:(p + 1) * prow].astype(BF16), v_refs[p][...].astype(BF16),
                            preferred_element_type=F32)
    m_ref[...] = m_new
    l_ref[...] = l_new
    acc_ref[...] = acc

    @pl.when(c == pl.num_programs(1) - 1)
    def _():
        kn = kn_ref[...].astype(BF16).astype(F32)
        vn = vn_ref[...].astype(BF16).astype(F32)
        s_self = jnp.sum(qb.astype(F32) * kn, axis=-1, keepdims=True)
        m_fin = jnp.maximum(m_new, s_self)
        a2 = jnp.exp(m_new - m_fin)
        p_self = jnp.exp(s_self - m_fin)
        l_fin = a2 * l_new + p_self
        acc_fin = a2 * acc + p_self.astype(BF16).astype(F32) * vn
        on = acc_fin / l_fin
        lam = _lam_from(lam_ref, lam_init)
        w = w_ref[...]
        for hh in range(A_HEADS):
            o = on[2 * hh:2 * hh + 1] - lam * on[2 * hh + 1:2 * hh + 2]
            o_ref[hh:hh + 1, :] = _rms(o, w) * (1.0 - lam_init)


def decode_diff_attention(q, k_new, v_new, cache_k, cache_v, layer, page_table, lamvec, subln_w, lam_init):
    b, n_pages = page_table.shape
    prow = cache_k.shape[2]
    pp = _tile(n_pages, DECODE_PAGES)
    rows = 2 * A_HEADS

    def row(bb, c, pt):
        return (bb, 0, 0)

    def page_idx(p):
        return lambda bb, c, pt: (layer, pt[bb, c * pp + p], 0, 0)

    row_spec = pl.BlockSpec((None, rows, LANES), row)
    page_specs = [pl.BlockSpec((None, None, prow, LANES), page_idx(p)) for p in range(pp)]
    grid_spec = pltpu.PrefetchScalarGridSpec(
        num_scalar_prefetch=1, grid=(b, n_pages // pp),
        in_specs=[pl.BlockSpec((8, LANES), lambda bb, c, pt: (0, 0)), row_spec, row_spec, row_spec,
                  pl.BlockSpec((1, LANES), lambda bb, c, pt: (0, 0))] + page_specs + page_specs,
        out_specs=pl.BlockSpec((None, A_HEADS, LANES), row),
        scratch_shapes=[pltpu.VMEM((rows, 1), F32), pltpu.VMEM((rows, 1), F32),
                        pltpu.VMEM((rows, LANES), F32)])
    r8 = lambda a: jnp.repeat(a.reshape(b, A_HEADS, LANES), 2, axis=1)
    out = pl.pallas_call(
        functools.partial(_decode_kernel, pp=pp, lam_init=lam_init),
        grid_spec=grid_spec,
        out_shape=jax.ShapeDtypeStruct((b, A_HEADS, LANES), F32),
        compiler_params=_cparams(("parallel", "arbitrary")))(
            page_table, lamvec, r8(q), r8(k_new), r8(v_new), subln_w,
            *([cache_k] * pp), *([cache_v] * pp))
    return out.reshape(b, A_HEADS * LANES)


HK = G_HEADS * G_QK
HV = G_HEADS * G_V
GLA_LEVELS = (1, 2, 4, 8, 16, 32)


def _gla_constants():
    c = CHUNK
    t = np.arange(c)[:, None]
    i = np.arange(c)[None, :]
    def prefix(s):
        return (i // s == t // s) & (i <= t)

    def suffix(s):
        return (i // s == t // s) & (i > t)

    mats = [np.where(t % (2 * s) >= s, prefix(s), suffix(s)) for s in GLA_LEVELS]
    mats += [prefix(c), suffix(c)]
    tri = np.concatenate(mats, axis=0).astype(np.float32)
    j = i
    level = np.full((c, c), -1, np.int32)
    level[t == j] = 0
    for n, s in enumerate(GLA_LEVELS):
        sel = (t // (2 * s) == j // (2 * s)) & (t % (2 * s) >= s) & (j % (2 * s) < s)
        level[sel] = n + 1
    level = np.tile(level, (G_HEADS, 1))
    headmask = (np.arange(HK)[None, :] // G_QK == np.arange(G_HEADS * c)[:, None] // c).astype(np.float32)
    return jnp.asarray(tri, BF16), jnp.asarray(level), jnp.asarray(headmask)


def _stack_heads(x, hm):
    return jnp.concatenate([x] * G_HEADS, axis=0) * hm


def _col_bcast(row, width):
    n = row.shape[1]
    eye = lax.broadcasted_iota(jnp.int32, (n, n), 0) == lax.broadcasted_iota(jnp.int32, (n, n), 1)
    ones = jnp.ones((n, width), BF16)
    out = jnp.zeros((n, width), F32)
    for part in _split3(row):
        d = jnp.where(eye, jnp.broadcast_to(part.astype(F32), (n, n)), 0.0)
        out = out + jnp.dot(d.astype(BF16), ones, preferred_element_type=F32)
    return out


def _gla_kernel(q_ref, k_ref, v_ref, rg_ref, lr_ref, wup_ref, bup_ref, nw_ref, tri_ref, lvl_ref, hm_ref,
                o_ref, s_out_ref, s_ref, *, n_chunks):
    c = CHUNK

    @pl.when(pl.program_id(0) == 0)
    def _():
        s_ref[...] = jnp.zeros(s_ref.shape, F32)

    tri = tri_ref[...]
    lvl = lvl_ref[...]
    hm = hm_ref[...]
    nl = len(GLA_LEVELS)

    def chunk(ci, carry):
        r0 = pl.multiple_of(ci * c, c)
        rows = pl.ds(r0, c)
        q = q_ref[rows, :] * (G_QK ** -0.5)
        k = k_ref[rows, :]
        v = v_ref[rows, :].astype(BF16)
        gk = jnp.dot(lr_ref[rows, :].astype(BF16), wup_ref[...], preferred_element_type=F32) + bup_ref[...]
        g = _log_sigmoid(gk) / G_NORMALIZER
        ps = jnp.zeros(((nl + 2) * c, HK), F32)
        for part in _split3(g):
            ps = ps + jnp.dot(tri, part, preferred_element_type=F32)
        gcum, gsuf = ps[nl * c:(nl + 1) * c], ps[(nl + 1) * c:(nl + 2) * c]
        a = jnp.zeros((G_HEADS * c, c), F32)
        for n in range(nl + 1):
            if n == 0:
                qq, kk = q, k
            else:
                f = jnp.exp(ps[(n - 1) * c:n * c])
                qq, kk = q * f, k * f
            d = lax.dot_general(_stack_heads(qq, hm).astype(BF16), kk.astype(BF16),
                                (((1,), (1,)), ((), ())), preferred_element_type=F32)
            a = jnp.where(lvl == n, d, a)
        s_old = s_ref[...]
        inter = jnp.dot(_stack_heads(q * jnp.exp(gcum), hm).astype(BF16), s_old.astype(BF16),
                        preferred_element_type=F32)
        ab = a.astype(BF16)
        nw = nw_ref[...]
        kv = lax.dot_general((k * jnp.exp(gsuf)).astype(BF16), v, (((0,), (0,)), ((), ())),
                             preferred_element_type=F32)
        decay = jnp.exp(_col_bcast(gcum[c - 1:c, :], G_V))
        for hh in range(G_HEADS):
            vs = slice(hh * G_V, (hh + 1) * G_V)
            o = inter[hh * c:(hh + 1) * c] + jnp.dot(ab[hh * c:(hh + 1) * c], v[:, vs],
                                                     preferred_element_type=F32)
            o_ref[rows, vs] = _rms(o, nw) * _silu(rg_ref[rows, vs])
            ks = slice(hh * G_QK, (hh + 1) * G_QK)
            s_ref[ks, :] = decay[ks] * s_old[ks] + kv[ks, vs]
        return carry

    lax.fori_loop(0, n_chunks, chunk, 0, unroll=SCAN_UNROLL)

    @pl.when(pl.program_id(0) == pl.num_programs(0) - 1)
    def _():
        s_out_ref[...] = s_ref[...]


def gla_prompt(h, lrg, w_up, b_up, norm_w):
    t = h.shape[0]
    tb = _tile(t, 512)
    tri, lvl, hm = _gla_constants()
    full = lambda a: pl.BlockSpec(a.shape, lambda i: (0,) * a.ndim)
    o, s = pl.pallas_call(
        functools.partial(_gla_kernel, n_chunks=tb // CHUNK),
        grid=(t // tb,),
        in_specs=[pl.BlockSpec((tb, HK), lambda i: (i, QG // 2)),
                  pl.BlockSpec((tb, HK), lambda i: (i, KG // 2)),
                  pl.BlockSpec((tb, HV), lambda i: (i, VG // 4)),
                  pl.BlockSpec((tb, HV), lambda i: (i, RG // 4)),
                  pl.BlockSpec((tb, LANES), lambda i: (i, 0)),
                  full(w_up), full(b_up), full(norm_w), full(tri), full(lvl), full(hm)],
        out_specs=[pl.BlockSpec((tb, HV), lambda i: (i, 0)),
                   pl.BlockSpec((HK, G_V), lambda i: (0, 0))],
        out_shape=[jax.ShapeDtypeStruct((t, HV), F32), jax.ShapeDtypeStruct((HK, G_V), F32)],
        scratch_shapes=[pltpu.VMEM((HK, G_V), F32)],
        compiler_params=_cparams(("arbitrary",)))(h, h, h, h, lrg, w_up, b_up, norm_w, tri, lvl, hm)
    return o, s.reshape(G_HEADS, G_QK, G_V)


def _ret_kernel(q_ref, k_ref, v_ref, gr_ref, nw_ref, dm_ref, cross_ref, tail_ref, gc_ref, hm_ref,
                o_ref, s_out_ref, s_ref, *, n_chunks):
    c = CHUNK

    @pl.when(pl.program_id(0) == 0)
    def _():
        s_ref[...] = jnp.zeros(s_ref.shape, F32)

    hm = hm_ref[...]
    dm = dm_ref[...]
    cross = cross_ref[...]
    tail = tail_ref[...]
    gc = gc_ref[...]
    nw = nw_ref[...]

    def chunk(ci, carry):
        r0 = pl.multiple_of(ci * c, c)
        rows = pl.ds(r0, c)
        q = q_ref[rows, :]
        k = k_ref[rows, :]
        v = v_ref[rows, :].astype(BF16)
        qs = _stack_heads(q, hm).astype(BF16)
        a = lax.dot_general(qs, k.astype(BF16), (((1,), (1,)), ((), ())), preferred_element_type=F32) * dm
        s_old = s_ref[...]
        inter = jnp.dot(qs, s_old.astype(BF16), preferred_element_type=F32) * cross
        kv = lax.dot_general((k * tail).astype(BF16), v, (((0,), (0,)), ((), ())), preferred_element_type=F32)
        ab = a.astype(BF16)
        for hh in range(R_HEADS):
            vs = slice(hh * G_V, (hh + 1) * G_V)
            o = inter[hh * c:(hh + 1) * c] + jnp.dot(ab[hh * c:(hh + 1) * c], v[:, vs],
                                                     preferred_element_type=F32)
            o_ref[rows, vs] = _rms(o, nw) * _silu(gr_ref[rows, vs])
            ks = slice(hh * R_QK, (hh + 1) * R_QK)
            s_ref[ks, :] = gc[ks] * s_old[ks] + kv[ks, vs]
        return carry

    lax.fori_loop(0, n_chunks, chunk, 0, unroll=SCAN_UNROLL)

    @pl.when(pl.program_id(0) == pl.num_programs(0) - 1)
    def _():
        s_out_ref[...] = s_ref[...]


def _ret_log_gamma():
    return jnp.log(1.0 - jnp.exp2(-5.0 - jnp.arange(R_HEADS, dtype=F32)))


def ret_prompt(h, qr, kr, norm_w):
    t = h.shape[0]
    tb = _tile(t, 512)
    c = CHUNK
    lg = _ret_log_gamma()
    idx = jnp.arange(c, dtype=F32)
    rel = idx[:, None] - idx[None, :]
    dmat = jnp.where(rel[None] >= 0, jnp.exp(jnp.maximum(rel, 0.0)[None] * lg[:, None, None]), 0.0)
    dmat = dmat.reshape(R_HEADS * c, c)
    cross = jnp.exp((idx + 1.0)[None, :] * lg[:, None]).reshape(R_HEADS * c, 1)
    cross = jnp.broadcast_to(cross, (R_HEADS * c, G_V))
    tail = jnp.exp((c - 1.0 - idx)[None, :] * lg[:, None])
    tail = jnp.repeat(tail.T, R_QK, axis=1)
    gc = jnp.broadcast_to(jnp.repeat(jnp.exp(c * lg), R_QK)[:, None], (HK, G_V))
    _, _, hm = _gla_constants()
    full = lambda a: pl.BlockSpec(a.shape, lambda i: (0,) * a.ndim)
    o, s = pl.pallas_call(
        functools.partial(_ret_kernel, n_chunks=tb // c),
        grid=(t // tb,),
        in_specs=[pl.BlockSpec((tb, HK), lambda i: (i, 0)),
                  pl.BlockSpec((tb, HK), lambda i: (i, 0)),
                  pl.BlockSpec((tb, HV), lambda i: (i, VR // 4)),
                  pl.BlockSpec((tb, HV), lambda i: (i, GR // 4)),
                  full(norm_w), full(dmat), full(cross), full(tail), full(gc), full(hm)],
        out_specs=[pl.BlockSpec((tb, HV), lambda i: (i, 0)),
                   pl.BlockSpec((HK, G_V), lambda i: (0, 0))],
        out_shape=[jax.ShapeDtypeStruct((t, HV), F32), jax.ShapeDtypeStruct((HK, G_V), F32)],
        scratch_shapes=[pltpu.VMEM((HK, G_V), F32)],
        compiler_params=_cparams(("arbitrary",)))(qr, kr, h, h, norm_w, dmat, cross, tail, gc, hm)
    return o, s.reshape(R_HEADS, R_QK, G_V)


def _step_kernel(s_ref, q_ref, k_ref, d_ref, v_ref, gate_ref, nw_ref, s_out_ref, o_ref, *, is_gla, q_scale):
    d = d_ref[...]
    if is_gla:
        d = _log_sigmoid(d) / G_NORMALIZER
    s_new = jnp.exp(d) * s_ref[...] + k_ref[...] * v_ref[...]
    s_out_ref[...] = s_new
    o = jnp.sum((q_ref[...] * q_scale) * s_new, axis=1, keepdims=True)
    o_ref[...] = _rms(o, nw_ref[...]) * _silu(gate_ref[...])


def recurrent_step(state, q, k, dlog, v, gate, norm_w, is_gla, q_scale):
    b = state.shape[0]
    col = lambda a: a.reshape(b, G_HEADS, G_QK, 1)
    rowv = lambda a: a.reshape(b, G_HEADS, 1, G_V)
    cspec = pl.BlockSpec((None, G_HEADS, G_QK, 1), lambda i: (i, 0, 0, 0))
    rspec = pl.BlockSpec((None, G_HEADS, 1, G_V), lambda i: (i, 0, 0, 0))
    sspec = pl.BlockSpec((None, G_HEADS, G_QK, G_V), lambda i: (i, 0, 0, 0))
    s_new, o = pl.pallas_call(
        functools.partial(_step_kernel, is_gla=is_gla, q_scale=q_scale),
        grid=(b,),
        in_specs=[sspec, cspec, cspec, cspec, rspec, rspec, pl.BlockSpec((1, G_V), lambda i: (0, 0))],
        out_specs=[sspec, rspec],
        out_shape=[jax.ShapeDtypeStruct(state.shape, F32), jax.ShapeDtypeStruct((b, G_HEADS, 1, G_V), F32)],
        compiler_params=_cparams(("parallel",)))(state, col(q), col(k), col(dlog), rowv(v), rowv(gate), norm_w)
    return o.reshape(b, HV), s_new


def _mixer_out_kernel(oa_ref, og_ref, or_ref, za_ref, zb_ref, zc_ref, x_ref, wpa_ref, wpb_ref, wpc_ref,
                      wo_ref, g_ref, b_ref, y_ref, *, alpha):
    def branch(o_ref, z_ref, w_ref):
        return _sigmoid(z_ref[...]) * jnp.dot(o_ref[...].astype(BF16), w_ref[...], preferred_element_type=F32)

    merged = branch(oa_ref, za_ref, wpa_ref) + branch(og_ref, zb_ref, wpb_ref) + branch(or_ref, zc_ref, wpc_ref)
    y = alpha * x_ref[...] + jnp.dot(merged.astype(BF16), wo_ref[...], preferred_element_type=F32)
    y_ref[...] = _layer_norm(y, g_ref[...], b_ref[...])


def mixer_out(oa, og, orr, h, x, wpa, wpb, wpc, wo, g, b, alpha):
    m = x.shape[0]
    tm = _tile(m, 256)
    bw = oa.shape[1]
    ospec = pl.BlockSpec((tm, bw), lambda i: (i, 0))
    zspec = lambda blk: pl.BlockSpec((tm, D_MODEL), lambda i: (i, blk // 8))
    xspec = pl.BlockSpec((tm, D_MODEL), lambda i: (i, 0))
    full = lambda a: pl.BlockSpec(a.shape, lambda i: (0,) * a.ndim)
    return pl.pallas_call(
        functools.partial(_mixer_out_kernel, alpha=alpha),
        grid=(m // tm,),
        in_specs=[ospec, ospec, ospec, zspec(ZA), zspec(ZB), zspec(ZC), xspec,
                  full(wpa), full(wpb), full(wpc), full(wo), full(g), full(b)],
        out_specs=xspec,
        out_shape=jax.ShapeDtypeStruct((m, D_MODEL), F32),
        compiler_params=_cparams(("parallel",)))(oa, og, orr, h, h, h, x, wpa, wpb, wpc, wo, g, b)


def _ffn_up_kernel(x_ref, wg_ref, wu_ref, h_ref):
    x = x_ref[...].astype(BF16)
    a = jnp.dot(x, wg_ref[...], preferred_element_type=F32)
    u = jnp.dot(x, wu_ref[...], preferred_element_type=F32)
    h_ref[...] = (_silu(a) * u).astype(h_ref.dtype)


def _ffn_down_kernel(h_ref, wd_ref, x_ref, g_ref, b_ref, y_ref, *, alpha):
    y = alpha * x_ref[...] + jnp.dot(h_ref[...], wd_ref[...], preferred_element_type=F32)
    y_ref[...] = _layer_norm(y, g_ref[...], b_ref[...])


def dense_ffn(x, wg, wu, wd, g, b, alpha):
    m = x.shape[0]
    f = wg.shape[1]
    tm, tf = _tile(m, 512), _tile(f, 1408)
    hmid = pl.pallas_call(
        _ffn_up_kernel, grid=(m // tm, f // tf),
        in_specs=[pl.BlockSpec((tm, D_MODEL), lambda i, j: (i, 0)),
                  pl.BlockSpec((D_MODEL, tf), lambda i, j: (0, j)),
                  pl.BlockSpec((D_MODEL, tf), lambda i, j: (0, j))],
        out_specs=pl.BlockSpec((tm, tf), lambda i, j: (i, j)),
        out_shape=jax.ShapeDtypeStruct((m, f), BF16),
        compiler_params=_cparams(("parallel", "parallel")))(x, wg, wu)
    full = lambda a: pl.BlockSpec(a.shape, lambda i: (0,) * a.ndim)
    xspec = pl.BlockSpec((tm, D_MODEL), lambda i: (i, 0))
    return pl.pallas_call(
        functools.partial(_ffn_down_kernel, alpha=alpha), grid=(m // tm,),
        in_specs=[pl.BlockSpec((tm, f), lambda i: (i, 0)), full(wd), xspec, full(g), full(b)],
        out_specs=xspec,
        out_shape=jax.ShapeDtypeStruct((m, D_MODEL), F32),
        compiler_params=_cparams(("parallel",)))(hmid, wd, x, g, b)


def _top2_weights(x, wh, wl):
    xh = x.astype(BF16)
    xl = (x - xh.astype(F32)).astype(BF16)
    logits = (jnp.dot(xh, wh, preferred_element_type=F32) + jnp.dot(xh, wl, preferred_element_type=F32)
              + jnp.dot(xl, wh, preferred_element_type=F32))
    lane = lax.broadcasted_iota(jnp.int32, logits.shape, 1)
    logits = jnp.where(lane < N_EXPERTS, logits, NEG_BIG)
    m1 = jnp.max(logits, axis=-1, keepdims=True)
    i1 = jnp.min(jnp.where(logits == m1, lane, LANES), axis=-1, keepdims=True)
    rest = jnp.where(lane == i1, NEG_BIG, logits)
    m2 = jnp.max(rest, axis=-1, keepdims=True)
    i2 = jnp.min(jnp.where(rest == m2, lane, LANES), axis=-1, keepdims=True)
    e2 = jnp.exp(m2 - m1)
    w1 = 1.0 / (1.0 + e2)
    w2 = e2 / (1.0 + e2)
    return jnp.where(lane == i1, w1, jnp.where(lane == i2, w2, 0.0))


def _router_kernel(x_ref, wh_ref, wl_ref, comb_ref):
    comb_ref[...] = _top2_weights(x_ref[...], wh_ref[...], wl_ref[...])


def _router_rank_kernel(x_ref, wh_ref, wl_ref, comb_ref, rankc_ref, rankr_ref, cnt_ref):
    comb = _top2_weights(x_ref[...], wh_ref[...], wl_ref[...])
    tb = comb.shape[0]
    routed = comb > 0.0
    ones = jnp.where(routed, 1.0, 0.0)
    earlier = (lax.broadcasted_iota(jnp.int32, (tb, tb), 0) > lax.broadcasted_iota(jnp.int32, (tb, tb), 1))
    rank = jnp.dot(jnp.where(earlier, 1.0, 0.0).astype(BF16), ones.astype(BF16), preferred_element_type=F32)
    rankc = jnp.where(routed, rank, -1.0)
    comb_ref[...] = comb
    rankc_ref[...] = rankc
    cnt_ref[...] = jnp.broadcast_to(jnp.sum(ones, axis=0, keepdims=True), cnt_ref.shape)
    eye = (lax.broadcasted_iota(jnp.int32, (LANES, LANES), 0) == lax.broadcasted_iota(jnp.int32, (LANES, LANES), 1))
    eye = jnp.where(eye, 1.0, 0.0).astype(BF16)
    rt = jnp.zeros((LANES, tb), F32)
    for part in _split3(rankc):
        rt = rt + lax.dot_general(eye, part, (((1,), (1,)), ((), ())), preferred_element_type=F32)
    rankr_ref[...] = rt[:N_EXPERTS]


def _router_weights(w_router):
    wpad = jnp.zeros((D_MODEL, LANES), F32).at[:, :N_EXPERTS].set(w_router)
    wh = wpad.astype(BF16)
    return wh, (wpad - wh.astype(F32)).astype(BF16)


def router(x, w_router):
    m = x.shape[0]
    tm = _tile(m, 512)
    wh, wl = _router_weights(w_router)
    full = lambda a: pl.BlockSpec(a.shape, lambda i: (0,) * a.ndim)
    return pl.pallas_call(
        _router_kernel, grid=(m // tm,),
        in_specs=[pl.BlockSpec((tm, D_MODEL), lambda i: (i, 0)), full(wh), full(wl)],
        out_specs=pl.BlockSpec((tm, LANES), lambda i: (i, 0)),
        out_shape=jax.ShapeDtypeStruct((m, LANES), F32),
        compiler_params=_cparams(("parallel",)))(x, wh, wl)


def router_ranked(x, w_router, tb):
    m = x.shape[0]
    nb = m // tb
    wh, wl = _router_weights(w_router)
    full = lambda a: pl.BlockSpec(a.shape, lambda i: (0,) * a.ndim)
    tok = pl.BlockSpec((tb, LANES), lambda i: (i, 0))
    return pl.pallas_call(
        _router_rank_kernel, grid=(nb,),
        in_specs=[pl.BlockSpec((tb, D_MODEL), lambda i: (i, 0)), full(wh), full(wl)],
        out_specs=[tok, tok, pl.BlockSpec((None, N_EXPERTS, tb), lambda i: (i, 0, 0)),
                   pl.BlockSpec((None, N_EXPERTS, LANES), lambda i: (i, 0, 0))],
        out_shape=[jax.ShapeDtypeStruct((m, LANES), F32), jax.ShapeDtypeStruct((m, LANES), F32),
                   jax.ShapeDtypeStruct((nb, N_EXPERTS, tb), F32),
                   jax.ShapeDtypeStruct((nb, N_EXPERTS, LANES), F32)],
        compiler_params=_cparams(("parallel",)))(x, wh, wl)


def _moe_kernel(x_ref, comb_ref, wg_ref, wu_ref, wd_ref, g_ref, b_ref, y_ref, acc_ref, *, alpha):
    e, f = pl.program_id(1), pl.program_id(2)

    @pl.when((e == 0) & (f == 0))
    def _():
        acc_ref[...] = jnp.zeros(acc_ref.shape, F32)

    x = x_ref[...].astype(BF16)
    a = jnp.dot(x, wg_ref[...], preferred_element_type=F32)
    u = jnp.dot(x, wu_ref[...], preferred_element_type=F32)
    hmid = (_silu(a) * u).astype(BF16)
    comb = comb_ref[...]
    lane = lax.broadcasted_iota(jnp.int32, comb.shape, 1)
    ce = jnp.sum(jnp.where(lane == e, comb, 0.0), axis=-1, keepdims=True)
    acc_ref[...] += ce * jnp.dot(hmid, wd_ref[...], preferred_element_type=F32)

    @pl.when((e == pl.num_programs(1) - 1) & (f == pl.num_programs(2) - 1))
    def _():
        y_ref[...] = _layer_norm(alpha * x_ref[...] + acc_ref[...], g_ref[...], b_ref[...])


def moe_ffn(x, comb, wg, wu, wd, g, b, alpha):
    m = x.shape[0]
    ne, _, f = wg.shape
    tm, tf = _tile(m, 1024), _tile(f, 512)
    xspec = pl.BlockSpec((tm, D_MODEL), lambda i, e, j: (i, 0))
    full = lambda a: pl.BlockSpec(a.shape, lambda i, e, j: (0,) * a.ndim)
    return pl.pallas_call(
        functools.partial(_moe_kernel, alpha=alpha), grid=(m // tm, ne, f // tf),
        in_specs=[xspec, pl.BlockSpec((tm, LANES), lambda i, e, j: (i, 0)),
                  pl.BlockSpec((None, D_MODEL, tf), lambda i, e, j: (e, 0, j)),
                  pl.BlockSpec((None, D_MODEL, tf), lambda i, e, j: (e, 0, j)),
                  pl.BlockSpec((None, tf, D_MODEL), lambda i, e, j: (e, j, 0)),
                  full(g), full(b)],
        out_specs=xspec,
        out_shape=jax.ShapeDtypeStruct((m, D_MODEL), F32),
        scratch_shapes=[pltpu.VMEM((tm, D_MODEL), F32)],
        compiler_params=_cparams(("parallel", "arbitrary", "arbitrary")))(x, comb, wg, wu, wd, g, b)


def _moe_routed_kernel(cnt_ref, x_ref, comb_ref, rankc_ref, rankr_ref, wg_ref, wu_ref, wd_ref, g_ref, b_ref,
                       y_ref, acc_ref, xb_ref, xc_ref, yacc_ref, *, alpha, rows, cap):
    blk, e, f = pl.program_id(0), pl.program_id(1), pl.program_id(2)
    last_f = pl.num_programs(2) - 1
    n = cnt_ref[blk * N_EXPERTS + e]
    tb = x_ref.shape[0]

    @pl.when((e == 0) & (f == 0))
    def _():
        acc_ref[...] = jnp.zeros(acc_ref.shape, F32)
        xb_ref[...] = x_ref[...].astype(BF16)

    def column(ref):
        a = ref[...]
        lane = lax.broadcasted_iota(jnp.int32, a.shape, 1)
        return jnp.sum(jnp.where(lane == e, a, 0.0), axis=-1, keepdims=True)

    def expert(xs):
        a = jnp.dot(xs, wg_ref[...], preferred_element_type=F32)
        u = jnp.dot(xs, wu_ref[...], preferred_element_type=F32)
        return jnp.dot((_silu(a) * u).astype(BF16), wd_ref[...], preferred_element_type=F32)

    @pl.when((n > 0) & (n <= rows))
    def _():
        @pl.when(f == 0)
        def _():
            rr = rankr_ref[...]
            sub = lax.broadcasted_iota(jnp.int32, rr.shape, 0)
            rrow = jnp.sum(jnp.where(sub == e, rr, 0.0), axis=0, keepdims=True)
            slot = lax.broadcasted_iota(jnp.int32, (rows, tb), 0).astype(F32)
            gather = jnp.where(slot == rrow, 1.0, 0.0).astype(BF16)
            xc_ref[...] = jnp.dot(gather, xb_ref[...], preferred_element_type=F32).astype(BF16)
            yacc_ref[0:cap, :] = jnp.zeros((cap, D_MODEL), F32)

        yacc_ref[0:rows, :] += expert(xc_ref[...])

        @pl.when(f == last_f)
        def _():
            slot = lax.broadcasted_iota(jnp.int32, (tb, cap), 1).astype(F32)
            scatter = jnp.where(slot == column(rankc_ref), 1.0, 0.0).astype(BF16)
            acc_ref[...] += column(comb_ref) * jnp.dot(scatter, yacc_ref[0:cap, :].astype(BF16),
                                                       preferred_element_type=F32)

    @pl.when(n > rows)
    def _():
        @pl.when(f == 0)
        def _():
            yacc_ref[...] = jnp.zeros(yacc_ref.shape, F32)

        yacc_ref[...] += expert(xb_ref[...])

        @pl.when(f == last_f)
        def _():
            acc_ref[...] += column(comb_ref) * yacc_ref[...]

    @pl.when((e == pl.num_programs(1) - 1) & (f == last_f))
    def _():
        y_ref[...] = _layer_norm(alpha * x_ref[...] + acc_ref[...], g_ref[...], b_ref[...])


def moe_routed_ffn(x, w_router, wg, wu, wd, g, b, alpha):
    m = x.shape[0]
    ne, _, f = wg.shape
    tb, tf = _tile(m, MOE_BLOCK), _tile(f, MOE_TF)
    rows = min(tb, MOE_ROWS)
    cap = min(tb, -(-rows // LANES) * LANES)
    comb, rankc, rankr, cnt = router_ranked(x, w_router, tb)
    counts = cnt[:, 0, :N_EXPERTS].astype(jnp.int32).reshape(-1)
    xspec = pl.BlockSpec((tb, D_MODEL), lambda i, e, j, c: (i, 0))
    tok = pl.BlockSpec((tb, LANES), lambda i, e, j, c: (i, 0))
    full = lambda a: pl.BlockSpec(a.shape, lambda i, e, j, c: (0,) * a.ndim)
    grid_spec = pltpu.PrefetchScalarGridSpec(
        num_scalar_prefetch=1, grid=(m // tb, ne, f // tf),
        in_specs=[xspec, tok, tok, pl.BlockSpec((None, N_EXPERTS, tb), lambda i, e, j, c: (i, 0, 0)),
                  pl.BlockSpec((None, D_MODEL, tf), lambda i, e, j, c: (e, 0, j)),
                  pl.BlockSpec((None, D_MODEL, tf), lambda i, e, j, c: (e, 0, j)),
                  pl.BlockSpec((None, tf, D_MODEL), lambda i, e, j, c: (e, j, 0)),
                  full(g), full(b)],
        out_specs=xspec,
        scratch_shapes=[pltpu.VMEM((tb, D_MODEL), F32), pltpu.VMEM((tb, D_MODEL), BF16),
                        pltpu.VMEM((rows, D_MODEL), BF16), pltpu.VMEM((tb, D_MODEL), F32)])
    return pl.pallas_call(
        functools.partial(_moe_routed_kernel, alpha=alpha, rows=rows, cap=cap),
        grid_spec=grid_spec,
        out_shape=jax.ShapeDtypeStruct((m, D_MODEL), F32),
        compiler_params=_cparams(("parallel", "arbitrary", "arbitrary")))(
            counts, x, comb, rankc, rankr, wg, wu, wd, g, b)


def _layer_weights(l, w_in, lam_q1, lam_k1, lam_q2, lam_k2, subln_w, w_gla_up, b_gla_up, gla_norm_w,
                   ret_norm_w, w_pa, w_pb, w_pc, w_out, ln1_g, ln1_b, ln2_g, ln2_b):
    wl = w_in[l]
    w_main = jnp.concatenate([wl[:, a:b] for a, b in PACK_ORDER], axis=1).astype(BF16)
    w_lrg = jnp.zeros((D_MODEL, LANES), F32).at[:, :G_RANK].set(wl[:, LRG_OFF:LRG_OFF + G_RANK]).astype(BF16)
    w_up = jnp.zeros((LANES, HK), F32).at[:G_RANK].set(w_gla_up[l]).astype(BF16)
    lamvec = jnp.zeros((8, LANES), F32)
    for r, vec in enumerate((lam_q1, lam_k1, lam_q2, lam_k2)):
        lamvec = lamvec.at[r, :A_QK].set(vec[l].astype(F32))
    row = lambda a: a[l].reshape(1, -1)
    return dict(w_main=w_main, w_lrg=w_lrg, w_up=w_up, b_up=row(b_gla_up), lamvec=lamvec,
                subln=row(subln_w), gla_nw=row(gla_norm_w), ret_nw=row(ret_norm_w),
                wpa=w_pa[l].astype(BF16), wpb=w_pb[l].astype(BF16), wpc=w_pc[l].astype(BF16),
                wo=w_out[l].astype(BF16), ln1_g=row(ln1_g), ln1_b=row(ln1_b),
                ln2_g=row(ln2_g), ln2_b=row(ln2_b), lam_init=0.8 - 0.6 * math.exp(-0.3 * l))


def _project(x, pos, lw, q_dtype, q_scale):
    h = matmul(x, lw['w_main'], tn_cap=INPROJ_TN)
    lrg = matmul(x, lw['w_lrg'], tn_cap=LANES)
    qa, ka = rotary(h, QA, A_HEADS * LANES, _rope_tables(pos), ROPE_DIM // 2, q_scale, 1.0, q_dtype)
    qr, kr = rotary(h, QR, HK, _ret_tables(pos), 1, 1.0, R_QK ** -0.5, F32)
    return h, lrg, qa, ka, qr, kr


def _channel_mix(x1, l, lw, ffn_w, alpha):
    if l % 2 == 0:
        wg, wu, wd = ffn_w['dense'][l // 2]
        return dense_ffn(x1, wg, wu, wd, lw['ln2_g'], lw['ln2_b'], alpha)
    w_r, wg, wu, wd = ffn_w['moe'][l // 2]
    if x1.shape[0] >= MOE_MIN_ROUTED:
        return moe_routed_ffn(x1, w_r, wg, wu, wd, lw['ln2_g'], lw['ln2_b'], alpha)
    comb = router(x1, w_r)
    return moe_ffn(x1, comb, wg, wu, wd, lw['ln2_g'], lw['ln2_b'], alpha)


def kernel(x_prompt, x_sample, cache_k, cache_v, state_gla, state_ret, page_table, w_in, lam_q1, lam_k1, lam_q2, lam_k2, subln_w, w_gla_up, b_gla_up, gla_norm_w, ret_norm_w, w_pa, w_pb, w_pc, w_out, ln1_g, ln1_b, w_ff_gate, w_ff_up, w_ff_down, w_router, w_exp_gate, w_exp_up, w_exp_down, ln2_g, ln2_b):
    bp, tp, _ = x_prompt.shape
    bs, ts, _ = x_sample.shape
    assert bp == 1 and ts == 1
    depth = w_in.shape[0]
    alpha = (2 * depth) ** 0.25
    n_pool, page = cache_k.shape[1], cache_k.shape[2]
    past_len = page_table.shape[1] * page
    pos_p = jnp.arange(tp, dtype=jnp.int32)
    pos_s = jnp.full((bs,), past_len, jnp.int32)
    ffn_w = dict(
        dense=[(w_ff_gate[i].astype(BF16), w_ff_up[i].astype(BF16), w_ff_down[i].astype(BF16))
               for i in range(w_ff_gate.shape[0])],
        moe=[(w_router[i], w_exp_gate[i].astype(BF16), w_exp_up[i].astype(BF16), w_exp_down[i].astype(BF16))
             for i in range(w_router.shape[0])])
    ck = cache_k.reshape(depth, n_pool, page * A_HEADS, LANES)
    cv = cache_v.reshape(depth, n_pool, page * A_HEADS, LANES)
    lg_col = jnp.broadcast_to(jnp.repeat(_ret_log_gamma(), R_QK)[None, :], (bs, HK))

    yp = x_prompt.reshape(tp, D_MODEL)
    ys = x_sample.reshape(bs, D_MODEL)
    outs = {n: [] for n in ('kp', 'vp', 'gp', 'rp', 'ks', 'vs', 'gs', 'rs')}
    cols = lambda a, blk, n: a[:, blk * LANES:(blk + n) * LANES]
    for l in range(depth):
        lw = _layer_weights(l, w_in, lam_q1, lam_k1, lam_q2, lam_k2, subln_w, w_gla_up, b_gla_up,
                            gla_norm_w, ret_norm_w, w_pa, w_pb, w_pc, w_out, ln1_g, ln1_b, ln2_g, ln2_b)
        h, lrg, qa, ka, qr, kr = _project(yp, pos_p, lw, BF16, A_QK ** -0.5 * LOG2E)
        oa = flash_diff_attention(qa, ka, h, lw['lamvec'], lw['subln'], lw['lam_init'])
        og, s_gla = gla_prompt(h, lrg, lw['w_up'], lw['b_up'], lw['gla_nw'])
        orr, s_ret = ret_prompt(h, qr, kr, lw['ret_nw'])
        x1 = mixer_out(oa, og, orr, h, yp, lw['wpa'], lw['wpb'], lw['wpc'], lw['wo'],
                       lw['ln1_g'], lw['ln1_b'], alpha)
        yp = _channel_mix(x1, l, lw, ffn_w, alpha)
        outs['kp'].append(ka.reshape(1, tp, A_HEADS, 2 * A_QK))
        outs['vp'].append(cols(h, VA, 4).reshape(1, tp, A_HEADS, 2 * A_QK))
        outs['gp'].append(s_gla[None])
        outs['rp'].append(s_ret[None])
        h, lrg, qa, ka, qr, kr = _project(ys, pos_s, lw, F32, A_QK ** -0.5)
        va = cols(h, VA, 4)
        oa = decode_diff_attention(qa, ka, va, ck, cv, l, page_table, lw['lamvec'], lw['subln'],
                                   lw['lam_init'])
        gk = matmul(lrg, lw['w_up'], bias=lw['b_up'])
        og, s_gla = recurrent_step(state_gla[l], cols(h, QG, 2), cols(h, KG, 2), gk, cols(h, VG, 4),
                                   cols(h, RG, 4), lw['gla_nw'], True, G_QK ** -0.5)
        orr, s_ret = recurrent_step(state_ret[l], qr, kr, lg_col, cols(h, VR, 4), cols(h, GR, 4),
                                    lw['ret_nw'], False, 1.0)
        x1 = mixer_out(oa, og, orr, h, ys, lw['wpa'], lw['wpb'], lw['wpc'], lw['wo'],
                       lw['ln1_g'], lw['ln1_b'], alpha)
        ys = _channel_mix(x1, l, lw, ffn_w, alpha)
        outs['ks'].append(ka.reshape(bs, 1, A_HEADS, 2 * A_QK))
        outs['vs'].append(va.reshape(bs, 1, A_HEADS, 2 * A_QK))
        outs['gs'].append(s_gla)
        outs['rs'].append(s_ret)

    st = lambda n: jnp.stack(outs[n])
    return (yp.reshape(bp, tp, D_MODEL), ys.reshape(bs, ts, D_MODEL), st('kp'), st('vp'), st('gp'),
            st('rp'), st('ks'), st('vs'), st('gs'), st('rs'))
```

```python
import functools
import math

import numpy as np
import jax
import jax.numpy as jnp
from jax import lax
from jax.experimental import pallas as pl
from jax.experimental.pallas import tpu as pltpu

F32 = jnp.float32
BF16 = jnp.bfloat16

D_MODEL = 1024
A_HEADS = 4
A_QK = 64
ROPE_DIM = 16
ROPE_THETA = 500000.0
G_HEADS = 4
G_QK = 64
G_V = 128
G_RANK = 16
G_NORMALIZER = 16.0
R_HEADS = 4
R_QK = 64
R_ANGLE_BASE = 10000.0
CHUNK = 64
N_EXPERTS = 8
NORM_EPS = 1e-5
NEG_BIG = -1e30
LOG2E = math.log2(math.e)
FLASH_T = 1024
FLASH_RC = 32
FLASH_KCOLS = 256
FLASH_PVROWS = 128
DECODE_PAGES = 16
SCAN_UNROLL = 4
MOE_BLOCK = 1024
MOE_ROWS = 320
MOE_MIN_ROUTED = 256
MOE_TF = 896
INPROJ_TN = 1536

LANES = 128
VMEM_LIMIT = 56 * 1024 * 1024

ZA, ZB, ZC, QA, KA, VA, VG, RG, VR, GR, QG, KG, QR, KR = 0, 8, 16, 24, 28, 32, 36, 40, 44, 48, 52, 54, 56, 58
PACK_ORDER = ((4624, 7696), (0, 1536), (2048, 2560), (2576, 3088), (3600, 4112), (4112, 4624),
              (1536, 2048), (3088, 3600))
LRG_OFF = 2560


def _cparams(sem):
    return pltpu.CompilerParams(dimension_semantics=sem, vmem_limit_bytes=VMEM_LIMIT)


def _tile(n, cap):
    c = min(n, cap)
    while n % c:
        c -= 1
    return c


def _layer_norm(y, g, b):
    mu = jnp.mean(y, axis=-1, keepdims=True)
    d = y - mu
    var = jnp.mean(d * d, axis=-1, keepdims=True)
    return d * lax.rsqrt(var + NORM_EPS) * g + b


def _rms(o, w):
    return o * lax.rsqrt(jnp.mean(o * o, axis=-1, keepdims=True) + NORM_EPS) * w


def _silu(x):
    return x * (1.0 / (1.0 + jnp.exp(-x)))


def _sigmoid(x):
    return 1.0 / (1.0 + jnp.exp(-x))


def _log_sigmoid(x):
    return jnp.minimum(x, 0.0) - jnp.log(1.0 + jnp.exp(-jnp.abs(x)))


def _split3(x):
    a = x.astype(BF16)
    r = x - a.astype(F32)
    b = r.astype(BF16)
    c = (r - b.astype(F32)).astype(BF16)
    return a, b, c


def _mm_kernel(x_ref, w_ref, o_ref):
    o_ref[...] = jnp.dot(x_ref[...].astype(BF16), w_ref[...],
                         preferred_element_type=F32).astype(o_ref.dtype)


def _mm_bias_kernel(x_ref, w_ref, b_ref, o_ref):
    o_ref[...] = (jnp.dot(x_ref[...].astype(BF16), w_ref[...],
                          preferred_element_type=F32) + b_ref[...]).astype(o_ref.dtype)


def matmul(x, w, bias=None, out_dtype=F32, tm_cap=1024, tn_cap=768):
    m, k = x.shape
    n = w.shape[1]
    tm, tn = _tile(m, tm_cap), _tile(n, tn_cap)
    in_specs = [pl.BlockSpec((tm, k), lambda i, j: (i, 0)),
                pl.BlockSpec((k, tn), lambda i, j: (0, j))]
    args = [x, w]
    kern = _mm_kernel
    if bias is not None:
        in_specs.append(pl.BlockSpec((1, tn), lambda i, j: (0, j)))
        args.append(bias)
        kern = _mm_bias_kernel
    return pl.pallas_call(
        kern, grid=(m // tm, n // tn), in_specs=in_specs,
        out_specs=pl.BlockSpec((tm, tn), lambda i, j: (i, j)),
        out_shape=jax.ShapeDtypeStruct((m, n), out_dtype),
        compiler_params=_cparams(("parallel", "parallel")))(*args)


def _rot_kernel(x_ref, c_ref, s1_ref, s2_ref, oq_ref, ok_ref, *, shift, q_scale, k_scale):
    c, s1, s2 = c_ref[...], s1_ref[...], s2_ref[...]
    nq = oq_ref.shape[1] // LANES
    nk = ok_ref.shape[1] // LANES
    for b in range(nq + nk):
        x = x_ref[:, b * LANES:(b + 1) * LANES]
        y = x * c + pltpu.roll(x, LANES - shift, 1) * s1 + pltpu.roll(x, shift, 1) * s2
        if b < nq:
            oq_ref[:, b * LANES:(b + 1) * LANES] = (y * q_scale).astype(oq_ref.dtype)
        else:
            ok_ref[:, (b - nq) * LANES:(b - nq + 1) * LANES] = (y * k_scale).astype(ok_ref.dtype)


def rotary(h, col_blk, width, tables, shift, q_scale, k_scale, q_dtype):
    m = h.shape[0]
    tm = _tile(m, 512)
    blk = col_blk * LANES // (2 * width)
    tspec = pl.BlockSpec((tm, LANES), lambda i: (i, 0))
    return pl.pallas_call(
        functools.partial(_rot_kernel, shift=shift, q_scale=q_scale, k_scale=k_scale),
        grid=(m // tm,),
        in_specs=[pl.BlockSpec((tm, 2 * width), lambda i: (i, blk)), tspec, tspec, tspec],
        out_specs=[pl.BlockSpec((tm, width), lambda i: (i, 0)),
                   pl.BlockSpec((tm, width), lambda i: (i, 0))],
        out_shape=[jax.ShapeDtypeStruct((m, width), q_dtype),
                   jax.ShapeDtypeStruct((m, width), F32)],
        compiler_params=_cparams(("parallel",)))(h, *tables)


def _rot_att_kernel(x_ref, v_ref, c_ref, s1_ref, s2_ref, *rest, shift, q_scale):
    oq_ref, ok_ref, ov_ref, okr_ref, ovr_ref = rest[-5:]
    c, s1, s2 = c_ref[...], s1_ref[...], s2_ref[...]
    tm = x_ref.shape[0]
    for b in range(2 * A_HEADS):
        x = x_ref[:, b * LANES:(b + 1) * LANES]
        y = x * c + pltpu.roll(x, LANES - shift, 1) * s1 + pltpu.roll(x, shift, 1) * s2
        if b < A_HEADS:
            oq_ref[:, b * LANES:(b + 1) * LANES] = (y * q_scale).astype(oq_ref.dtype)
        else:
            hh = b - A_HEADS
            ok_ref[:, hh * LANES:(hh + 1) * LANES] = y.astype(ok_ref.dtype)
            okr_ref[pl.ds(hh, tm, stride=A_HEADS), :] = y
    for hh in range(A_HEADS):
        v = v_ref[:, hh * LANES:(hh + 1) * LANES]
        ov_ref[:, hh * LANES:(hh + 1) * LANES] = v.astype(ov_ref.dtype)
        ovr_ref[pl.ds(hh, tm, stride=A_HEADS), :] = v


def rotary_attention(h, tables, shift, q_scale, layer, depth, row_bufs=None):
    m = h.shape[0]
    tm = _tile(m, 512)
    nblk = m // tm
    width = A_HEADS * LANES
    tspec = pl.BlockSpec((tm, LANES), lambda i: (i, 0))
    wide = pl.BlockSpec((tm, width), lambda i: (i, 0))
    rowsp = pl.BlockSpec((tm * A_HEADS, LANES), lambda i: (layer * nblk + i, 0))
    in_specs = [pl.BlockSpec((tm, 2 * width), lambda i: (i, QA // 8)),
                pl.BlockSpec((tm, width), lambda i: (i, VA // 4)), tspec, tspec, tspec]
    args = [h, h, *tables]
    aliases = {}
    if row_bufs is not None:
        in_specs += [pl.BlockSpec(memory_space=pl.ANY)] * 2
        aliases = {len(args): 3, len(args) + 1: 4}
        args += list(row_bufs)
    return pl.pallas_call(
        functools.partial(_rot_att_kernel, shift=shift, q_scale=q_scale),
        grid=(nblk,),
        in_specs=in_specs,
        out_specs=[wide, wide, wide, rowsp, rowsp],
        out_shape=[jax.ShapeDtypeStruct((m, width), BF16)] * 3
                  + [jax.ShapeDtypeStruct((depth * m * A_HEADS, LANES), F32)] * 2,
        input_output_aliases=aliases,
        compiler_params=_cparams(("parallel",)))(*args)


def _rope_tables(pos):
    half = ROPE_DIM // 2
    inv = ROPE_THETA ** (-jnp.arange(half, dtype=F32) * 2.0 / ROPE_DIM)
    ang = pos.astype(F32)[:, None] * inv[None, :]
    cos, sin = jnp.cos(ang), jnp.sin(ang)
    lane = np.arange(LANES) % A_QK
    fi = lane % half
    cos_l, sin_l = cos[:, fi], sin[:, fi]
    in_rot = (lane < ROPE_DIM)[None, :]
    lo = (lane < half)[None, :]
    c = jnp.where(in_rot, cos_l, 1.0)
    s1 = jnp.where(lo, -sin_l, 0.0)
    s2 = jnp.where(in_rot & ~lo, sin_l, 0.0)
    return c, s1, s2


def _ret_tables(pos):
    inv = 1.0 / (R_ANGLE_BASE ** jnp.linspace(0.0, 1.0, R_QK // 2, dtype=F32))
    ang = pos.astype(F32)[:, None] * inv[None, :]
    cos, sin = jnp.cos(ang), jnp.sin(ang)
    lane = np.arange(LANES) % R_QK
    cos_l, sin_l = cos[:, lane // 2], sin[:, lane // 2]
    even = (lane % 2 == 0)[None, :]
    return cos_l, jnp.where(even, -sin_l, 0.0), jnp.where(even, 0.0, sin_l)


def _lam_from(lam_ref, lam_init):
    v = lam_ref[...]
    t1 = jnp.sum(v[0:1] * v[1:2], axis=-1, keepdims=True)
    t2 = jnp.sum(v[2:3] * v[3:4], axis=-1, keepdims=True)
    return jnp.exp(t1) - jnp.exp(t2) + lam_init


def _flash_kernel(qi_ref, kj_ref, lam_ref, q_ref, k_ref, v_ref, w_ref, o_ref, m_ref, acc_ref, sa_ref, sb_ref,
                  p_ref, *, t, rc, lam_init):
    step = pl.program_id(1)
    i, j = qi_ref[step], kj_ref[step]
    nch = t // LANES

    kw = min(t, FLASH_KCOLS)
    pvr = min(t, FLASH_PVROWS)

    def produce_items(s_dst):
        items = []
        for c in range(2):
            for kc in range(t // kw):
                def item(c=c, kc=kc):
                    q = q_ref[...]
                    lane = lax.broadcasted_iota(jnp.int32, q.shape, 1)
                    qc = jnp.where((lane < A_QK) if c == 0 else (lane >= A_QK), q, jnp.zeros_like(q))
                    kk = k_ref[kc * kw:(kc + 1) * kw, :].astype(BF16)
                    s_dst[c, :, kc * kw:(kc + 1) * kw] = lax.dot_general(
                        qc, kk, (((1,), (1,)), ((), ())), preferred_element_type=F32)
                items.append(item)
        return items

    def consume_items(s_src, masked):
        items = []
        for g in range(t // pvr):
            def sweep(g=g):
                for r in range(g * pvr // rc, (g + 1) * pvr // rc):
                    rows = slice(r * rc, (r + 1) * rc)
                    if masked:
                        keep = (lax.broadcasted_iota(jnp.int32, (rc, t), 1)
                                <= r * rc + lax.broadcasted_iota(jnp.int32, (rc, t), 0))
                    for c in range(2):
                        s = s_src[c, rows, :]
                        if masked:
                            s = jnp.where(keep, s, NEG_BIG)
                        m_prev = m_ref[c, rows, :]
                        m_new = jnp.maximum(m_prev, jnp.max(s, axis=-1, keepdims=True))
                        p = jnp.exp2(s - jnp.concatenate([m_new] * nch, axis=1))
                        alpha = jnp.exp2(m_prev - m_new)
                        acc_ref[c, rows, :] = jnp.concatenate([alpha, alpha], axis=1) * acc_ref[c, rows, :]
                        m_ref[c, rows, :] = m_new
                        p_ref[c, rows, :] = p.astype(BF16)

            def pv(g=g):
                v = jnp.concatenate([v_ref[...].astype(BF16), jnp.ones((t, LANES), BF16)], axis=1)
                rows = slice(g * pvr, (g + 1) * pvr)
                for c in range(2):
                    acc_ref[c, rows, :] += jnp.dot(p_ref[c, rows, :], v, preferred_element_type=F32)
            items += [sweep, pv]
        return items

    def run(*item_lists):
        longest = max(len(l) for l in item_lists)
        for n in range(longest):
            for l in item_lists:
                lo, hi = n * len(l) // longest, (n + 1) * len(l) // longest
                for it in l[lo:hi]:
                    it()

    def produce(s_dst):
        run(produce_items(s_dst))

    def consume(s_src, masked):
        run(consume_items(s_src, masked))

    even = j % 2 == 0

    @pl.when(j == 0)
    def _():
        m_ref[...] = jnp.full(m_ref.shape, NEG_BIG, F32)
        acc_ref[...] = jnp.zeros(acc_ref.shape, F32)
        produce(sa_ref)

    steady = (j >= 1) & (j <= i)

    @pl.when(steady & even)
    def _():
        run(consume_items(sb_ref, False), produce_items(sa_ref))

    @pl.when(steady & jnp.logical_not(even))
    def _():
        run(consume_items(sa_ref, False), produce_items(sb_ref))

    drain = j == i + 1
    pl.when(drain & even)(functools.partial(consume, sb_ref, True))
    pl.when(drain & jnp.logical_not(even))(functools.partial(consume, sa_ref, True))

    @pl.when(drain)
    def _():
        lam = _lam_from(lam_ref, lam_init)
        o = (acc_ref[0, :, :LANES] / acc_ref[0, :, LANES:]
             - lam * (acc_ref[1, :, :LANES] / acc_ref[1, :, LANES:]))
        o_ref[...] = _rms(o, w_ref[...]) * (1.0 - lam_init)


def flash_diff_attention(q, k, v, lamvec, subln_w, lam_init, t_cap=FLASH_T, rc=FLASH_RC):
    n = q.shape[0]
    t = _tile(n, t_cap)
    rc = _tile(t, rc)
    nb = n // t
    pairs = [(i, j) for i in range(nb) for j in range(i + 2)]
    qi = jnp.asarray([p[0] for p in pairs], jnp.int32)
    kj = jnp.asarray([p[1] for p in pairs], jnp.int32)
    grid_spec = pltpu.PrefetchScalarGridSpec(
        num_scalar_prefetch=2, grid=(A_HEADS, len(pairs)),
        in_specs=[pl.BlockSpec((8, LANES), lambda hh, s, qi, kj: (0, 0)),
                  pl.BlockSpec((t, LANES), lambda hh, s, qi, kj: (qi[s], hh)),
                  pl.BlockSpec((t, LANES), lambda hh, s, qi, kj: (jnp.minimum(kj[s], qi[s]), hh)),
                  pl.BlockSpec((t, LANES), lambda hh, s, qi, kj: (jnp.clip(kj[s] - 1, 0, qi[s]), hh)),
                  pl.BlockSpec((1, LANES), lambda hh, s, qi, kj: (0, 0))],
        out_specs=pl.BlockSpec((t, LANES), lambda hh, s, qi, kj: (qi[s], hh)),
        scratch_shapes=[pltpu.VMEM((2, t, LANES), F32), pltpu.VMEM((2, t, 2 * LANES), F32),
                        pltpu.VMEM((2, t, t), F32), pltpu.VMEM((2, t, t), F32),
                        pltpu.VMEM((2, t, t), BF16)])
    return pl.pallas_call(
        functools.partial(_flash_kernel, t=t, rc=rc, lam_init=lam_init),
        grid_spec=grid_spec,
        out_shape=jax.ShapeDtypeStruct((n, A_HEADS * LANES), F32),
        compiler_params=_cparams(("parallel", "arbitrary")))(qi, kj, lamvec, q, k, v, subln_w)


def _decode_kernel(pt_ref, lam_ref, q_ref, kn_ref, vn_ref, w_ref, *rest, pp, lam_init):
    k_refs, v_refs = rest[:pp], rest[pp:2 * pp]
    o_ref, m_ref, l_ref, acc_ref = rest[2 * pp:]
    c = pl.program_id(1)
    rows = 2 * A_HEADS
    prow = k_refs[0].shape[0]

    @pl.when(c == 0)
    def _():
        m_ref[...] = jnp.full(m_ref.shape, NEG_BIG, F32)
        l_ref[...] = jnp.zeros(l_ref.shape, F32)
        acc_ref[...] = jnp.zeros(acc_ref.shape, F32)

    rid = lax.broadcasted_iota(jnp.int32, (rows, LANES), 0)
    lid = lax.broadcasted_iota(jnp.int32, (rows, LANES), 1)
    qb = jnp.where(lid // A_QK == rid % 2, q_ref[...], 0.0).astype(BF16)
    s = jnp.concatenate(
        [lax.dot_general(qb, k_refs[p][...].astype(BF16), (((1,), (1,)), ((), ())),
                         preferred_element_type=F32) for p in range(pp)], axis=-1)
    srow = lax.broadcasted_iota(jnp.int32, s.shape, 0)
    scol = lax.broadcasted_iota(jnp.int32, s.shape, 1)
    s = jnp.where(scol % A_HEADS == srow // 2, s, NEG_BIG)
    m_prev = m_ref[...]
    m_new = jnp.maximum(m_prev, jnp.max(s, axis=-1, keepdims=True))
    p_ = jnp.exp(s - m_new)
    alpha = jnp.exp(m_prev - m_new)
    l_new = alpha * l_ref[...] + jnp.sum(p_, axis=-1, keepdims=True)
    acc = alpha * acc_ref[...]
    for p in range(pp):
        acc = acc + jnp.dot(p_[:, p * prow:(p + 1) * prow].astype(BF16), v_refs[p][...].astype(BF16),
                            preferred_element_type=F32)
    m_ref[...] = m_new
    l_ref[...] = l_new
    acc_ref[...] = acc

    @pl.when(c == pl.num_programs(1) - 1)
    def _():
        kn = kn_ref[...].astype(BF16).astype(F32)
        vn = vn_ref[...].astype(BF16).astype(F32)
        s_self = jnp.sum(qb.astype(F32) * kn, axis=-1, keepdims=True)
        m_fin = jnp.maximum(m_new, s_self)
        a2 = jnp.exp(m_new - m_fin)
        p_self = jnp.exp(s_self - m_fin)
        l_fin = a2 * l_new + p_self
        acc_fin = a2 * acc + p_self.astype(BF16).astype(F32) * vn
        on = acc_fin / l_fin
        lam = _lam_from(lam_ref, lam_init)
        w = w_ref[...]
        for hh in range(A_HEADS):
            o = on[2 * hh:2 * hh + 1] - lam * on[2 * hh + 1:2 * hh + 2]
            o_ref[hh:hh + 1, :] = _rms(o, w) * (1.0 - lam_init)


def decode_diff_attention(q, k_new, v_new, cache_k, cache_v, layer, page_table, lamvec, subln_w, lam_init):
    b, n_pages = page_table.shape
    prow = cache_k.shape[2]
    pp = _tile(n_pages, DECODE_PAGES)
    rows = 2 * A_HEADS

    def row(bb, c, pt):
        return (bb, 0, 0)

    def page_idx(p):
        return lambda bb, c, pt: (layer, pt[bb, c * pp + p], 0, 0)

    row_spec = pl.BlockSpec((None, rows, LANES), row)
    page_specs = [pl.BlockSpec((None, None, prow, LANES), page_idx(p)) for p in range(pp)]
    grid_spec = pltpu.PrefetchScalarGridSpec(
        num_scalar_prefetch=1, grid=(b, n_pages // pp),
        in_specs=[pl.BlockSpec((8, LANES), lambda bb, c, pt: (0, 0)), row_spec, row_spec, row_spec,
                  pl.BlockSpec((1, LANES), lambda bb, c, pt: (0, 0))] + page_specs + page_specs,
        out_specs=pl.BlockSpec((None, A_HEADS, LANES), row),
        scratch_shapes=[pltpu.VMEM((rows, 1), F32), pltpu.VMEM((rows, 1), F32),
                        pltpu.VMEM((rows, LANES), F32)])
    r8 = lambda a: jnp.repeat(a.reshape(b, A_HEADS, LANES), 2, axis=1)
    out = pl.pallas_call(
        functools.partial(_decode_kernel, pp=pp, lam_init=lam_init),
        grid_spec=grid_spec,
        out_shape=jax.ShapeDtypeStruct((b, A_HEADS, LANES), F32),
        compiler_params=_cparams(("parallel", "arbitrary")))(
            page_table, lamvec, r8(q), r8(k_new), r8(v_new), subln_w,
            *([cache_k] * pp), *([cache_v] * pp))
    return out.reshape(b, A_HEADS * LANES)


HK = G_HEADS * G_QK
HV = G_HEADS * G_V
GLA_LEVELS = (1, 2, 4, 8, 16, 32)


def _gla_constants():
    c = CHUNK
    t = np.arange(c)[:, None]
    i = np.arange(c)[None, :]
    def prefix(s):
        return (i // s == t // s) & (i <= t)

    def suffix(s):
        return (i // s == t // s) & (i > t)

    mats = [np.where(t % (2 * s) >= s, prefix(s), suffix(s)) for s in GLA_LEVELS]
    mats += [prefix(c), suffix(c)]
    tri = np.concatenate(mats, axis=0).astype(np.float32)
    j = i
    level = np.full((c, c), -1, np.int32)
    level[t == j] = 0
    for n, s in enumerate(GLA_LEVELS):
        sel = (t // (2 * s) == j // (2 * s)) & (t % (2 * s) >= s) & (j % (2 * s) < s)
        level[sel] = n + 1
    level = np.tile(level, (G_HEADS, 1))
    headmask = (np.arange(HK)[None, :] // G_QK == np.arange(G_HEADS * c)[:, None] // c).astype(np.float32)
    return jnp.asarray(tri, BF16), jnp.asarray(level), jnp.asarray(headmask)


def _stack_heads(x, hm):
    return jnp.concatenate([x] * G_HEADS, axis=0) * hm


def _col_bcast(row, width):
    n = row.shape[1]
    eye = lax.broadcasted_iota(jnp.int32, (n, n), 0) == lax.broadcasted_iota(jnp.int32, (n, n), 1)
    ones = jnp.ones((n, width), BF16)
    out = jnp.zeros((n, width), F32)
    for part in _split3(row):
        d = jnp.where(eye, jnp.broadcast_to(part.astype(F32), (n, n)), 0.0)
        out = out + jnp.dot(d.astype(BF16), ones, preferred_element_type=F32)
    return out


def _gla_kernel(q_ref, k_ref, v_ref, rg_ref, lr_ref, wup_ref, bup_ref, nw_ref, tri_ref, lvl_ref, hm_ref,
                o_ref, s_out_ref, s_ref, *, n_chunks):
    c = CHUNK

    @pl.when(pl.program_id(0) == 0)
    def _():
        s_ref[...] = jnp.zeros(s_ref.shape, F32)

    tri = tri_ref[...]
    lvl = lvl_ref[...]
    hm = hm_ref[...]
    nl = len(GLA_LEVELS)

    def chunk(ci, carry):
        r0 = pl.multiple_of(ci * c, c)
        rows = pl.ds(r0, c)
        q = q_ref[rows, :] * (G_QK ** -0.5)
        k = k_ref[rows, :]
        v = v_ref[rows, :].astype(BF16)
        gk = jnp.dot(lr_ref[rows, :].astype(BF16), wup_ref[...], preferred_element_type=F32) + bup_ref[...]
        g = _log_sigmoid(gk) / G_NORMALIZER
        ps = jnp.zeros(((nl + 2) * c, HK), F32)
        for part in _split3(g):
            ps = ps + jnp.dot(tri, part, preferred_element_type=F32)
        gcum, gsuf = ps[nl * c:(nl + 1) * c], ps[(nl + 1) * c:(nl + 2) * c]
        a = jnp.zeros((G_HEADS * c, c), F32)
        for n in range(nl + 1):
            if n == 0:
                qq, kk = q, k
            else:
                f = jnp.exp(ps[(n - 1) * c:n * c])
                qq, kk = q * f, k * f
            d = lax.dot_general(_stack_heads(qq, hm).astype(BF16), kk.astype(BF16),
                                (((1,), (1,)), ((), ())), preferred_element_type=F32)
            a = jnp.where(lvl == n, d, a)
        s_old = s_ref[...]
        inter = jnp.dot(_stack_heads(q * jnp.exp(gcum), hm).astype(BF16), s_old.astype(BF16),
                        preferred_element_type=F32)
        ab = a.astype(BF16)
        nw = nw_ref[...]
        kv = lax.dot_general((k * jnp.exp(gsuf)).astype(BF16), v, (((0,), (0,)), ((), ())),
                             preferred_element_type=F32)
        decay = jnp.exp(_col_bcast(gcum[c - 1:c, :], G_V))
        for hh in range(G_HEADS):
            vs = slice(hh * G_V, (hh + 1) * G_V)
            o = inter[hh * c:(hh + 1) * c] + jnp.dot(ab[hh * c:(hh + 1) * c], v[:, vs],
                                                     preferred_element_type=F32)
            o_ref[rows, vs] = _rms(o, nw) * _silu(rg_ref[rows, vs])
            ks = slice(hh * G_QK, (hh + 1) * G_QK)
            s_ref[ks, :] = decay[ks] * s_old[ks] + kv[ks, vs]
        return carry

    lax.fori_loop(0, n_chunks, chunk, 0, unroll=SCAN_UNROLL)

    @pl.when(pl.program_id(0) == pl.num_programs(0) - 1)
    def _():
        s_out_ref[...] = s_ref[...]


def gla_prompt(h, lrg, w_up, b_up, norm_w):
    t = h.shape[0]
    tb = _tile(t, 512)
    tri, lvl, hm = _gla_constants()
    full = lambda a: pl.BlockSpec(a.shape, lambda i: (0,) * a.ndim)
    o, s = pl.pallas_call(
        functools.partial(_gla_kernel, n_chunks=tb // CHUNK),
        grid=(t // tb,),
        in_specs=[pl.BlockSpec((tb, HK), lambda i: (i, QG // 2)),
                  pl.BlockSpec((tb, HK), lambda i: (i, KG // 2)),
                  pl.BlockSpec((tb, HV), lambda i: (i, VG // 4)),
                  pl.BlockSpec((tb, HV), lambda i: (i, RG // 4)),
                  pl.BlockSpec((tb, LANES), lambda i: (i, 0)),
                  full(w_up), full(b_up), full(norm_w), full(tri), full(lvl), full(hm)],
        out_specs=[pl.BlockSpec((tb, HV), lambda i: (i, 0)),
                   pl.BlockSpec((HK, G_V), lambda i: (0, 0))],
        out_shape=[jax.ShapeDtypeStruct((t, HV), F32), jax.ShapeDtypeStruct((HK, G_V), F32)],
        scratch_shapes=[pltpu.VMEM((HK, G_V), F32)],
        compiler_params=_cparams(("arbitrary",)))(h, h, h, h, lrg, w_up, b_up, norm_w, tri, lvl, hm)
    return o, s.reshape(G_HEADS, G_QK, G_V)


def _ret_kernel(q_ref, k_ref, v_ref, gr_ref, nw_ref, dm_ref, cross_ref, tail_ref, gc_ref, hm_ref,
                o_ref, s_out_ref, s_ref, *, n_chunks):
    c = CHUNK

    @pl.when(pl.program_id(0) == 0)
    def _():
        s_ref[...] = jnp.zeros(s_ref.shape, F32)

    hm = hm_ref[...]
    dm = dm_ref[...]
    cross = cross_ref[...]
    tail = tail_ref[...]
    gc = gc_ref[...]
    nw = nw_ref[...]

    def chunk(ci, carry):
        r0 = pl.multiple_of(ci * c, c)
        rows = pl.ds(r0, c)
        q = q_ref[rows, :]
        k = k_ref[rows, :]
        v = v_ref[rows, :].astype(BF16)
        qs = _stack_heads(q, hm).astype(BF16)
        a = lax.dot_general(qs, k.astype(BF16), (((1,), (1,)), ((), ())), preferred_element_type=F32) * dm
        s_old = s_ref[...]
        inter = jnp.dot(qs, s_old.astype(BF16), preferred_element_type=F32) * cross
        kv = lax.dot_general((k * tail).astype(BF16), v, (((0,), (0,)), ((), ())), preferred_element_type=F32)
        ab = a.astype(BF16)
        for hh in range(R_HEADS):
            vs = slice(hh * G_V, (hh + 1) * G_V)
            o = inter[hh * c:(hh + 1) * c] + jnp.dot(ab[hh * c:(hh + 1) * c], v[:, vs],
                                                     preferred_element_type=F32)
            o_ref[rows, vs] = _rms(o, nw) * _silu(gr_ref[rows, vs])
            ks = slice(hh * R_QK, (hh + 1) * R_QK)
            s_ref[ks, :] = gc[ks] * s_old[ks] + kv[ks, vs]
        return carry

    lax.fori_loop(0, n_chunks, chunk, 0, unroll=SCAN_UNROLL)

    @pl.when(pl.program_id(0) == pl.num_programs(0) - 1)
    def _():
        s_out_ref[...] = s_ref[...]


def _ret_log_gamma():
    return jnp.log(1.0 - jnp.exp2(-5.0 - jnp.arange(R_HEADS, dtype=F32)))


def ret_prompt(h, qr, kr, norm_w):
    t = h.shape[0]
    tb = _tile(t, 512)
    c = CHUNK
    lg = _ret_log_gamma()
    idx = jnp.arange(c, dtype=F32)
    rel = idx[:, None] - idx[None, :]
    dmat = jnp.where(rel[None] >= 0, jnp.exp(jnp.maximum(rel, 0.0)[None] * lg[:, None, None]), 0.0)
    dmat = dmat.reshape(R_HEADS * c, c)
    cross = jnp.exp((idx + 1.0)[None, :] * lg[:, None]).reshape(R_HEADS * c, 1)
    cross = jnp.broadcast_to(cross, (R_HEADS * c, G_V))
    tail = jnp.exp((c - 1.0 - idx)[None, :] * lg[:, None])
    tail = jnp.repeat(tail.T, R_QK, axis=1)
    gc = jnp.broadcast_to(jnp.repeat(jnp.exp(c * lg), R_QK)[:, None], (HK, G_V))
    _, _, hm = _gla_constants()
    full = lambda a: pl.BlockSpec(a.shape, lambda i: (0,) * a.ndim)
    o, s = pl.pallas_call(
        functools.partial(_ret_kernel, n_chunks=tb // c),
        grid=(t // tb,),
        in_specs=[pl.BlockSpec((tb, HK), lambda i: (i, 0)),
                  pl.BlockSpec((tb, HK), lambda i: (i, 0)),
                  pl.BlockSpec((tb, HV), lambda i: (i, VR // 4)),
                  pl.BlockSpec((tb, HV), lambda i: (i, GR // 4)),
                  full(norm_w), full(dmat), full(cross), full(tail), full(gc), full(hm)],
        out_specs=[pl.BlockSpec((tb, HV), lambda i: (i, 0)),
                   pl.BlockSpec((HK, G_V), lambda i: (0, 0))],
        out_shape=[jax.ShapeDtypeStruct((t, HV), F32), jax.ShapeDtypeStruct((HK, G_V), F32)],
        scratch_shapes=[pltpu.VMEM((HK, G_V), F32)],
        compiler_params=_cparams(("arbitrary",)))(qr, kr, h, h, norm_w, dmat, cross, tail, gc, hm)
    return o, s.reshape(R_HEADS, R_QK, G_V)


def _step_kernel(s_ref, q_ref, k_ref, d_ref, v_ref, gate_ref, nw_ref, s_out_ref, o_ref, *, is_gla, q_scale):
    d = d_ref[...]
    if is_gla:
        d = _log_sigmoid(d) / G_NORMALIZER
    s_new = jnp.exp(d) * s_ref[...] + k_ref[...] * v_ref[...]
    s_out_ref[...] = s_new
    o = jnp.sum((q_ref[...] * q_scale) * s_new, axis=1, keepdims=True)
    o_ref[...] = _rms(o, nw_ref[...]) * _silu(gate_ref[...])


def recurrent_step(state, q, k, dlog, v, gate, norm_w, is_gla, q_scale):
    b = state.shape[0]
    col = lambda a: a.reshape(b, G_HEADS, G_QK, 1)
    rowv = lambda a: a.reshape(b, G_HEADS, 1, G_V)
    cspec = pl.BlockSpec((None, G_HEADS, G_QK, 1), lambda i: (i, 0, 0, 0))
    rspec = pl.BlockSpec((None, G_HEADS, 1, G_V), lambda i: (i, 0, 0, 0))
    sspec = pl.BlockSpec((None, G_HEADS, G_QK, G_V), lambda i: (i, 0, 0, 0))
    s_new, o = pl.pallas_call(
        functools.partial(_step_kernel, is_gla=is_gla, q_scale=q_scale),
        grid=(b,),
        in_specs=[sspec, cspec, cspec, cspec, rspec, rspec, pl.BlockSpec((1, G_V), lambda i: (0, 0))],
        out_specs=[sspec, rspec],
        out_shape=[jax.ShapeDtypeStruct(state.shape, F32), jax.ShapeDtypeStruct((b, G_HEADS, 1, G_V), F32)],
        compiler_params=_cparams(("parallel",)))(state, col(q), col(k), col(dlog), rowv(v), rowv(gate), norm_w)
    return o.reshape(b, HV), s_new


def _mixer_out_kernel(oa_ref, og_ref, or_ref, za_ref, zb_ref, zc_ref, x_ref, wpa_ref, wpb_ref, wpc_ref,
                      wo_ref, g_ref, b_ref, y_ref, *, alpha):
    def branch(o_ref, z_ref, w_ref):
        return _sigmoid(z_ref[...]) * jnp.dot(o_ref[...].astype(BF16), w_ref[...], preferred_element_type=F32)

    merged = branch(oa_ref, za_ref, wpa_ref) + branch(og_ref, zb_ref, wpb_ref) + branch(or_ref, zc_ref, wpc_ref)
    y = alpha * x_ref[...] + jnp.dot(merged.astype(BF16), wo_ref[...], preferred_element_type=F32)
    y_ref[...] = _layer_norm(y, g_ref[...], b_ref[...])


def mixer_out(oa, og, orr, h, x, wpa, wpb, wpc, wo, g, b, alpha):
    m = x.shape[0]
    tm = _tile(m, 256)
    bw = oa.shape[1]
    ospec = pl.BlockSpec((tm, bw), lambda i: (i, 0))
    zspec = lambda blk: pl.BlockSpec((tm, D_MODEL), lambda i: (i, blk // 8))
    xspec = pl.BlockSpec((tm, D_MODEL), lambda i: (i, 0))
    full = lambda a: pl.BlockSpec(a.shape, lambda i: (0,) * a.ndim)
    return pl.pallas_call(
        functools.partial(_mixer_out_kernel, alpha=alpha),
        grid=(m // tm,),
        in_specs=[ospec, ospec, ospec, zspec(ZA), zspec(ZB), zspec(ZC), xspec,
                  full(wpa), full(wpb), full(wpc), full(wo), full(g), full(b)],
        out_specs=xspec,
        out_shape=jax.ShapeDtypeStruct((m, D_MODEL), F32),
        compiler_params=_cparams(("parallel",)))(oa, og, orr, h, h, h, x, wpa, wpb, wpc, wo, g, b)


def _ffn_up_kernel(x_ref, wg_ref, wu_ref, h_ref):
    x = x_ref[...].astype(BF16)
    a = jnp.dot(x, wg_ref[...], preferred_element_type=F32)
    u = jnp.dot(x, wu_ref[...], preferred_element_type=F32)
    h_ref[...] = (_silu(a) * u).astype(h_ref.dtype)


def _ffn_down_kernel(h_ref, wd_ref, x_ref, g_ref, b_ref, y_ref, *, alpha):
    y = alpha * x_ref[...] + jnp.dot(h_ref[...], wd_ref[...], preferred_element_type=F32)
    y_ref[...] = _layer_norm(y, g_ref[...], b_ref[...])


def dense_ffn(x, wg, wu, wd, g, b, alpha):
    m = x.shape[0]
    f = wg.shape[1]
    tm, tf = _tile(m, 512), _tile(f, 1408)
    hmid = pl.pallas_call(
        _ffn_up_kernel, grid=(m // tm, f // tf),
        in_specs=[pl.BlockSpec((tm, D_MODEL), lambda i, j: (i, 0)),
                  pl.BlockSpec((D_MODEL, tf), lambda i, j: (0, j)),
                  pl.BlockSpec((D_MODEL, tf), lambda i, j: (0, j))],
        out_specs=pl.BlockSpec((tm, tf), lambda i, j: (i, j)),
        out_shape=jax.ShapeDtypeStruct((m, f), BF16),
        compiler_params=_cparams(("parallel", "parallel")))(x, wg, wu)
    full = lambda a: pl.BlockSpec(a.shape, lambda i: (0,) * a.ndim)
    xspec = pl.BlockSpec((tm, D_MODEL), lambda i: (i, 0))
    return pl.pallas_call(
        functools.partial(_ffn_down_kernel, alpha=alpha), grid=(m // tm,),
        in_specs=[pl.BlockSpec((tm, f), lambda i: (i, 0)), full(wd), xspec, full(g), full(b)],
        out_specs=xspec,
        out_shape=jax.ShapeDtypeStruct((m, D_MODEL), F32),
        compiler_params=_cparams(("parallel",)))(hmid, wd, x, g, b)


def _top2_weights(x, wh, wl):
    xh = x.astype(BF16)
    xl = (x - xh.astype(F32)).astype(BF16)
    logits = (jnp.dot(xh, wh, preferred_element_type=F32) + jnp.dot(xh, wl, preferred_element_type=F32)
              + jnp.dot(xl, wh, preferred_element_type=F32))
    lane = lax.broadcasted_iota(jnp.int32, logits.shape, 1)
    logits = jnp.where(lane < N_EXPERTS, logits, NEG_BIG)
    m1 = jnp.max(logits, axis=-1, keepdims=True)
    i1 = jnp.min(jnp.where(logits == m1, lane, LANES), axis=-1, keepdims=True)
    rest = jnp.where(lane == i1, NEG_BIG, logits)
    m2 = jnp.max(rest, axis=-1, keepdims=True)
    i2 = jnp.min(jnp.where(rest == m2, lane, LANES), axis=-1, keepdims=True)
    e2 = jnp.exp(m2 - m1)
    w1 = 1.0 / (1.0 + e2)
    w2 = e2 / (1.0 + e2)
    return jnp.where(lane == i1, w1, jnp.where(lane == i2, w2, 0.0))


def _router_kernel(x_ref, wh_ref, wl_ref, comb_ref):
    comb_ref[...] = _top2_weights(x_ref[...], wh_ref[...], wl_ref[...])


def _router_rank_kernel(x_ref, wh_ref, wl_ref, comb_ref, rankc_ref, rankr_ref, cnt_ref):
    comb = _top2_weights(x_ref[...], wh_ref[...], wl_ref[...])
    tb = comb.shape[0]
    routed = comb > 0.0
    ones = jnp.where(routed, 1.0, 0.0)
    earlier = (lax.broadcasted_iota(jnp.int32, (tb, tb), 0) > lax.broadcasted_iota(jnp.int32, (tb, tb), 1))
    rank = jnp.dot(jnp.where(earlier, 1.0, 0.0).astype(BF16), ones.astype(BF16), preferred_element_type=F32)
    rankc = jnp.where(routed, rank, -1.0)
    comb_ref[...] = comb
    rankc_ref[...] = rankc
    cnt_ref[...] = jnp.broadcast_to(jnp.sum(ones, axis=0, keepdims=True), cnt_ref.shape)
    eye = (lax.broadcasted_iota(jnp.int32, (LANES, LANES), 0) == lax.broadcasted_iota(jnp.int32, (LANES, LANES), 1))
    eye = jnp.where(eye, 1.0, 0.0).astype(BF16)
    rt = jnp.zeros((LANES, tb), F32)
    for part in _split3(rankc):
        rt = rt + lax.dot_general(eye, part, (((1,), (1,)), ((), ())), preferred_element_type=F32)
    rankr_ref[...] = rt[:N_EXPERTS]


def _router_weights(w_router):
    wpad = jnp.zeros((D_MODEL, LANES), F32).at[:, :N_EXPERTS].set(w_router)
    wh = wpad.astype(BF16)
    return wh, (wpad - wh.astype(F32)).astype(BF16)


def router(x, w_router):
    m = x.shape[0]
    tm = _tile(m, 512)
    wh, wl = _router_weights(w_router)
    full = lambda a: pl.BlockSpec(a.shape, lambda i: (0,) * a.ndim)
    return pl.pallas_call(
        _router_kernel, grid=(m // tm,),
        in_specs=[pl.BlockSpec((tm, D_MODEL), lambda i: (i, 0)), full(wh), full(wl)],
        out_specs=pl.BlockSpec((tm, LANES), lambda i: (i, 0)),
        out_shape=jax.ShapeDtypeStruct((m, LANES), F32),
        compiler_params=_cparams(("parallel",)))(x, wh, wl)


def router_ranked(x, w_router, tb):
    m = x.shape[0]
    nb = m // tb
    wh, wl = _router_weights(w_router)
    full = lambda a: pl.BlockSpec(a.shape, lambda i: (0,) * a.ndim)
    tok = pl.BlockSpec((tb, LANES), lambda i: (i, 0))
    return pl.pallas_call(
        _router_rank_kernel, grid=(nb,),
        in_specs=[pl.BlockSpec((tb, D_MODEL), lambda i: (i, 0)), full(wh), full(wl)],
        out_specs=[tok, tok, pl.BlockSpec((None, N_EXPERTS, tb), lambda i: (i, 0, 0)),
                   pl.BlockSpec((None, N_EXPERTS, LANES), lambda i: (i, 0, 0))],
        out_shape=[jax.ShapeDtypeStruct((m, LANES), F32), jax.ShapeDtypeStruct((m, LANES), F32),
                   jax.ShapeDtypeStruct((nb, N_EXPERTS, tb), F32),
                   jax.ShapeDtypeStruct((nb, N_EXPERTS, LANES), F32)],
        compiler_params=_cparams(("parallel",)))(x, wh, wl)


def _moe_kernel(x_ref, comb_ref, wg_ref, wu_ref, wd_ref, g_ref, b_ref, y_ref, acc_ref, *, alpha):
    e, f = pl.program_id(1), pl.program_id(2)

    @pl.when((e == 0) & (f == 0))
    def _():
        acc_ref[...] = jnp.zeros(acc_ref.shape, F32)

    x = x_ref[...].astype(BF16)
    a = jnp.dot(x, wg_ref[...], preferred_element_type=F32)
    u = jnp.dot(x, wu_ref[...], preferred_element_type=F32)
    hmid = (_silu(a) * u).astype(BF16)
    comb = comb_ref[...]
    lane = lax.broadcasted_iota(jnp.int32, comb.shape, 1)
    ce = jnp.sum(jnp.where(lane == e, comb, 0.0), axis=-1, keepdims=True)
    acc_ref[...] += ce * jnp.dot(hmid, wd_ref[...], preferred_element_type=F32)

    @pl.when((e == pl.num_programs(1) - 1) & (f == pl.num_programs(2) - 1))
    def _():
        y_ref[...] = _layer_norm(alpha * x_ref[...] + acc_ref[...], g_ref[...], b_ref[...])


def moe_ffn(x, comb, wg, wu, wd, g, b, alpha):
    m = x.shape[0]
    ne, _, f = wg.shape
    tm, tf = _tile(m, 1024), _tile(f, 512)
    xspec = pl.BlockSpec((tm, D_MODEL), lambda i, e, j: (i, 0))
    full = lambda a: pl.BlockSpec(a.shape, lambda i, e, j: (0,) * a.ndim)
    return pl.pallas_call(
        functools.partial(_moe_kernel, alpha=alpha), grid=(m // tm, ne, f // tf),
        in_specs=[xspec, pl.BlockSpec((tm, LANES), lambda i, e, j: (i, 0)),
                  pl.BlockSpec((None, D_MODEL, tf), lambda i, e, j: (e, 0, j)),
                  pl.BlockSpec((None, D_MODEL, tf), lambda i, e, j: (e, 0, j)),
                  pl.BlockSpec((None, tf, D_MODEL), lambda i, e, j: (e, j, 0)),
                  full(g), full(b)],
        out_specs=xspec,
        out_shape=jax.ShapeDtypeStruct((m, D_MODEL), F32),
        scratch_shapes=[pltpu.VMEM((tm, D_MODEL), F32)],
        compiler_params=_cparams(("parallel", "arbitrary", "arbitrary")))(x, comb, wg, wu, wd, g, b)


def _moe_routed_kernel(cnt_ref, x_ref, comb_ref, rankc_ref, rankr_ref, wg_ref, wu_ref, wd_ref, g_ref, b_ref,
                       y_ref, acc_ref, xb_ref, xc_ref, yacc_ref, *, alpha, rows, cap):
    blk, e, f = pl.program_id(0), pl.program_id(1), pl.program_id(2)
    last_f = pl.num_programs(2) - 1
    n = cnt_ref[blk * N_EXPERTS + e]
    tb = x_ref.shape[0]

    @pl.when((e == 0) & (f == 0))
    def _():
        acc_ref[...] = jnp.zeros(acc_ref.shape, F32)
        xb_ref[...] = x_ref[...].astype(BF16)

    def column(ref):
        a = ref[...]
        lane = lax.broadcasted_iota(jnp.int32, a.shape, 1)
        return jnp.sum(jnp.where(lane == e, a, 0.0), axis=-1, keepdims=True)

    def expert(xs):
        a = jnp.dot(xs, wg_ref[...], preferred_element_type=F32)
        u = jnp.dot(xs, wu_ref[...], preferred_element_type=F32)
        return jnp.dot((_silu(a) * u).astype(BF16), wd_ref[...], preferred_element_type=F32)

    @pl.when((n > 0) & (n <= rows))
    def _():
        @pl.when(f == 0)
        def _():
            rr = rankr_ref[...]
            sub = lax.broadcasted_iota(jnp.int32, rr.shape, 0)
            rrow = jnp.sum(jnp.where(sub == e, rr, 0.0), axis=0, keepdims=True)
            slot = lax.broadcasted_iota(jnp.int32, (rows, tb), 0).astype(F32)
            gather = jnp.where(slot == rrow, 1.0, 0.0).astype(BF16)
            xc_ref[...] = jnp.dot(gather, xb_ref[...], preferred_element_type=F32).astype(BF16)
            yacc_ref[0:cap, :] = jnp.zeros((cap, D_MODEL), F32)

        yacc_ref[0:rows, :] += expert(xc_ref[...])

        @pl.when(f == last_f)
        def _():
            slot = lax.broadcasted_iota(jnp.int32, (tb, cap), 1).astype(F32)
            scatter = jnp.where(slot == column(rankc_ref), 1.0, 0.0).astype(BF16)
            acc_ref[...] += column(comb_ref) * jnp.dot(scatter, yacc_ref[0:cap, :].astype(BF16),
                                                       preferred_element_type=F32)

    @pl.when(n > rows)
    def _():
        @pl.when(f == 0)
        def _():
            yacc_ref[...] = jnp.zeros(yacc_ref.shape, F32)

        yacc_ref[...] += expert(xb_ref[...])

        @pl.when(f == last_f)
        def _():
            acc_ref[...] += column(comb_ref) * yacc_ref[...]

    @pl.when((e == pl.num_programs(1) - 1) & (f == last_f))
    def _():
        y_ref[...] = _layer_norm(alpha * x_ref[...] + acc_ref[...], g_ref[...], b_ref[...])


def moe_routed_ffn(x, w_router, wg, wu, wd, g, b, alpha):
    m = x.shape[0]
    ne, _, f = wg.shape
    tb, tf = _tile(m, MOE_BLOCK), _tile(f, MOE_TF)
    rows = min(tb, MOE_ROWS)
    cap = min(tb, -(-rows // LANES) * LANES)
    comb, rankc, rankr, cnt = router_ranked(x, w_router, tb)
    counts = cnt[:, 0, :N_EXPERTS].astype(jnp.int32).reshape(-1)
    xspec = pl.BlockSpec((tb, D_MODEL), lambda i, e, j, c: (i, 0))
    tok = pl.BlockSpec((tb, LANES), lambda i, e, j, c: (i, 0))
    full = lambda a: pl.BlockSpec(a.shape, lambda i, e, j, c: (0,) * a.ndim)
    grid_spec = pltpu.PrefetchScalarGridSpec(
        num_scalar_prefetch=1, grid=(m // tb, ne, f // tf),
        in_specs=[xspec, tok, tok, pl.BlockSpec((None, N_EXPERTS, tb), lambda i, e, j, c: (i, 0, 0)),
                  pl.BlockSpec((None, D_MODEL, tf), lambda i, e, j, c: (e, 0, j)),
                  pl.BlockSpec((None, D_MODEL, tf), lambda i, e, j, c: (e, 0, j)),
                  pl.BlockSpec((None, tf, D_MODEL), lambda i, e, j, c: (e, j, 0)),
                  full(g), full(b)],
        out_specs=xspec,
        scratch_shapes=[pltpu.VMEM((tb, D_MODEL), F32), pltpu.VMEM((tb, D_MODEL), BF16),
                        pltpu.VMEM((rows, D_MODEL), BF16), pltpu.VMEM((tb, D_MODEL), F32)])
    return pl.pallas_call(
        functools.partial(_moe_routed_kernel, alpha=alpha, rows=rows, cap=cap),
        grid_spec=grid_spec,
        out_shape=jax.ShapeDtypeStruct((m, D_MODEL), F32),
        compiler_params=_cparams(("parallel", "arbitrary", "arbitrary")))(
            counts, x, comb, rankc, rankr, wg, wu, wd, g, b)


def _layer_weights(l, w_in, lam_q1, lam_k1, lam_q2, lam_k2, subln_w, w_gla_up, b_gla_up, gla_norm_w,
                   ret_norm_w, w_pa, w_pb, w_pc, w_out, ln1_g, ln1_b, ln2_g, ln2_b):
    wl = w_in[l]
    w_main = jnp.concatenate([wl[:, a:b] for a, b in PACK_ORDER], axis=1).astype(BF16)
    w_lrg = jnp.zeros((D_MODEL, LANES), F32).at[:, :G_RANK].set(wl[:, LRG_OFF:LRG_OFF + G_RANK]).astype(BF16)
    w_up = jnp.zeros((LANES, HK), F32).at[:G_RANK].set(w_gla_up[l]).astype(BF16)
    lamvec = jnp.zeros((8, LANES), F32)
    for r, vec in enumerate((lam_q1, lam_k1, lam_q2, lam_k2)):
        lamvec = lamvec.at[r, :A_QK].set(vec[l].astype(F32))
    row = lambda a: a[l].reshape(1, -1)
    return dict(w_main=w_main, w_lrg=w_lrg, w_up=w_up, b_up=row(b_gla_up), lamvec=lamvec,
                subln=row(subln_w), gla_nw=row(gla_norm_w), ret_nw=row(ret_norm_w),
                wpa=w_pa[l].astype(BF16), wpb=w_pb[l].astype(BF16), wpc=w_pc[l].astype(BF16),
                wo=w_out[l].astype(BF16), ln1_g=row(ln1_g), ln1_b=row(ln1_b),
                ln2_g=row(ln2_g), ln2_b=row(ln2_b), lam_init=0.8 - 0.6 * math.exp(-0.3 * l))


def _project(x, pos, lw):
    h = matmul(x, lw['w_main'], tn_cap=INPROJ_TN)
    lrg = matmul(x, lw['w_lrg'], tn_cap=LANES)
    qr, kr = rotary(h, QR, HK, _ret_tables(pos), 1, 1.0, R_QK ** -0.5, F32)
    return h, lrg, qr, kr


def _channel_mix(x1, l, lw, ffn_w, alpha):
    if l % 2 == 0:
        wg, wu, wd = ffn_w['dense'][l // 2]
        return dense_ffn(x1, wg, wu, wd, lw['ln2_g'], lw['ln2_b'], alpha)
    w_r, wg, wu, wd = ffn_w['moe'][l // 2]
    if x1.shape[0] >= MOE_MIN_ROUTED:
        return moe_routed_ffn(x1, w_r, wg, wu, wd, lw['ln2_g'], lw['ln2_b'], alpha)
    comb = router(x1, w_r)
    return moe_ffn(x1, comb, wg, wu, wd, lw['ln2_g'], lw['ln2_b'], alpha)


def kernel(x_prompt, x_sample, cache_k, cache_v, state_gla, state_ret, page_table, w_in, lam_q1, lam_k1, lam_q2, lam_k2, subln_w, w_gla_up, b_gla_up, gla_norm_w, ret_norm_w, w_pa, w_pb, w_pc, w_out, ln1_g, ln1_b, w_ff_gate, w_ff_up, w_ff_down, w_router, w_exp_gate, w_exp_up, w_exp_down, ln2_g, ln2_b):
    bp, tp, _ = x_prompt.shape
    bs, ts, _ = x_sample.shape
    assert bp == 1 and ts == 1
    depth = w_in.shape[0]
    alpha = (2 * depth) ** 0.25
    n_pool, page = cache_k.shape[1], cache_k.shape[2]
    past_len = page_table.shape[1] * page
    pos_p = jnp.arange(tp, dtype=jnp.int32)
    pos_s = jnp.full((bs,), past_len, jnp.int32)
    ffn_w = dict(
        dense=[(w_ff_gate[i].astype(BF16), w_ff_up[i].astype(BF16), w_ff_down[i].astype(BF16))
               for i in range(w_ff_gate.shape[0])],
        moe=[(w_router[i], w_exp_gate[i].astype(BF16), w_exp_up[i].astype(BF16), w_exp_down[i].astype(BF16))
             for i in range(w_router.shape[0])])
    ck = cache_k.reshape(depth, n_pool, page * A_HEADS, LANES)
    cv = cache_v.reshape(depth, n_pool, page * A_HEADS, LANES)
    lg_col = jnp.broadcast_to(jnp.repeat(_ret_log_gamma(), R_QK)[None, :], (bs, HK))

    yp = x_prompt.reshape(tp, D_MODEL)
    ys = x_sample.reshape(bs, D_MODEL)
    outs = {n: [] for n in ('gp', 'rp', 'ks', 'vs', 'gs', 'rs')}
    row_bufs = None
    cols = lambda a, blk, n: a[:, blk * LANES:(blk + n) * LANES]
    for l in range(depth):
        lw = _layer_weights(l, w_in, lam_q1, lam_k1, lam_q2, lam_k2, subln_w, w_gla_up, b_gla_up,
                            gla_norm_w, ret_norm_w, w_pa, w_pb, w_pc, w_out, ln1_g, ln1_b, ln2_g, ln2_b)
        h, lrg, qr, kr = _project(yp, pos_p, lw)
        qa, ka, va, *row_bufs = rotary_attention(h, _rope_tables(pos_p), ROPE_DIM // 2,
                                                 A_QK ** -0.5 * LOG2E, l, depth, row_bufs)
        oa = flash_diff_attention(qa, ka, va, lw['lamvec'], lw['subln'], lw['lam_init'])
        og, s_gla = gla_prompt(h, lrg, lw['w_up'], lw['b_up'], lw['gla_nw'])
        orr, s_ret = ret_prompt(h, qr, kr, lw['ret_nw'])
        x1 = mixer_out(oa, og, orr, h, yp, lw['wpa'], lw['wpb'], lw['wpc'], lw['wo'],
                       lw['ln1_g'], lw['ln1_b'], alpha)
        yp = _channel_mix(x1, l, lw, ffn_w, alpha)
        outs['gp'].append(s_gla[None])
        outs['rp'].append(s_ret[None])
        h, lrg, qr, kr = _project(ys, pos_s, lw)
        qa, ka = rotary(h, QA, A_HEADS * LANES, _rope_tables(pos_s), ROPE_DIM // 2, A_QK ** -0.5, 1.0, F32)
        va = cols(h, VA, 4)
        oa = decode_diff_attention(qa, ka, va, ck, cv, l, page_table, lw['lamvec'], lw['subln'],
                                   lw['lam_init'])
        gk = matmul(lrg, lw['w_up'], bias=lw['b_up'])
        og, s_gla = recurrent_step(state_gla[l], cols(h, QG, 2), cols(h, KG, 2), gk, cols(h, VG, 4),
                                   cols(h, RG, 4), lw['gla_nw'], True, G_QK ** -0.5)
        orr, s_ret = recurrent_step(state_ret[l], qr, kr, lg_col, cols(h, VR, 4), cols(h, GR, 4),
                                    lw['ret_nw'], False, 1.0)
        x1 = mixer_out(oa, og, orr, h, ys, lw['wpa'], lw['wpb'], lw['wpc'], lw['wo'],
                       lw['ln1_g'], lw['ln1_b'], alpha)
        ys = _channel_mix(x1, l, lw, ffn_w, alpha)
        outs['ks'].append(ka.reshape(bs, 1, A_HEADS, 2 * A_QK))
        outs['vs'].append(va.reshape(bs, 1, A_HEADS, 2 * A_QK))
        outs['gs'].append(s_gla)
        outs['rs'].append(s_ret)

    st = lambda n: jnp.stack(outs[n])
    kv_shape = (depth, bp, tp, A_HEADS, 2 * A_QK)
    return (yp.reshape(bp, tp, D_MODEL), ys.reshape(bs, ts, D_MODEL), row_bufs[0].reshape(kv_shape),
            row_bufs[1].reshape(kv_shape), st('gp'),
            st('rp'), st('ks'), st('vs'), st('gs'), st('rs'))
```

```python
import functools
import math

import numpy as np
import jax
import jax.numpy as jnp
from jax import lax
from jax.experimental import pallas as pl
from jax.experimental.pallas import tpu as pltpu

F32 = jnp.float32
BF16 = jnp.bfloat16

D_MODEL = 1024
A_HEADS = 4
A_QK = 64
ROPE_DIM = 16
ROPE_THETA = 500000.0
G_HEADS = 4
G_QK = 64
G_V = 128
G_RANK = 16
G_NORMALIZER = 16.0
R_HEADS = 4
R_QK = 64
R_ANGLE_BASE = 10000.0
CHUNK = 64
N_EXPERTS = 8
NORM_EPS = 1e-5
NEG_BIG = -1e30
LOG2E = math.log2(math.e)
FLASH_T = 1024
FLASH_RC = 32
FLASH_KCOLS = 256
FLASH_PVROWS = 128
DECODE_PAGES = 16
SCAN_UNROLL = 4
MOE_BLOCK = 1024
MOE_ROWS = 320
MOE_MIN_ROUTED = 256
MOE_TF = 896
INPROJ_TN = 1536

LANES = 128
VMEM_LIMIT = 56 * 1024 * 1024

ZA, ZB, ZC, QA, KA, VA, VG, RG, VR, GR, QG, KG, QR, KR = 0, 8, 16, 24, 28, 32, 36, 40, 44, 48, 52, 54, 56, 58
PACK_ORDER = ((4624, 7696), (0, 1536), (2048, 2560), (2576, 3088), (3600, 4112), (4112, 4624),
              (1536, 2048), (3088, 3600))
LRG_OFF = 2560


def _cparams(sem):
    return pltpu.CompilerParams(dimension_semantics=sem, vmem_limit_bytes=VMEM_LIMIT)


def _tile(n, cap):
    c = min(n, cap)
    while n % c:
        c -= 1
    return c


def _layer_norm(y, g, b):
    mu = jnp.mean(y, axis=-1, keepdims=True)
    d = y - mu
    var = jnp.mean(d * d, axis=-1, keepdims=True)
    return d * lax.rsqrt(var + NORM_EPS) * g + b


def _rms(o, w):
    return o * lax.rsqrt(jnp.mean(o * o, axis=-1, keepdims=True) + NORM_EPS) * w


def _silu(x):
    return x * (1.0 / (1.0 + jnp.exp(-x)))


def _sigmoid(x):
    return 1.0 / (1.0 + jnp.exp(-x))


def _log_sigmoid(x):
    return jnp.minimum(x, 0.0) - jnp.log(1.0 + jnp.exp(-jnp.abs(x)))


def _split3(x):
    a = x.astype(BF16)
    r = x - a.astype(F32)
    b = r.astype(BF16)
    c = (r - b.astype(F32)).astype(BF16)
    return a, b, c


def _mm_kernel(x_ref, w_ref, o_ref):
    o_ref[...] = jnp.dot(x_ref[...].astype(BF16), w_ref[...],
                         preferred_element_type=F32).astype(o_ref.dtype)


def _mm_bias_kernel(x_ref, w_ref, b_ref, o_ref):
    o_ref[...] = (jnp.dot(x_ref[...].astype(BF16), w_ref[...],
                          preferred_element_type=F32) + b_ref[...]).astype(o_ref.dtype)


def matmul(x, w, bias=None, out_dtype=F32, tm_cap=1024, tn_cap=768):
    m, k = x.shape
    n = w.shape[1]
    tm, tn = _tile(m, tm_cap), _tile(n, tn_cap)
    in_specs = [pl.BlockSpec((tm, k), lambda i, j: (i, 0)),
                pl.BlockSpec((k, tn), lambda i, j: (0, j))]
    args = [x, w]
    kern = _mm_kernel
    if bias is not None:
        in_specs.append(pl.BlockSpec((1, tn), lambda i, j: (0, j)))
        args.append(bias)
        kern = _mm_bias_kernel
    return pl.pallas_call(
        kern, grid=(m // tm, n // tn), in_specs=in_specs,
        out_specs=pl.BlockSpec((tm, tn), lambda i, j: (i, j)),
        out_shape=jax.ShapeDtypeStruct((m, n), out_dtype),
        compiler_params=_cparams(("parallel", "parallel")))(*args)


def _rot_kernel(x_ref, c_ref, s1_ref, s2_ref, oq_ref, ok_ref, *, shift, q_scale, k_scale):
    c, s1, s2 = c_ref[...], s1_ref[...], s2_ref[...]
    nq = oq_ref.shape[1] // LANES
    nk = ok_ref.shape[1] // LANES
    for b in range(nq + nk):
        x = x_ref[:, b * LANES:(b + 1) * LANES]
        y = x * c + pltpu.roll(x, LANES - shift, 1) * s1 + pltpu.roll(x, shift, 1) * s2
        if b < nq:
            oq_ref[:, b * LANES:(b + 1) * LANES] = (y * q_scale).astype(oq_ref.dtype)
        else:
            ok_ref[:, (b - nq) * LANES:(b - nq + 1) * LANES] = (y * k_scale).astype(ok_ref.dtype)


def rotary(h, col_blk, width, tables, shift, q_scale, k_scale, q_dtype):
    m = h.shape[0]
    tm = _tile(m, 512)
    blk = col_blk * LANES // (2 * width)
    tspec = pl.BlockSpec((tm, LANES), lambda i: (i, 0))
    return pl.pallas_call(
        functools.partial(_rot_kernel, shift=shift, q_scale=q_scale, k_scale=k_scale),
        grid=(m // tm,),
        in_specs=[pl.BlockSpec((tm, 2 * width), lambda i: (i, blk)), tspec, tspec, tspec],
        out_specs=[pl.BlockSpec((tm, width), lambda i: (i, 0)),
                   pl.BlockSpec((tm, width), lambda i: (i, 0))],
        out_shape=[jax.ShapeDtypeStruct((m, width), q_dtype),
                   jax.ShapeDtypeStruct((m, width), F32)],
        compiler_params=_cparams(("parallel",)))(h, *tables)


def _rot_att_kernel(x_ref, v_ref, c_ref, s1_ref, s2_ref, *rest, shift, q_scale):
    oq_ref, ok_ref, ov_ref, okr_ref, ovr_ref = rest[-5:]
    c, s1, s2 = c_ref[...], s1_ref[...], s2_ref[...]
    tm = x_ref.shape[0]
    for b in range(2 * A_HEADS):
        x = x_ref[:, b * LANES:(b + 1) * LANES]
        y = x * c + pltpu.roll(x, LANES - shift, 1) * s1 + pltpu.roll(x, shift, 1) * s2
        if b < A_HEADS:
            oq_ref[:, b * LANES:(b + 1) * LANES] = (y * q_scale).astype(oq_ref.dtype)
        else:
            hh = b - A_HEADS
            ok_ref[:, hh * LANES:(hh + 1) * LANES] = y.astype(ok_ref.dtype)
            okr_ref[pl.ds(hh, tm, stride=A_HEADS), :] = y
    for hh in range(A_HEADS):
        v = v_ref[:, hh * LANES:(hh + 1) * LANES]
        ov_ref[:, hh * LANES:(hh + 1) * LANES] = v.astype(ov_ref.dtype)
        ovr_ref[pl.ds(hh, tm, stride=A_HEADS), :] = v


def rotary_attention(h, tables, shift, q_scale, layer, depth, row_bufs=None):
    m = h.shape[0]
    tm = _tile(m, 512)
    nblk = m // tm
    width = A_HEADS * LANES
    tspec = pl.BlockSpec((tm, LANES), lambda i: (i, 0))
    wide = pl.BlockSpec((tm, width), lambda i: (i, 0))
    rowsp = pl.BlockSpec((tm * A_HEADS, LANES), lambda i: (layer * nblk + i, 0))
    in_specs = [pl.BlockSpec((tm, 2 * width), lambda i: (i, QA // 8)),
                pl.BlockSpec((tm, width), lambda i: (i, VA // 4)), tspec, tspec, tspec]
    args = [h, h, *tables]
    aliases = {}
    if row_bufs is not None:
        in_specs += [pl.BlockSpec(memory_space=pl.ANY)] * 2
        aliases = {len(args): 3, len(args) + 1: 4}
        args += list(row_bufs)
    return pl.pallas_call(
        functools.partial(_rot_att_kernel, shift=shift, q_scale=q_scale),
        grid=(nblk,),
        in_specs=in_specs,
        out_specs=[wide, wide, wide, rowsp, rowsp],
        out_shape=[jax.ShapeDtypeStruct((m, width), BF16)] * 3
                  + [jax.ShapeDtypeStruct((depth * m * A_HEADS, LANES), F32)] * 2,
        input_output_aliases=aliases,
        compiler_params=_cparams(("parallel",)))(*args)


def _rope_tables(pos):
    half = ROPE_DIM // 2
    inv = ROPE_THETA ** (-jnp.arange(half, dtype=F32) * 2.0 / ROPE_DIM)
    ang = pos.astype(F32)[:, None] * inv[None, :]
    cos, sin = jnp.cos(ang), jnp.sin(ang)
    lane = np.arange(LANES) % A_QK
    fi = lane % half
    cos_l, sin_l = cos[:, fi], sin[:, fi]
    in_rot = (lane < ROPE_DIM)[None, :]
    lo = (lane < half)[None, :]
    c = jnp.where(in_rot, cos_l, 1.0)
    s1 = jnp.where(lo, -sin_l, 0.0)
    s2 = jnp.where(in_rot & ~lo, sin_l, 0.0)
    return c, s1, s2


def _ret_tables(pos):
    inv = 1.0 / (R_ANGLE_BASE ** jnp.linspace(0.0, 1.0, R_QK // 2, dtype=F32))
    ang = pos.astype(F32)[:, None] * inv[None, :]
    cos, sin = jnp.cos(ang), jnp.sin(ang)
    lane = np.arange(LANES) % R_QK
    cos_l, sin_l = cos[:, lane // 2], sin[:, lane // 2]
    even = (lane % 2 == 0)[None, :]
    return cos_l, jnp.where(even, -sin_l, 0.0), jnp.where(even, 0.0, sin_l)


def _lam_from(lam_ref, lam_init):
    v = lam_ref[...]
    t1 = jnp.sum(v[0:1] * v[1:2], axis=-1, keepdims=True)
    t2 = jnp.sum(v[2:3] * v[3:4], axis=-1, keepdims=True)
    return jnp.exp(t1) - jnp.exp(t2) + lam_init


def _flash_kernel(qi_ref, kj_ref, lam_ref, q_ref, k_ref, v_ref, w_ref, o_ref, m_ref, acc_ref, sa_ref, sb_ref,
                  p_ref, *, t, rc, lam_init):
    step = pl.program_id(1)
    i, j = qi_ref[step], kj_ref[step]
    nch = t // LANES

    kw = min(t, FLASH_KCOLS)
    pvr = min(t, FLASH_PVROWS)

    def produce_items(s_dst):
        items = []
        for c in range(2):
            for kc in range(t // kw):
                def item(c=c, kc=kc):
                    q = q_ref[...]
                    lane = lax.broadcasted_iota(jnp.int32, q.shape, 1)
                    qc = jnp.where((lane < A_QK) if c == 0 else (lane >= A_QK), q, jnp.zeros_like(q))
                    kk = k_ref[kc * kw:(kc + 1) * kw, :].astype(BF16)
                    s_dst[c, :, kc * kw:(kc + 1) * kw] = lax.dot_general(
                        qc, kk, (((1,), (1,)), ((), ())), preferred_element_type=F32)
                items.append(item)
        return items

    def consume_items(s_src, masked):
        items = []
        for g in range(t // pvr):
            def sweep(g=g):
                for r in range(g * pvr // rc, (g + 1) * pvr // rc):
                    rows = slice(r * rc, (r + 1) * rc)
                    if masked:
                        keep = (lax.broadcasted_iota(jnp.int32, (rc, t), 1)
                                <= r * rc + lax.broadcasted_iota(jnp.int32, (rc, t), 0))
                    for c in range(2):
                        s = s_src[c, rows, :]
                        if masked:
                            s = jnp.where(keep, s, NEG_BIG)
                        m_prev = m_ref[c, rows, :]
                        m_new = jnp.maximum(m_prev, jnp.max(s, axis=-1, keepdims=True))
                        p = jnp.exp2(s - jnp.concatenate([m_new] * nch, axis=1))
                        alpha = jnp.exp2(m_prev - m_new)
                        acc_ref[c, rows, :] = jnp.concatenate([alpha, alpha], axis=1) * acc_ref[c, rows, :]
                        m_ref[c, rows, :] = m_new
                        p_ref[c, rows, :] = p.astype(BF16)

            def pv(g=g):
                v = jnp.concatenate([v_ref[...].astype(BF16), jnp.ones((t, LANES), BF16)], axis=1)
                rows = slice(g * pvr, (g + 1) * pvr)
                for c in range(2):
                    acc_ref[c, rows, :] += jnp.dot(p_ref[c, rows, :], v, preferred_element_type=F32)
            items += [sweep, pv]
        return items

    def run(*item_lists):
        longest = max(len(l) for l in item_lists)
        for n in range(longest):
            for l in item_lists:
                lo, hi = n * len(l) // longest, (n + 1) * len(l) // longest
                for it in l[lo:hi]:
                    it()

    def produce(s_dst):
        run(produce_items(s_dst))

    def consume(s_src, masked):
        run(consume_items(s_src, masked))

    even = j % 2 == 0

    @pl.when(j == 0)
    def _():
        m_ref[...] = jnp.full(m_ref.shape, NEG_BIG, F32)
        acc_ref[...] = jnp.zeros(acc_ref.shape, F32)
        produce(sa_ref)

    steady = (j >= 1) & (j <= i)

    @pl.when(steady & even)
    def _():
        run(consume_items(sb_ref, False), produce_items(sa_ref))

    @pl.when(steady & jnp.logical_not(even))
    def _():
        run(consume_items(sa_ref, False), produce_items(sb_ref))

    drain = j == i + 1
    pl.when(drain & even)(functools.partial(consume, sb_ref, True))
    pl.when(drain & jnp.logical_not(even))(functools.partial(consume, sa_ref, True))

    @pl.when(drain)
    def _():
        lam = _lam_from(lam_ref, lam_init)
        o = (acc_ref[0, :, :LANES] / acc_ref[0, :, LANES:]
             - lam * (acc_ref[1, :, :LANES] / acc_ref[1, :, LANES:]))
        o_ref[...] = _rms(o, w_ref[...]) * (1.0 - lam_init)


def flash_diff_attention(q, k, v, lamvec, subln_w, lam_init, t_cap=FLASH_T, rc=FLASH_RC):
    n = q.shape[0]
    t = _tile(n, t_cap)
    rc = _tile(t, rc)
    nb = n // t
    pairs = [(i, j) for i in range(nb) for j in range(i + 2)]
    qi = jnp.asarray([p[0] for p in pairs], jnp.int32)
    kj = jnp.asarray([p[1] for p in pairs], jnp.int32)
    grid_spec = pltpu.PrefetchScalarGridSpec(
        num_scalar_prefetch=2, grid=(A_HEADS, len(pairs)),
        in_specs=[pl.BlockSpec((8, LANES), lambda hh, s, qi, kj: (0, 0)),
                  pl.BlockSpec((t, LANES), lambda hh, s, qi, kj: (qi[s], hh)),
                  pl.BlockSpec((t, LANES), lambda hh, s, qi, kj: (jnp.minimum(kj[s], qi[s]), hh)),
                  pl.BlockSpec((t, LANES), lambda hh, s, qi, kj: (jnp.clip(kj[s] - 1, 0, qi[s]), hh)),
                  pl.BlockSpec((1, LANES), lambda hh, s, qi, kj: (0, 0))],
        out_specs=pl.BlockSpec((t, LANES), lambda hh, s, qi, kj: (qi[s], hh)),
        scratch_shapes=[pltpu.VMEM((2, t, LANES), F32), pltpu.VMEM((2, t, 2 * LANES), F32),
                        pltpu.VMEM((2, t, t), F32), pltpu.VMEM((2, t, t), F32),
                        pltpu.VMEM((2, t, t), BF16)])
    return pl.pallas_call(
        functools.partial(_flash_kernel, t=t, rc=rc, lam_init=lam_init),
        grid_spec=grid_spec,
        out_shape=jax.ShapeDtypeStruct((n, A_HEADS * LANES), F32),
        compiler_params=_cparams(("parallel", "arbitrary")))(qi, kj, lamvec, q, k, v, subln_w)


def _decode_kernel(pt_ref, lam_ref, q_ref, kn_ref, vn_ref, w_ref, *rest, pp, lam_init):
    k_refs, v_refs = rest[:pp], rest[pp:2 * pp]
    o_ref, m_ref, l_ref, acc_ref = rest[2 * pp:]
    c = pl.program_id(1)
    rows = 2 * A_HEADS
    prow = k_refs[0].shape[0]

    @pl.when(c == 0)
    def _():
        m_ref[...] = jnp.full(m_ref.shape, NEG_BIG, F32)
        l_ref[...] = jnp.zeros(l_ref.shape, F32)
        acc_ref[...] = jnp.zeros(acc_ref.shape, F32)

    rid = lax.broadcasted_iota(jnp.int32, (rows, LANES), 0)
    lid = lax.broadcasted_iota(jnp.int32, (rows, LANES), 1)
    qb = jnp.where(lid // A_QK == rid % 2, q_ref[...], 0.0).astype(BF16)
    s = jnp.concatenate(
        [lax.dot_general(qb, k_refs[p][...].astype(BF16), (((1,), (1,)), ((), ())),
                         preferred_element_type=F32) for p in range(pp)], axis=-1)
    srow = lax.broadcasted_iota(jnp.int32, s.shape, 0)
    scol = lax.broadcasted_iota(jnp.int32, s.shape, 1)
    s = jnp.where(scol % A_HEADS == srow // 2, s, NEG_BIG)
    m_prev = m_ref[...]
    m_new = jnp.maximum(m_prev, jnp.max(s, axis=-1, keepdims=True))
    p_ = jnp.exp(s - m_new)
    alpha = jnp.exp(m_prev - m_new)
    l_new = alpha * l_ref[...] + jnp.sum(p_, axis=-1, keepdims=True)
    acc = alpha * acc_ref[...]
    for p in range(pp):
        acc = acc + jnp.dot(p_[:, p * prow:(p + 1) * prow].astype(BF16), v_refs[p][...].astype(BF16),
                            preferred_element_type=F32)
    m_ref[...] = m_new
    l_ref[...] = l_new
    acc_ref[...] = acc

    @pl.when(c == pl.num_programs(1) - 1)
    def _():
        kn = kn_ref[...].astype(BF16).astype(F32)
        vn = vn_ref[...].astype(BF16).astype(F32)
        s_self = jnp.sum(qb.astype(F32) * kn, axis=-1, keepdims=True)
        m_fin = jnp.maximum(m_new, s_self)
        a2 = jnp.exp(m_new - m_fin)
        p_self = jnp.exp(s_self - m_fin)
        l_fin = a2 * l_new + p_self
        acc_fin = a2 * acc + p_self.astype(BF16).astype(F32) * vn
        on = acc_fin / l_fin
        lam = _lam_from(lam_ref, lam_init)
        w = w_ref[...]
        for hh in range(A_HEADS):
            o = on[2 * hh:2 * hh + 1] - lam * on[2 * hh + 1:2 * hh + 2]
            o_ref[hh:hh + 1, :] = _rms(o, w) * (1.0 - lam_init)


def decode_diff_attention(q, k_new, v_new, cache_k, cache_v, layer, page_table, lamvec, subln_w, lam_init):
    b, n_pages = page_table.shape
    prow = cache_k.shape[2]
    pp = _tile(n_pages, DECODE_PAGES)
    rows = 2 * A_HEADS

    def row(bb, c, pt):
        return (bb, 0, 0)

    def page_idx(p):
        return lambda bb, c, pt: (layer, pt[bb, c * pp + p], 0, 0)

    row_spec = pl.BlockSpec((None, rows, LANES), row)
    page_specs = [pl.BlockSpec((None, None, prow, LANES), page_idx(p)) for p in range(pp)]
    grid_spec = pltpu.PrefetchScalarGridSpec(
        num_scalar_prefetch=1, grid=(b, n_pages // pp),
        in_specs=[pl.BlockSpec((8, LANES), lambda bb, c, pt: (0, 0)), row_spec, row_spec, row_spec,
                  pl.BlockSpec((1, LANES), lambda bb, c, pt: (0, 0))] + page_specs + page_specs,
        out_specs=pl.BlockSpec((None, A_HEADS, LANES), row),
        scratch_shapes=[pltpu.VMEM((rows, 1), F32), pltpu.VMEM((rows, 1), F32),
                        pltpu.VMEM((rows, LANES), F32)])
    r8 = lambda a: jnp.repeat(a.reshape(b, A_HEADS, LANES), 2, axis=1)
    out = pl.pallas_call(
        functools.partial(_decode_kernel, pp=pp, lam_init=lam_init),
        grid_spec=grid_spec,
        out_shape=jax.ShapeDtypeStruct((b, A_HEADS, LANES), F32),
        compiler_params=_cparams(("parallel", "arbitrary")))(
            page_table, lamvec, r8(q), r8(k_new), r8(v_new), subln_w,
            *([cache_k] * pp), *([cache_v] * pp))
    return out.reshape(b, A_HEADS * LANES)


HK = G_HEADS * G_QK
HV = G_HEADS * G_V
GLA_LEVELS = (1, 2, 4, 8, 16, 32)


def _gla_constants():
    c = CHUNK
    t = np.arange(c)[:, None]
    i = np.arange(c)[None, :]
    def prefix(s):
        return (i // s == t // s) & (i <= t)

    def suffix(s):
        return (i // s == t // s) & (i > t)

    mats = [np.where(t % (2 * s) >= s, prefix(s), suffix(s)) for s in GLA_LEVELS]
    mats += [prefix(c), suffix(c)]
    tri = np.concatenate(mats, axis=0).astype(np.float32)
    j = i
    level = np.full((c, c), -1, np.int32)
    level[t == j] = 0
    for n, s in enumerate(GLA_LEVELS):
        sel = (t // (2 * s) == j // (2 * s)) & (t % (2 * s) >= s) & (j % (2 * s) < s)
        level[sel] = n + 1
    level = np.tile(level, (G_HEADS, 1))
    headmask = (np.arange(HK)[None, :] // G_QK == np.arange(G_HEADS * c)[:, None] // c).astype(np.float32)
    return jnp.asarray(tri, BF16), jnp.asarray(level), jnp.asarray(headmask)


def _stack_heads(x, hm):
    return jnp.concatenate([x] * G_HEADS, axis=0) * hm


def _col_bcast(row, width):
    n = row.shape[1]
    eye = lax.broadcasted_iota(jnp.int32, (n, n), 0) == lax.broadcasted_iota(jnp.int32, (n, n), 1)
    ones = jnp.ones((n, width), BF16)
    out = jnp.zeros((n, width), F32)
    for part in _split3(row):
        d = jnp.where(eye, jnp.broadcast_to(part.astype(F32), (n, n)), 0.0)
        out = out + jnp.dot(d.astype(BF16), ones, preferred_element_type=F32)
    return out


def _gla_kernel(q_ref, k_ref, v_ref, rg_ref, lr_ref, wup_ref, bup_ref, nw_ref, tri_ref, lvl_ref, hm_ref,
                o_ref, s_out_ref, s_ref, *, n_chunks):
    c = CHUNK

    @pl.when(pl.program_id(0) == 0)
    def _():
        s_ref[...] = jnp.zeros(s_ref.shape, F32)

    tri = tri_ref[...]
    lvl = lvl_ref[...]
    hm = hm_ref[...]
    nl = len(GLA_LEVELS)

    def chunk(ci, carry):
        r0 = pl.multiple_of(ci * c, c)
        rows = pl.ds(r0, c)
        q = q_ref[rows, :] * (G_QK ** -0.5)
        k = k_ref[rows, :]
        v = v_ref[rows, :].astype(BF16)
        gk = jnp.dot(lr_ref[rows, :].astype(BF16), wup_ref[...], preferred_element_type=F32) + bup_ref[...]
        g = _log_sigmoid(gk) / G_NORMALIZER
        ps = jnp.zeros(((nl + 2) * c, HK), F32)
        for part in _split3(g):
            ps = ps + jnp.dot(tri, part, preferred_element_type=F32)
        gcum, gsuf = ps[nl * c:(nl + 1) * c], ps[(nl + 1) * c:(nl + 2) * c]
        a = jnp.zeros((G_HEADS * c, c), F32)
        for n in range(nl + 1):
            if n == 0:
                qq, kk = q, k
            else:
                f = jnp.exp(ps[(n - 1) * c:n * c])
                qq, kk = q * f, k * f
            d = lax.dot_general(_stack_heads(qq, hm).astype(BF16), kk.astype(BF16),
                                (((1,), (1,)), ((), ())), preferred_element_type=F32)
            a = jnp.where(lvl == n, d, a)
        s_old = s_ref[...]
        inter = jnp.dot(_stack_heads(q * jnp.exp(gcum), hm).astype(BF16), s_old.astype(BF16),
                        preferred_element_type=F32)
        ab = a.astype(BF16)
        nw = nw_ref[...]
        kv = lax.dot_general((k * jnp.exp(gsuf)).astype(BF16), v, (((0,), (0,)), ((), ())),
                             preferred_element_type=F32)
        decay = jnp.exp(_col_bcast(gcum[c - 1:c, :], G_V))
        for hh in range(G_HEADS):
            vs = slice(hh * G_V, (hh + 1) * G_V)
            o = inter[hh * c:(hh + 1) * c] + jnp.dot(ab[hh * c:(hh + 1) * c], v[:, vs],
                                                     preferred_element_type=F32)
            o_ref[rows, vs] = _rms(o, nw) * _silu(rg_ref[rows, vs])
            ks = slice(hh * G_QK, (hh + 1) * G_QK)
            s_ref[ks, :] = decay[ks] * s_old[ks] + kv[ks, vs]
        return carry

    lax.fori_loop(0, n_chunks, chunk, 0, unroll=SCAN_UNROLL)

    @pl.when(pl.program_id(0) == pl.num_programs(0) - 1)
    def _():
        s_out_ref[...] = s_ref[...]


def gla_prompt(h, lrg, w_up, b_up, norm_w):
    t = h.shape[0]
    tb = _tile(t, 512)
    tri, lvl, hm = _gla_constants()
    full = lambda a: pl.BlockSpec(a.shape, lambda i: (0,) * a.ndim)
    o, s = pl.pallas_call(
        functools.partial(_gla_kernel, n_chunks=tb // CHUNK),
        grid=(t // tb,),
        in_specs=[pl.BlockSpec((tb, HK), lambda i: (i, QG // 2)),
                  pl.BlockSpec((tb, HK), lambda i: (i, KG // 2)),
                  pl.BlockSpec((tb, HV), lambda i: (i, VG // 4)),
                  pl.BlockSpec((tb, HV), lambda i: (i, RG // 4)),
                  pl.BlockSpec((tb, LANES), lambda i: (i, 0)),
                  full(w_up), full(b_up), full(norm_w), full(tri), full(lvl), full(hm)],
        out_specs=[pl.BlockSpec((tb, HV), lambda i: (i, 0)),
                   pl.BlockSpec((HK, G_V), lambda i: (0, 0))],
        out_shape=[jax.ShapeDtypeStruct((t, HV), F32), jax.ShapeDtypeStruct((HK, G_V), F32)],
        scratch_shapes=[pltpu.VMEM((HK, G_V), F32)],
        compiler_params=_cparams(("arbitrary",)))(h, h, h, h, lrg, w_up, b_up, norm_w, tri, lvl, hm)
    return o, s.reshape(G_HEADS, G_QK, G_V)


def _ret_kernel(q_ref, k_ref, v_ref, gr_ref, nw_ref, dm_ref, cross_ref, tail_ref, gc_ref, hm_ref,
                o_ref, s_out_ref, s_ref, *, n_chunks):
    c = CHUNK

    @pl.when(pl.program_id(0) == 0)
    def _():
        s_ref[...] = jnp.zeros(s_ref.shape, F32)

    hm = hm_ref[...]
    dm = dm_ref[...]
    cross = cross_ref[...]
    tail = tail_ref[...]
    gc = gc_ref[...]
    nw = nw_ref[...]

    def chunk(ci, carry):
        r0 = pl.multiple_of(ci * c, c)
        rows = pl.ds(r0, c)
        q = q_ref[rows, :]
        k = k_ref[rows, :]
        v = v_ref[rows, :].astype(BF16)
        qs = _stack_heads(q, hm).astype(BF16)
        a = lax.dot_general(qs, k.astype(BF16), (((1,), (1,)), ((), ())), preferred_element_type=F32) * dm
        s_old = s_ref[...]
        inter = jnp.dot(qs, s_old.astype(BF16), preferred_element_type=F32) * cross
        kv = lax.dot_general((k * tail).astype(BF16), v, (((0,), (0,)), ((), ())), preferred_element_type=F32)
        ab = a.astype(BF16)
        for hh in range(R_HEADS):
            vs = slice(hh * G_V, (hh + 1) * G_V)
            o = inter[hh * c:(hh + 1) * c] + jnp.dot(ab[hh * c:(hh + 1) * c], v[:, vs],
                                                     preferred_element_type=F32)
            o_ref[rows, vs] = _rms(o, nw) * _silu(gr_ref[rows, vs])
            ks = slice(hh * R_QK, (hh + 1) * R_QK)
            s_ref[ks, :] = gc[ks] * s_old[ks] + kv[ks, vs]
        return carry

    lax.fori_loop(0, n_chunks, chunk, 0, unroll=SCAN_UNROLL)

    @pl.when(pl.program_id(0) == pl.num_programs(0) - 1)
    def _():
        s_out_ref[...] = s_ref[...]


def _ret_log_gamma():
    return jnp.log(1.0 - jnp.exp2(-5.0 - jnp.arange(R_HEADS, dtype=F32)))


def ret_prompt(h, qr, kr, norm_w):
    t = h.shape[0]
    tb = _tile(t, 512)
    c = CHUNK
    lg = _ret_log_gamma()
    idx = jnp.arange(c, dtype=F32)
    rel = idx[:, None] - idx[None, :]
    dmat = jnp.where(rel[None] >= 0, jnp.exp(jnp.maximum(rel, 0.0)[None] * lg[:, None, None]), 0.0)
    dmat = dmat.reshape(R_HEADS * c, c)
    cross = jnp.exp((idx + 1.0)[None, :] * lg[:, None]).reshape(R_HEADS * c, 1)
    cross = jnp.broadcast_to(cross, (R_HEADS * c, G_V))
    tail = jnp.exp((c - 1.0 - idx)[None, :] * lg[:, None])
    tail = jnp.repeat(tail.T, R_QK, axis=1)
    gc = jnp.broadcast_to(jnp.repeat(jnp.exp(c * lg), R_QK)[:, None], (HK, G_V))
    _, _, hm = _gla_constants()
    full = lambda a: pl.BlockSpec(a.shape, lambda i: (0,) * a.ndim)
    o, s = pl.pallas_call(
        functools.partial(_ret_kernel, n_chunks=tb // c),
        grid=(t // tb,),
        in_specs=[pl.BlockSpec((tb, HK), lambda i: (i, 0)),
                  pl.BlockSpec((tb, HK), lambda i: (i, 0)),
                  pl.BlockSpec((tb, HV), lambda i: (i, VR // 4)),
                  pl.BlockSpec((tb, HV), lambda i: (i, GR // 4)),
                  full(norm_w), full(dmat), full(cross), full(tail), full(gc), full(hm)],
        out_specs=[pl.BlockSpec((tb, HV), lambda i: (i, 0)),
                   pl.BlockSpec((HK, G_V), lambda i: (0, 0))],
        out_shape=[jax.ShapeDtypeStruct((t, HV), F32), jax.ShapeDtypeStruct((HK, G_V), F32)],
        scratch_shapes=[pltpu.VMEM((HK, G_V), F32)],
        compiler_params=_cparams(("arbitrary",)))(qr, kr, h, h, norm_w, dmat, cross, tail, gc, hm)
    return o, s.reshape(R_HEADS, R_QK, G_V)


def _step_kernel(s_ref, q_ref, k_ref, d_ref, v_ref, gate_ref, nw_ref, s_out_ref, o_ref, *, is_gla, q_scale):
    d = d_ref[...]
    if is_gla:
        d = _log_sigmoid(d) / G_NORMALIZER
    s_new = jnp.exp(d) * s_ref[...] + k_ref[...] * v_ref[...]
    s_out_ref[...] = s_new
    o = jnp.sum((q_ref[...] * q_scale) * s_new, axis=1, keepdims=True)
    o_ref[...] = _rms(o, nw_ref[...]) * _silu(gate_ref[...])


def recurrent_step(state, q, k, dlog, v, gate, norm_w, is_gla, q_scale):
    b = state.shape[0]
    col = lambda a: a.reshape(b, G_HEADS, G_QK, 1)
    rowv = lambda a: a.reshape(b, G_HEADS, 1, G_V)
    cspec = pl.BlockSpec((None, G_HEADS, G_QK, 1), lambda i: (i, 0, 0, 0))
    rspec = pl.BlockSpec((None, G_HEADS, 1, G_V), lambda i: (i, 0, 0, 0))
    sspec = pl.BlockSpec((None, G_HEADS, G_QK, G_V), lambda i: (i, 0, 0, 0))
    s_new, o = pl.pallas_call(
        functools.partial(_step_kernel, is_gla=is_gla, q_scale=q_scale),
        grid=(b,),
        in_specs=[sspec, cspec, cspec, cspec, rspec, rspec, pl.BlockSpec((1, G_V), lambda i: (0, 0))],
        out_specs=[sspec, rspec],
        out_shape=[jax.ShapeDtypeStruct(state.shape, F32), jax.ShapeDtypeStruct((b, G_HEADS, 1, G_V), F32)],
        compiler_params=_cparams(("parallel",)))(state, col(q), col(k), col(dlog), rowv(v), rowv(gate), norm_w)
    return o.reshape(b, HV), s_new


def _mixer_out_kernel(oa_ref, og_ref, or_ref, za_ref, zb_ref, zc_ref, x_ref, wpa_ref, wpb_ref, wpc_ref,
                      wo_ref, g_ref, b_ref, y_ref, *, alpha):
    def branch(o_ref, z_ref, w_ref):
        return _sigmoid(z_ref[...]) * jnp.dot(o_ref[...].astype(BF16), w_ref[...], preferred_element_type=F32)

    merged = branch(oa_ref, za_ref, wpa_ref) + branch(og_ref, zb_ref, wpb_ref) + branch(or_ref, zc_ref, wpc_ref)
    y = alpha * x_ref[...] + jnp.dot(merged.astype(BF16), wo_ref[...], preferred_element_type=F32)
    y_ref[...] = _layer_norm(y, g_ref[...], b_ref[...])


def mixer_out(oa, og, orr, h, x, wpa, wpb, wpc, wo, g, b, alpha):
    m = x.shape[0]
    tm = _tile(m, 512)
    bw = oa.shape[1]
    ospec = pl.BlockSpec((tm, bw), lambda i: (i, 0))
    zspec = lambda blk: pl.BlockSpec((tm, D_MODEL), lambda i: (i, blk // 8))
    xspec = pl.BlockSpec((tm, D_MODEL), lambda i: (i, 0))
    full = lambda a: pl.BlockSpec(a.shape, lambda i: (0,) * a.ndim)
    return pl.pallas_call(
        functools.partial(_mixer_out_kernel, alpha=alpha),
        grid=(m // tm,),
        in_specs=[ospec, ospec, ospec, zspec(ZA), zspec(ZB), zspec(ZC), xspec,
                  full(wpa), full(wpb), full(wpc), full(wo), full(g), full(b)],
        out_specs=xspec,
        out_shape=jax.ShapeDtypeStruct((m, D_MODEL), F32),
        compiler_params=_cparams(("parallel",)))(oa, og, orr, h, h, h, x, wpa, wpb, wpc, wo, g, b)


def _ffn_up_kernel(x_ref, wg_ref, wu_ref, h_ref):
    x = x_ref[...].astype(BF16)
    a = jnp.dot(x, wg_ref[...], preferred_element_type=F32)
    u = jnp.dot(x, wu_ref[...], preferred_element_type=F32)
    h_ref[...] = (_silu(a) * u).astype(h_ref.dtype)


def _ffn_down_kernel(h_ref, wd_ref, x_ref, g_ref, b_ref, y_ref, *, alpha):
    y = alpha * x_ref[...] + jnp.dot(h_ref[...], wd_ref[...], preferred_element_type=F32)
    y_ref[...] = _layer_norm(y, g_ref[...], b_ref[...])


def dense_ffn(x, wg, wu, wd, g, b, alpha):
    m = x.shape[0]
    f = wg.shape[1]
    tm, tf = _tile(m, 512), _tile(f, 1408)
    hmid = pl.pallas_call(
        _ffn_up_kernel, grid=(m // tm, f // tf),
        in_specs=[pl.BlockSpec((tm, D_MODEL), lambda i, j: (i, 0)),
                  pl.BlockSpec((D_MODEL, tf), lambda i, j: (0, j)),
                  pl.BlockSpec((D_MODEL, tf), lambda i, j: (0, j))],
        out_specs=pl.BlockSpec((tm, tf), lambda i, j: (i, j)),
        out_shape=jax.ShapeDtypeStruct((m, f), BF16),
        compiler_params=_cparams(("parallel", "parallel")))(x, wg, wu)
    full = lambda a: pl.BlockSpec(a.shape, lambda i: (0,) * a.ndim)
    xspec = pl.BlockSpec((tm, D_MODEL), lambda i: (i, 0))
    return pl.pallas_call(
        functools.partial(_ffn_down_kernel, alpha=alpha), grid=(m // tm,),
        in_specs=[pl.BlockSpec((tm, f), lambda i: (i, 0)), full(wd), xspec, full(g), full(b)],
        out_specs=xspec,
        out_shape=jax.ShapeDtypeStruct((m, D_MODEL), F32),
        compiler_params=_cparams(("parallel",)))(hmid, wd, x, g, b)


def _top2_weights(x, wh, wl):
    xh = x.astype(BF16)
    xl = (x - xh.astype(F32)).astype(BF16)
    logits = (jnp.dot(xh, wh, preferred_element_type=F32) + jnp.dot(xh, wl, preferred_element_type=F32)
              + jnp.dot(xl, wh, preferred_element_type=F32))
    lane = lax.broadcasted_iota(jnp.int32, logits.shape, 1)
    logits = jnp.where(lane < N_EXPERTS, logits, NEG_BIG)
    m1 = jnp.max(logits, axis=-1, keepdims=True)
    i1 = jnp.min(jnp.where(logits == m1, lane, LANES), axis=-1, keepdims=True)
    rest = jnp.where(lane == i1, NEG_BIG, logits)
    m2 = jnp.max(rest, axis=-1, keepdims=True)
    i2 = jnp.min(jnp.where(rest == m2, lane, LANES), axis=-1, keepdims=True)
    e2 = jnp.exp(m2 - m1)
    w1 = 1.0 / (1.0 + e2)
    w2 = e2 / (1.0 + e2)
    return jnp.where(lane == i1, w1, jnp.where(lane == i2, w2, 0.0))


def _router_kernel(x_ref, wh_ref, wl_ref, comb_ref):
    comb_ref[...] = _top2_weights(x_ref[...], wh_ref[...], wl_ref[...])


def _router_rank_kernel(x_ref, wh_ref, wl_ref, comb_ref, rankc_ref, rankr_ref, cnt_ref):
    comb = _top2_weights(x_ref[...], wh_ref[...], wl_ref[...])
    tb = comb.shape[0]
    routed = comb > 0.0
    ones = jnp.where(routed, 1.0, 0.0)
    earlier = (lax.broadcasted_iota(jnp.int32, (tb, tb), 0) > lax.broadcasted_iota(jnp.int32, (tb, tb), 1))
    rank = jnp.dot(jnp.where(earlier, 1.0, 0.0).astype(BF16), ones.astype(BF16), preferred_element_type=F32)
    rankc = jnp.where(routed, rank, -1.0)
    comb_ref[...] = comb
    rankc_ref[...] = rankc
    cnt_ref[...] = jnp.broadcast_to(jnp.sum(ones, axis=0, keepdims=True), cnt_ref.shape)
    eye = (lax.broadcasted_iota(jnp.int32, (LANES, LANES), 0) == lax.broadcasted_iota(jnp.int32, (LANES, LANES), 1))
    eye = jnp.where(eye, 1.0, 0.0).astype(BF16)
    rt = jnp.zeros((LANES, tb), F32)
    for part in _split3(rankc):
        rt = rt + lax.dot_general(eye, part, (((1,), (1,)), ((), ())), preferred_element_type=F32)
    rankr_ref[...] = rt[:N_EXPERTS]


def _router_weights(w_router):
    wpad = jnp.zeros((D_MODEL, LANES), F32).at[:, :N_EXPERTS].set(w_router)
    wh = wpad.astype(BF16)
    return wh, (wpad - wh.astype(F32)).astype(BF16)


def router(x, w_router):
    m = x.shape[0]
    tm = _tile(m, 512)
    wh, wl = _router_weights(w_router)
    full = lambda a: pl.BlockSpec(a.shape, lambda i: (0,) * a.ndim)
    return pl.pallas_call(
        _router_kernel, grid=(m // tm,),
        in_specs=[pl.BlockSpec((tm, D_MODEL), lambda i: (i, 0)), full(wh), full(wl)],
        out_specs=pl.BlockSpec((tm, LANES), lambda i: (i, 0)),
        out_shape=jax.ShapeDtypeStruct((m, LANES), F32),
        compiler_params=_cparams(("parallel",)))(x, wh, wl)


def router_ranked(x, w_router, tb):
    m = x.shape[0]
    nb = m // tb
    wh, wl = _router_weights(w_router)
    full = lambda a: pl.BlockSpec(a.shape, lambda i: (0,) * a.ndim)
    tok = pl.BlockSpec((tb, LANES), lambda i: (i, 0))
    return pl.pallas_call(
        _router_rank_kernel, grid=(nb,),
        in_specs=[pl.BlockSpec((tb, D_MODEL), lambda i: (i, 0)), full(wh), full(wl)],
        out_specs=[tok, tok, pl.BlockSpec((None, N_EXPERTS, tb), lambda i: (i, 0, 0)),
                   pl.BlockSpec((None, N_EXPERTS, LANES), lambda i: (i, 0, 0))],
        out_shape=[jax.ShapeDtypeStruct((m, LANES), F32), jax.ShapeDtypeStruct((m, LANES), F32),
                   jax.ShapeDtypeStruct((nb, N_EXPERTS, tb), F32),
                   jax.ShapeDtypeStruct((nb, N_EXPERTS, LANES), F32)],
        compiler_params=_cparams(("parallel",)))(x, wh, wl)


def _moe_kernel(x_ref, comb_ref, wg_ref, wu_ref, wd_ref, g_ref, b_ref, y_ref, acc_ref, *, alpha):
    e, f = pl.program_id(1), pl.program_id(2)

    @pl.when((e == 0) & (f == 0))
    def _():
        acc_ref[...] = jnp.zeros(acc_ref.shape, F32)

    x = x_ref[...].astype(BF16)
    a = jnp.dot(x, wg_ref[...], preferred_element_type=F32)
    u = jnp.dot(x, wu_ref[...], preferred_element_type=F32)
    hmid = (_silu(a) * u).astype(BF16)
    comb = comb_ref[...]
    lane = lax.broadcasted_iota(jnp.int32, comb.shape, 1)
    ce = jnp.sum(jnp.where(lane == e, comb, 0.0), axis=-1, keepdims=True)
    acc_ref[...] += ce * jnp.dot(hmid, wd_ref[...], preferred_element_type=F32)

    @pl.when((e == pl.num_programs(1) - 1) & (f == pl.num_programs(2) - 1))
    def _():
        y_ref[...] = _layer_norm(alpha * x_ref[...] + acc_ref[...], g_ref[...], b_ref[...])


def expert_tiles(w):
    e, d, f = w.shape
    tf = _tile(f, MOE_TF)
    return w.astype(BF16).reshape(e, d, f // tf, tf).transpose(0, 2, 1, 3)


def moe_ffn(x, comb, wg, wu, wd, g, b, alpha):
    m = x.shape[0]
    ne, nf, _, tf = wg.shape
    tm = _tile(m, 1024)
    xspec = pl.BlockSpec((tm, D_MODEL), lambda i, e, j: (i, 0))
    full = lambda a: pl.BlockSpec(a.shape, lambda i, e, j: (0,) * a.ndim)
    return pl.pallas_call(
        functools.partial(_moe_kernel, alpha=alpha), grid=(m // tm, ne, nf),
        in_specs=[xspec, pl.BlockSpec((tm, LANES), lambda i, e, j: (i, 0)),
                  pl.BlockSpec((None, None, D_MODEL, tf), lambda i, e, j: (e, j, 0, 0)),
                  pl.BlockSpec((None, None, D_MODEL, tf), lambda i, e, j: (e, j, 0, 0)),
                  pl.BlockSpec((None, tf, D_MODEL), lambda i, e, j: (e, j, 0)),
                  full(g), full(b)],
        out_specs=xspec,
        out_shape=jax.ShapeDtypeStruct((m, D_MODEL), F32),
        scratch_shapes=[pltpu.VMEM((tm, D_MODEL), F32)],
        compiler_params=_cparams(("parallel", "arbitrary", "arbitrary")))(x, comb, wg, wu, wd, g, b)


def _moe_routed_kernel(cnt_ref, x_ref, comb_ref, rankc_ref, rankr_ref, wg_ref, wu_ref, wd_ref, g_ref, b_ref,
                       y_ref, acc_ref, xb_ref, xc_ref, yacc_ref, *, alpha, rows, cap):
    blk, e, f = pl.program_id(0), pl.program_id(1), pl.program_id(2)
    last_f = pl.num_programs(2) - 1
    n = cnt_ref[blk * N_EXPERTS + e]
    tb = x_ref.shape[0]

    @pl.when((e == 0) & (f == 0))
    def _():
        acc_ref[...] = jnp.zeros(acc_ref.shape, F32)
        xb_ref[...] = x_ref[...].astype(BF16)

    def column(ref):
        a = ref[...]
        lane = lax.broadcasted_iota(jnp.int32, a.shape, 1)
        return jnp.sum(jnp.where(lane == e, a, 0.0), axis=-1, keepdims=True)

    def expert(xs):
        a = jnp.dot(xs, wg_ref[...], preferred_element_type=F32)
        u = jnp.dot(xs, wu_ref[...], preferred_element_type=F32)
        return jnp.dot((_silu(a) * u).astype(BF16), wd_ref[...], preferred_element_type=F32)

    @pl.when((n > 0) & (n <= rows))
    def _():
        @pl.when(f == 0)
        def _():
            rr = rankr_ref[...]
            sub = lax.broadcasted_iota(jnp.int32, rr.shape, 0)
            rrow = jnp.sum(jnp.where(sub == e, rr, 0.0), axis=0, keepdims=True)
            slot = lax.broadcasted_iota(jnp.int32, (rows, tb), 0).astype(F32)
            gather = jnp.where(slot == rrow, 1.0, 0.0).astype(BF16)
            xc_ref[...] = jnp.dot(gather, xb_ref[...], preferred_element_type=F32).astype(BF16)
            yacc_ref[0:cap, :] = jnp.zeros((cap, D_MODEL), F32)

        yacc_ref[0:rows, :] += expert(xc_ref[...])

        @pl.when(f == last_f)
        def _():
            slot = lax.broadcasted_iota(jnp.int32, (tb, cap), 1).astype(F32)
            scatter = jnp.where(slot == column(rankc_ref), 1.0, 0.0).astype(BF16)
            acc_ref[...] += column(comb_ref) * jnp.dot(scatter, yacc_ref[0:cap, :].astype(BF16),
                                                       preferred_element_type=F32)

    @pl.when(n > rows)
    def _():
        @pl.when(f == 0)
        def _():
            yacc_ref[...] = jnp.zeros(yacc_ref.shape, F32)

        yacc_ref[...] += expert(xb_ref[...])

        @pl.when(f == last_f)
        def _():
            acc_ref[...] += column(comb_ref) * yacc_ref[...]

    @pl.when((e == pl.num_programs(1) - 1) & (f == last_f))
    def _():
        y_ref[...] = _layer_norm(alpha * x_ref[...] + acc_ref[...], g_ref[...], b_ref[...])


def moe_routed_ffn(x, w_router, wg, wu, wd, g, b, alpha):
    m = x.shape[0]
    ne, nf, _, tf = wg.shape
    tb = _tile(m, MOE_BLOCK)
    rows = min(tb, MOE_ROWS)
    cap = min(tb, -(-rows // LANES) * LANES)
    comb, rankc, rankr, cnt = router_ranked(x, w_router, tb)
    counts = cnt[:, 0, :N_EXPERTS].astype(jnp.int32).reshape(-1)
    xspec = pl.BlockSpec((tb, D_MODEL), lambda i, e, j, c: (i, 0))
    tok = pl.BlockSpec((tb, LANES), lambda i, e, j, c: (i, 0))
    full = lambda a: pl.BlockSpec(a.shape, lambda i, e, j, c: (0,) * a.ndim)
    grid_spec = pltpu.PrefetchScalarGridSpec(
        num_scalar_prefetch=1, grid=(m // tb, ne, nf),
        in_specs=[xspec, tok, tok, pl.BlockSpec((None, N_EXPERTS, tb), lambda i, e, j, c: (i, 0, 0)),
                  pl.BlockSpec((None, None, D_MODEL, tf), lambda i, e, j, c: (e, j, 0, 0)),
                  pl.BlockSpec((None, None, D_MODEL, tf), lambda i, e, j, c: (e, j, 0, 0)),
                  pl.BlockSpec((None, tf, D_MODEL), lambda i, e, j, c: (e, j, 0)),
                  full(g), full(b)],
        out_specs=xspec,
        scratch_shapes=[pltpu.VMEM((tb, D_MODEL), F32), pltpu.VMEM((tb, D_MODEL), BF16),
                        pltpu.VMEM((rows, D_MODEL), BF16), pltpu.VMEM((tb, D_MODEL), F32)])
    return pl.pallas_call(
        functools.partial(_moe_routed_kernel, alpha=alpha, rows=rows, cap=cap),
        grid_spec=grid_spec,
        out_shape=jax.ShapeDtypeStruct((m, D_MODEL), F32),
        compiler_params=_cparams(("parallel", "arbitrary", "arbitrary")))(
            counts, x, comb, rankc, rankr, wg, wu, wd, g, b)


def _layer_weights(l, w_in, lam_q1, lam_k1, lam_q2, lam_k2, subln_w, w_gla_up, b_gla_up, gla_norm_w,
                   ret_norm_w, w_pa, w_pb, w_pc, w_out, ln1_g, ln1_b, ln2_g, ln2_b):
    wl = w_in[l]
    w_main = jnp.concatenate([wl[:, a:b] for a, b in PACK_ORDER], axis=1).astype(BF16)
    w_lrg = jnp.zeros((D_MODEL, LANES), F32).at[:, :G_RANK].set(wl[:, LRG_OFF:LRG_OFF + G_RANK]).astype(BF16)
    w_up = jnp.zeros((LANES, HK), F32).at[:G_RANK].set(w_gla_up[l]).astype(BF16)
    lamvec = jnp.zeros((8, LANES), F32)
    for r, vec in enumerate((lam_q1, lam_k1, lam_q2, lam_k2)):
        lamvec = lamvec.at[r, :A_QK].set(vec[l].astype(F32))
    row = lambda a: a[l].reshape(1, -1)
    return dict(w_main=w_main, w_lrg=w_lrg, w_up=w_up, b_up=row(b_gla_up), lamvec=lamvec,
                subln=row(subln_w), gla_nw=row(gla_norm_w), ret_nw=row(ret_norm_w),
                wpa=w_pa[l].astype(BF16), wpb=w_pb[l].astype(BF16), wpc=w_pc[l].astype(BF16),
                wo=w_out[l].astype(BF16), ln1_g=row(ln1_g), ln1_b=row(ln1_b),
                ln2_g=row(ln2_g), ln2_b=row(ln2_b), lam_init=0.8 - 0.6 * math.exp(-0.3 * l))


def _project(x, pos, lw):
    h = matmul(x, lw['w_main'], tn_cap=INPROJ_TN)
    lrg = matmul(x, lw['w_lrg'], tn_cap=LANES)
    qr, kr = rotary(h, QR, HK, _ret_tables(pos), 1, 1.0, R_QK ** -0.5, F32)
    return h, lrg, qr, kr


def _channel_mix(x1, l, lw, ffn_w, alpha):
    if l % 2 == 0:
        wg, wu, wd = ffn_w['dense'][l // 2]
        return dense_ffn(x1, wg, wu, wd, lw['ln2_g'], lw['ln2_b'], alpha)
    w_r, wg, wu, wd = ffn_w['moe'][l // 2]
    if x1.shape[0] >= MOE_MIN_ROUTED:
        return moe_routed_ffn(x1, w_r, wg, wu, wd, lw['ln2_g'], lw['ln2_b'], alpha)
    comb = router(x1, w_r)
    return moe_ffn(x1, comb, wg, wu, wd, lw['ln2_g'], lw['ln2_b'], alpha)


def kernel(x_prompt, x_sample, cache_k, cache_v, state_gla, state_ret, page_table, w_in, lam_q1, lam_k1, lam_q2, lam_k2, subln_w, w_gla_up, b_gla_up, gla_norm_w, ret_norm_w, w_pa, w_pb, w_pc, w_out, ln1_g, ln1_b, w_ff_gate, w_ff_up, w_ff_down, w_router, w_exp_gate, w_exp_up, w_exp_down, ln2_g, ln2_b):
    bp, tp, _ = x_prompt.shape
    bs, ts, _ = x_sample.shape
    assert bp == 1 and ts == 1
    depth = w_in.shape[0]
    alpha = (2 * depth) ** 0.25
    n_pool, page = cache_k.shape[1], cache_k.shape[2]
    past_len = page_table.shape[1] * page
    pos_p = jnp.arange(tp, dtype=jnp.int32)
    pos_s = jnp.full((bs,), past_len, jnp.int32)
    ffn_w = dict(
        dense=[(w_ff_gate[i].astype(BF16), w_ff_up[i].astype(BF16), w_ff_down[i].astype(BF16))
               for i in range(w_ff_gate.shape[0])],
        moe=[(w_router[i], expert_tiles(w_exp_gate[i]), expert_tiles(w_exp_up[i]), w_exp_down[i].astype(BF16))
             for i in range(w_router.shape[0])])
    ck = cache_k.reshape(depth, n_pool, page * A_HEADS, LANES)
    cv = cache_v.reshape(depth, n_pool, page * A_HEADS, LANES)
    lg_col = jnp.broadcast_to(jnp.repeat(_ret_log_gamma(), R_QK)[None, :], (bs, HK))

    yp = x_prompt.reshape(tp, D_MODEL)
    ys = x_sample.reshape(bs, D_MODEL)
    outs = {n: [] for n in ('gp', 'rp', 'ks', 'vs', 'gs', 'rs')}
    row_bufs = None
    cols = lambda a, blk, n: a[:, blk * LANES:(blk + n) * LANES]
    for l in range(depth):
        lw = _layer_weights(l, w_in, lam_q1, lam_k1, lam_q2, lam_k2, subln_w, w_gla_up, b_gla_up,
                            gla_norm_w, ret_norm_w, w_pa, w_pb, w_pc, w_out, ln1_g, ln1_b, ln2_g, ln2_b)
        h, lrg, qr, kr = _project(yp, pos_p, lw)
        qa, ka, va, *row_bufs = rotary_attention(h, _rope_tables(pos_p), ROPE_DIM // 2,
                                                 A_QK ** -0.5 * LOG2E, l, depth, row_bufs)
        oa = flash_diff_attention(qa, ka, va, lw['lamvec'], lw['subln'], lw['lam_init'])
        og, s_gla = gla_prompt(h, lrg, lw['w_up'], lw['b_up'], lw['gla_nw'])
        orr, s_ret = ret_prompt(h, qr, kr, lw['ret_nw'])
        x1 = mixer_out(oa, og, orr, h, yp, lw['wpa'], lw['wpb'], lw['wpc'], lw['wo'],
                       lw['ln1_g'], lw['ln1_b'], alpha)
        yp = _channel_mix(x1, l, lw, ffn_w, alpha)
        outs['gp'].append(s_gla[None])
        outs['rp'].append(s_ret[None])
        h, lrg, qr, kr = _project(ys, pos_s, lw)
        qa, ka = rotary(h, QA, A_HEADS * LANES, _rope_tables(pos_s), ROPE_DIM // 2, A_QK ** -0.5, 1.0, F32)
        va = cols(h, VA, 4)
        oa = decode_diff_attention(qa, ka, va, ck, cv, l, page_table, lw['lamvec'], lw['subln'],
                                   lw['lam_init'])
        gk = matmul(lrg, lw['w_up'], bias=lw['b_up'])
        og, s_gla = recurrent_step(state_gla[l], cols(h, QG, 2), cols(h, KG, 2), gk, cols(h, VG, 4),
                                   cols(h, RG, 4), lw['gla_nw'], True, G_QK ** -0.5)
        orr, s_ret = recurrent_step(state_ret[l], qr, kr, lg_col, cols(h, VR, 4), cols(h, GR, 4),
                                    lw['ret_nw'], False, 1.0)
        x1 = mixer_out(oa, og, orr, h, ys, lw['wpa'], lw['wpb'], lw['wpc'], lw['wo'],
                       lw['ln1_g'], lw['ln1_b'], alpha)
        ys = _channel_mix(x1, l, lw, ffn_w, alpha)
        outs['ks'].append(ka.reshape(bs, 1, A_HEADS, 2 * A_QK))
        outs['vs'].append(va.reshape(bs, 1, A_HEADS, 2 * A_QK))
        outs['gs'].append(s_gla)
        outs['rs'].append(s_ret)

    st = lambda n: jnp.stack(outs[n])
    kv_shape = (depth, bp, tp, A_HEADS, 2 * A_QK)
    return (yp.reshape(bp, tp, D_MODEL), ys.reshape(bs, ts, D_MODEL), row_bufs[0].reshape(kv_shape),
            row_bufs[1].reshape(kv_shape), st('gp'),
            st('rp'), st('ks'), st('vs'), st('gs'), st('rs'))
```

```python
import functools
import math

import numpy as np
import jax
import jax.numpy as jnp
from jax import lax
from jax.experimental import pallas as pl
from jax.experimental.pallas import tpu as pltpu

F32 = jnp.float32
BF16 = jnp.bfloat16

D_MODEL = 1024
A_HEADS = 4
A_QK = 64
ROPE_DIM = 16
ROPE_THETA = 500000.0
G_HEADS = 4
G_QK = 64
G_V = 128
G_RANK = 16
G_NORMALIZER = 16.0
R_HEADS = 4
R_QK = 64
R_ANGLE_BASE = 10000.0
CHUNK = 64
N_EXPERTS = 8
NORM_EPS = 1e-5
NEG_BIG = -1e30
LOG2E = math.log2(math.e)
FLASH_T = 1024
FLASH_RC = 32
FLASH_KCOLS = 256
FLASH_PVROWS = 128
DECODE_PAGES = 16
SCAN_UNROLL = 4
MOE_BLOCK = 1024
MOE_ROWS = 320
MOE_MIN_ROUTED = 256
MOE_TF = 896
INPROJ_TN = 1536

LANES = 128
VMEM_LIMIT = 56 * 1024 * 1024

ZA, ZB, ZC, QA, KA, VA, VG, RG, VR, GR, QG, KG, QR, KR = 0, 8, 16, 24, 28, 32, 36, 40, 44, 48, 52, 54, 56, 58
PACK_ORDER = ((4624, 7696), (0, 1536), (2048, 2560), (2576, 3088), (3600, 4112), (4112, 4624),
              (1536, 2048), (3088, 3600))
LRG_OFF = 2560


def _cparams(sem):
    return pltpu.CompilerParams(dimension_semantics=sem, vmem_limit_bytes=VMEM_LIMIT)


def _tile(n, cap):
    c = min(n, cap)
    while n % c:
        c -= 1
    return c


def _layer_norm(y, g, b):
    mu = jnp.mean(y, axis=-1, keepdims=True)
    d = y - mu
    var = jnp.mean(d * d, axis=-1, keepdims=True)
    return d * lax.rsqrt(var + NORM_EPS) * g + b


def _rms(o, w):
    return o * lax.rsqrt(jnp.mean(o * o, axis=-1, keepdims=True) + NORM_EPS) * w


def _silu(x):
    return x * (1.0 / (1.0 + jnp.exp(-x)))


def _sigmoid(x):
    return 1.0 / (1.0 + jnp.exp(-x))


def _log_sigmoid(x):
    return jnp.minimum(x, 0.0) - jnp.log(1.0 + jnp.exp(-jnp.abs(x)))


def _split3(x):
    a = x.astype(BF16)
    r = x - a.astype(F32)
    b = r.astype(BF16)
    c = (r - b.astype(F32)).astype(BF16)
    return a, b, c


def _mm_kernel(x_ref, w_ref, o_ref):
    o_ref[...] = jnp.dot(x_ref[...].astype(BF16), w_ref[...],
                         preferred_element_type=F32).astype(o_ref.dtype)


def _mm_bias_kernel(x_ref, w_ref, b_ref, o_ref):
    o_ref[...] = (jnp.dot(x_ref[...].astype(BF16), w_ref[...],
                          preferred_element_type=F32) + b_ref[...]).astype(o_ref.dtype)


def matmul(x, w, bias=None, out_dtype=F32, tm_cap=1024, tn_cap=768):
    m, k = x.shape
    n = w.shape[1]
    tm, tn = _tile(m, tm_cap), _tile(n, tn_cap)
    in_specs = [pl.BlockSpec((tm, k), lambda i, j: (i, 0)),
                pl.BlockSpec((k, tn), lambda i, j: (0, j))]
    args = [x, w]
    kern = _mm_kernel
    if bias is not None:
        in_specs.append(pl.BlockSpec((1, tn), lambda i, j: (0, j)))
        args.append(bias)
        kern = _mm_bias_kernel
    return pl.pallas_call(
        kern, grid=(m // tm, n // tn), in_specs=in_specs,
        out_specs=pl.BlockSpec((tm, tn), lambda i, j: (i, j)),
        out_shape=jax.ShapeDtypeStruct((m, n), out_dtype),
        compiler_params=_cparams(("parallel", "parallel")))(*args)


def _rot_kernel(x_ref, c_ref, s1_ref, s2_ref, oq_ref, ok_ref, *, shift, q_scale, k_scale):
    c, s1, s2 = c_ref[...], s1_ref[...], s2_ref[...]
    nq = oq_ref.shape[1] // LANES
    nk = ok_ref.shape[1] // LANES
    for b in range(nq + nk):
        x = x_ref[:, b * LANES:(b + 1) * LANES]
        y = x * c + pltpu.roll(x, LANES - shift, 1) * s1 + pltpu.roll(x, shift, 1) * s2
        if b < nq:
            oq_ref[:, b * LANES:(b + 1) * LANES] = (y * q_scale).astype(oq_ref.dtype)
        else:
            ok_ref[:, (b - nq) * LANES:(b - nq + 1) * LANES] = (y * k_scale).astype(ok_ref.dtype)


def rotary(h, col_blk, width, tables, shift, q_scale, k_scale, q_dtype):
    m = h.shape[0]
    tm = _tile(m, 512)
    blk = col_blk * LANES // (2 * width)
    tspec = pl.BlockSpec((tm, LANES), lambda i: (i, 0))
    return pl.pallas_call(
        functools.partial(_rot_kernel, shift=shift, q_scale=q_scale, k_scale=k_scale),
        grid=(m // tm,),
        in_specs=[pl.BlockSpec((tm, 2 * width), lambda i: (i, blk)), tspec, tspec, tspec],
        out_specs=[pl.BlockSpec((tm, width), lambda i: (i, 0)),
                   pl.BlockSpec((tm, width), lambda i: (i, 0))],
        out_shape=[jax.ShapeDtypeStruct((m, width), q_dtype),
                   jax.ShapeDtypeStruct((m, width), F32)],
        compiler_params=_cparams(("parallel",)))(h, *tables)


def _rot_att_kernel(x_ref, v_ref, c_ref, s1_ref, s2_ref, *rest, shift, q_scale):
    oq_ref, ok_ref, ov_ref, okr_ref, ovr_ref = rest[-5:]
    c, s1, s2 = c_ref[...], s1_ref[...], s2_ref[...]
    tm = x_ref.shape[0]
    for b in range(2 * A_HEADS):
        x = x_ref[:, b * LANES:(b + 1) * LANES]
        y = x * c + pltpu.roll(x, LANES - shift, 1) * s1 + pltpu.roll(x, shift, 1) * s2
        if b < A_HEADS:
            oq_ref[:, b * LANES:(b + 1) * LANES] = (y * q_scale).astype(oq_ref.dtype)
        else:
            hh = b - A_HEADS
            ok_ref[:, hh * LANES:(hh + 1) * LANES] = y.astype(ok_ref.dtype)
            okr_ref[pl.ds(hh, tm, stride=A_HEADS), :] = y
    for hh in range(A_HEADS):
        v = v_ref[:, hh * LANES:(hh + 1) * LANES]
        ov_ref[:, hh * LANES:(hh + 1) * LANES] = v.astype(ov_ref.dtype)
        ovr_ref[pl.ds(hh, tm, stride=A_HEADS), :] = v


def rotary_attention(h, tables, shift, q_scale, layer, depth, row_bufs=None):
    m = h.shape[0]
    tm = _tile(m, 512)
    nblk = m // tm
    width = A_HEADS * LANES
    tspec = pl.BlockSpec((tm, LANES), lambda i: (i, 0))
    wide = pl.BlockSpec((tm, width), lambda i: (i, 0))
    rowsp = pl.BlockSpec((tm * A_HEADS, LANES), lambda i: (layer * nblk + i, 0))
    in_specs = [pl.BlockSpec((tm, 2 * width), lambda i: (i, QA // 8)),
                pl.BlockSpec((tm, width), lambda i: (i, VA // 4)), tspec, tspec, tspec]
    args = [h, h, *tables]
    aliases = {}
    if row_bufs is not None:
        in_specs += [pl.BlockSpec(memory_space=pl.ANY)] * 2
        aliases = {len(args): 3, len(args) + 1: 4}
        args += list(row_bufs)
    return pl.pallas_call(
        functools.partial(_rot_att_kernel, shift=shift, q_scale=q_scale),
        grid=(nblk,),
        in_specs=in_specs,
        out_specs=[wide, wide, wide, rowsp, rowsp],
        out_shape=[jax.ShapeDtypeStruct((m, width), BF16)] * 3
                  + [jax.ShapeDtypeStruct((depth * m * A_HEADS, LANES), F32)] * 2,
        input_output_aliases=aliases,
        compiler_params=_cparams(("parallel",)))(*args)


def _rope_tables(pos):
    half = ROPE_DIM // 2
    inv = ROPE_THETA ** (-jnp.arange(half, dtype=F32) * 2.0 / ROPE_DIM)
    ang = pos.astype(F32)[:, None] * inv[None, :]
    cos, sin = jnp.cos(ang), jnp.sin(ang)
    lane = np.arange(LANES) % A_QK
    fi = lane % half
    cos_l, sin_l = cos[:, fi], sin[:, fi]
    in_rot = (lane < ROPE_DIM)[None, :]
    lo = (lane < half)[None, :]
    c = jnp.where(in_rot, cos_l, 1.0)
    s1 = jnp.where(lo, -sin_l, 0.0)
    s2 = jnp.where(in_rot & ~lo, sin_l, 0.0)
    return c, s1, s2


def _ret_tables(pos):
    inv = 1.0 / (R_ANGLE_BASE ** jnp.linspace(0.0, 1.0, R_QK // 2, dtype=F32))
    ang = pos.astype(F32)[:, None] * inv[None, :]
    cos, sin = jnp.cos(ang), jnp.sin(ang)
    lane = np.arange(LANES) % R_QK
    cos_l, sin_l = cos[:, lane // 2], sin[:, lane // 2]
    even = (lane % 2 == 0)[None, :]
    return cos_l, jnp.where(even, -sin_l, 0.0), jnp.where(even, 0.0, sin_l)


def _lam_from(lam_ref, lam_init):
    v = lam_ref[...]
    t1 = jnp.sum(v[0:1] * v[1:2], axis=-1, keepdims=True)
    t2 = jnp.sum(v[2:3] * v[3:4], axis=-1, keepdims=True)
    return jnp.exp(t1) - jnp.exp(t2) + lam_init


def _flash_kernel(qi_ref, kj_ref, lam_ref, q_ref, k_ref, v_ref, w_ref, o_ref, m_ref, acc_ref, sa_ref, sb_ref,
                  p_ref, *, t, rc, lam_init):
    step = pl.program_id(1)
    i, j = qi_ref[step], kj_ref[step]
    nch = t // LANES

    kw = min(t, FLASH_KCOLS)
    pvr = min(t, FLASH_PVROWS)

    def produce_items(s_dst):
        items = []
        for c in range(2):
            for kc in range(t // kw):
                def item(c=c, kc=kc):
                    q = q_ref[...]
                    lane = lax.broadcasted_iota(jnp.int32, q.shape, 1)
                    qc = jnp.where((lane < A_QK) if c == 0 else (lane >= A_QK), q, jnp.zeros_like(q))
                    kk = k_ref[kc * kw:(kc + 1) * kw, :].astype(BF16)
                    s_dst[c, :, kc * kw:(kc + 1) * kw] = lax.dot_general(
                        qc, kk, (((1,), (1,)), ((), ())), preferred_element_type=F32)
                items.append(item)
        return items

    def consume_items(s_src, masked):
        items = []
        for g in range(t // pvr):
            def sweep(g=g):
                for r in range(g * pvr // rc, (g + 1) * pvr // rc):
                    rows = slice(r * rc, (r + 1) * rc)
                    if masked:
                        keep = (lax.broadcasted_iota(jnp.int32, (rc, t), 1)
                                <= r * rc + lax.broadcasted_iota(jnp.int32, (rc, t), 0))
                    for c in range(2):
                        s = s_src[c, rows, :]
                        if masked:
                            s = jnp.where(keep, s, NEG_BIG)
                        m_prev = m_ref[c, rows, :]
                        m_new = jnp.maximum(m_prev, jnp.max(s, axis=-1, keepdims=True))
                        p = jnp.exp2(s - jnp.concatenate([m_new] * nch, axis=1))
                        alpha = jnp.exp2(m_prev - m_new)
                        acc_ref[c, rows, :] = jnp.concatenate([alpha, alpha], axis=1) * acc_ref[c, rows, :]
                        m_ref[c, rows, :] = m_new
                        p_ref[c, rows, :] = p.astype(BF16)

            def pv(g=g):
                v = jnp.concatenate([v_ref[...].astype(BF16), jnp.ones((t, LANES), BF16)], axis=1)
                rows = slice(g * pvr, (g + 1) * pvr)
                for c in range(2):
                    acc_ref[c, rows, :] += jnp.dot(p_ref[c, rows, :], v, preferred_element_type=F32)
            items += [sweep, pv]
        return items

    def run(*item_lists):
        longest = max(len(l) for l in item_lists)
        for n in range(longest):
            for l in item_lists:
                lo, hi = n * len(l) // longest, (n + 1) * len(l) // longest
                for it in l[lo:hi]:
                    it()

    def produce(s_dst):
        run(produce_items(s_dst))

    def consume(s_src, masked):
        run(consume_items(s_src, masked))

    even = j % 2 == 0

    @pl.when(j == 0)
    def _():
        m_ref[...] = jnp.full(m_ref.shape, NEG_BIG, F32)
        acc_ref[...] = jnp.zeros(acc_ref.shape, F32)
        produce(sa_ref)

    steady = (j >= 1) & (j <= i)

    @pl.when(steady & even)
    def _():
        run(consume_items(sb_ref, False), produce_items(sa_ref))

    @pl.when(steady & jnp.logical_not(even))
    def _():
        run(consume_items(sa_ref, False), produce_items(sb_ref))

    drain = j == i + 1
    pl.when(drain & even)(functools.partial(consume, sb_ref, True))
    pl.when(drain & jnp.logical_not(even))(functools.partial(consume, sa_ref, True))

    @pl.when(drain)
    def _():
        lam = _lam_from(lam_ref, lam_init)
        o = (acc_ref[0, :, :LANES] / acc_ref[0, :, LANES:]
             - lam * (acc_ref[1, :, :LANES] / acc_ref[1, :, LANES:]))
        o_ref[...] = _rms(o, w_ref[...]) * (1.0 - lam_init)


def flash_diff_attention(q, k, v, lamvec, subln_w, lam_init, t_cap=FLASH_T, rc=FLASH_RC):
    n = q.shape[0]
    t = _tile(n, t_cap)
    rc = _tile(t, rc)
    nb = n // t
    pairs = [(i, j) for i in range(nb) for j in range(i + 2)]
    qi = jnp.asarray([p[0] for p in pairs], jnp.int32)
    kj = jnp.asarray([p[1] for p in pairs], jnp.int32)
    grid_spec = pltpu.PrefetchScalarGridSpec(
        num_scalar_prefetch=2, grid=(A_HEADS, len(pairs)),
        in_specs=[pl.BlockSpec((8, LANES), lambda hh, s, qi, kj: (0, 0)),
                  pl.BlockSpec((t, LANES), lambda hh, s, qi, kj: (qi[s], hh)),
                  pl.BlockSpec((t, LANES), lambda hh, s, qi, kj: (jnp.minimum(kj[s], qi[s]), hh)),
                  pl.BlockSpec((t, LANES), lambda hh, s, qi, kj: (jnp.clip(kj[s] - 1, 0, qi[s]), hh)),
                  pl.BlockSpec((1, LANES), lambda hh, s, qi, kj: (0, 0))],
        out_specs=pl.BlockSpec((t, LANES), lambda hh, s, qi, kj: (qi[s], hh)),
        scratch_shapes=[pltpu.VMEM((2, t, LANES), F32), pltpu.VMEM((2, t, 2 * LANES), F32),
                        pltpu.VMEM((2, t, t), F32), pltpu.VMEM((2, t, t), F32),
                        pltpu.VMEM((2, t, t), BF16)])
    return pl.pallas_call(
        functools.partial(_flash_kernel, t=t, rc=rc, lam_init=lam_init),
        grid_spec=grid_spec,
        out_shape=jax.ShapeDtypeStruct((n, A_HEADS * LANES), F32),
        compiler_params=_cparams(("parallel", "arbitrary")))(qi, kj, lamvec, q, k, v, subln_w)


def _decode_kernel(pt_ref, lam_ref, q_ref, kn_ref, vn_ref, w_ref, *rest, pp, lam_init):
    k_refs, v_refs = rest[:pp], rest[pp:2 * pp]
    o_ref, m_ref, l_ref, acc_ref = rest[2 * pp:]
    c = pl.program_id(1)
    rows = 2 * A_HEADS
    prow = k_refs[0].shape[0]

    @pl.when(c == 0)
    def _():
        m_ref[...] = jnp.full(m_ref.shape, NEG_BIG, F32)
        l_ref[...] = jnp.zeros(l_ref.shape, F32)
        acc_ref[...] = jnp.zeros(acc_ref.shape, F32)

    rid = lax.broadcasted_iota(jnp.int32, (rows, LANES), 0)
    lid = lax.broadcasted_iota(jnp.int32, (rows, LANES), 1)
    qb = jnp.where(lid // A_QK == rid % 2, q_ref[...], 0.0).astype(BF16)
    s = jnp.concatenate(
        [lax.dot_general(qb, k_refs[p][...].astype(BF16), (((1,), (1,)), ((), ())),
                         preferred_element_type=F32) for p in range(pp)], axis=-1)
    srow = lax.broadcasted_iota(jnp.int32, s.shape, 0)
    scol = lax.broadcasted_iota(jnp.int32, s.shape, 1)
    s = jnp.where(scol % A_HEADS == srow // 2, s, NEG_BIG)
    m_prev = m_ref[...]
    m_new = jnp.maximum(m_prev, jnp.max(s, axis=-1, keepdims=True))
    p_ = jnp.exp(s - m_new)
    alpha = jnp.exp(m_prev - m_new)
    l_new = alpha * l_ref[...] + jnp.sum(p_, axis=-1, keepdims=True)
    acc = alpha * acc_ref[...]
    for p in range(pp):
        acc = acc + jnp.dot(p_[:, p * prow:(p + 1) * prow].astype(BF16), v_refs[p][...].astype(BF16),
                            preferred_element_type=F32)
    m_ref[...] = m_new
    l_ref[...] = l_new
    acc_ref[...] = acc

    @pl.when(c == pl.num_programs(1) - 1)
    def _():
        kn = kn_ref[...].astype(BF16).astype(F32)
        vn = vn_ref[...].astype(BF16).astype(F32)
        s_self = jnp.sum(qb.astype(F32) * kn, axis=-1, keepdims=True)
        m_fin = jnp.maximum(m_new, s_self)
        a2 = jnp.exp(m_new - m_fin)
        p_self = jnp.exp(s_self - m_fin)
        l_fin = a2 * l_new + p_self
        acc_fin = a2 * acc + p_self.astype(BF16).astype(F32) * vn
        on = acc_fin / l_fin
        lam = _lam_from(lam_ref, lam_init)
        w = w_ref[...]
        for hh in range(A_HEADS):
            o = on[2 * hh:2 * hh + 1] - lam * on[2 * hh + 1:2 * hh + 2]
            o_ref[hh:hh + 1, :] = _rms(o, w) * (1.0 - lam_init)


def decode_diff_attention(q, k_new, v_new, cache_k, cache_v, layer, page_table, lamvec, subln_w, lam_init):
    b, n_pages = page_table.shape
    prow = cache_k.shape[2]
    pp = _tile(n_pages, DECODE_PAGES)
    rows = 2 * A_HEADS

    def row(bb, c, pt):
        return (bb, 0, 0)

    def page_idx(p):
        return lambda bb, c, pt: (layer, pt[bb, c * pp + p], 0, 0)

    row_spec = pl.BlockSpec((None, rows, LANES), row)
    page_specs = [pl.BlockSpec((None, None, prow, LANES), page_idx(p)) for p in range(pp)]
    grid_spec = pltpu.PrefetchScalarGridSpec(
        num_scalar_prefetch=1, grid=(b, n_pages // pp),
        in_specs=[pl.BlockSpec((8, LANES), lambda bb, c, pt: (0, 0)), row_spec, row_spec, row_spec,
                  pl.BlockSpec((1, LANES), lambda bb, c, pt: (0, 0))] + page_specs + page_specs,
        out_specs=pl.BlockSpec((None, A_HEADS, LANES), row),
        scratch_shapes=[pltpu.VMEM((rows, 1), F32), pltpu.VMEM((rows, 1), F32),
                        pltpu.VMEM((rows, LANES), F32)])
    r8 = lambda a: jnp.repeat(a.reshape(b, A_HEADS, LANES), 2, axis=1)
    out = pl.pallas_call(
        functools.partial(_decode_kernel, pp=pp, lam_init=lam_init),
        grid_spec=grid_spec,
        out_shape=jax.ShapeDtypeStruct((b, A_HEADS, LANES), F32),
        compiler_params=_cparams(("parallel", "arbitrary")))(
            page_table, lamvec, r8(q), r8(k_new), r8(v_new), subln_w,
            *([cache_k] * pp), *([cache_v] * pp))
    return out.reshape(b, A_HEADS * LANES)


HK = G_HEADS * G_QK
HV = G_HEADS * G_V
GLA_LEVELS = (1, 2, 4, 8, 16, 32)


def _gla_constants():
    c = CHUNK
    t = np.arange(c)[:, None]
    i = np.arange(c)[None, :]
    def prefix(s):
        return (i // s == t // s) & (i <= t)

    def suffix(s):
        return (i // s == t // s) & (i > t)

    mats = [np.where(t % (2 * s) >= s, prefix(s), suffix(s)) for s in GLA_LEVELS]
    mats += [prefix(c), suffix(c)]
    tri = np.concatenate(mats, axis=0).astype(np.float32)
    j = i
    level = np.full((c, c), -1, np.int32)
    level[t == j] = 0
    for n, s in enumerate(GLA_LEVELS):
        sel = (t // (2 * s) == j // (2 * s)) & (t % (2 * s) >= s) & (j % (2 * s) < s)
        level[sel] = n + 1
    level = np.tile(level, (G_HEADS, 1))
    headmask = (np.arange(HK)[None, :] // G_QK == np.arange(G_HEADS * c)[:, None] // c).astype(np.float32)
    return jnp.asarray(tri, BF16), jnp.asarray(level), jnp.asarray(headmask)


def _stack_heads(x, hm):
    return jnp.concatenate([x] * G_HEADS, axis=0) * hm


def _col_bcast(row, width):
    n = row.shape[1]
    eye = lax.broadcasted_iota(jnp.int32, (n, n), 0) == lax.broadcasted_iota(jnp.int32, (n, n), 1)
    ones = jnp.ones((n, width), BF16)
    out = jnp.zeros((n, width), F32)
    for part in _split3(row):
        d = jnp.where(eye, jnp.broadcast_to(part.astype(F32), (n, n)), 0.0)
        out = out + jnp.dot(d.astype(BF16), ones, preferred_element_type=F32)
    return out


def _gla_kernel(q_ref, k_ref, v_ref, rg_ref, lr_ref, wup_ref, bup_ref, nw_ref, tri_ref, lvl_ref, hm_ref,
                o_ref, s_out_ref, s_ref, *, n_chunks):
    c = CHUNK

    @pl.when(pl.program_id(0) == 0)
    def _():
        s_ref[...] = jnp.zeros(s_ref.shape, F32)

    tri = tri_ref[...]
    lvl = lvl_ref[...]
    hm = hm_ref[...]
    nl = len(GLA_LEVELS)

    def chunk(ci, carry):
        r0 = pl.multiple_of(ci * c, c)
        rows = pl.ds(r0, c)
        q = q_ref[rows, :] * (G_QK ** -0.5)
        k = k_ref[rows, :]
        v = v_ref[rows, :].astype(BF16)
        gk = jnp.dot(lr_ref[rows, :].astype(BF16), wup_ref[...], preferred_element_type=F32) + bup_ref[...]
        g = _log_sigmoid(gk) / G_NORMALIZER
        ps = jnp.zeros(((nl + 2) * c, HK), F32)
        for part in _split3(g):
            ps = ps + jnp.dot(tri, part, preferred_element_type=F32)
        gcum, gsuf = ps[nl * c:(nl + 1) * c], ps[(nl + 1) * c:(nl + 2) * c]
        a = jnp.zeros((G_HEADS * c, c), F32)
        for n in range(nl + 1):
            if n == 0:
                qq, kk = q, k
            else:
                f = jnp.exp(ps[(n - 1) * c:n * c])
                qq, kk = q * f, k * f
            d = lax.dot_general(_stack_heads(qq, hm).astype(BF16), kk.astype(BF16),
                                (((1,), (1,)), ((), ())), preferred_element_type=F32)
            a = jnp.where(lvl == n, d, a)
        s_old = s_ref[...]
        inter = jnp.dot(_stack_heads(q * jnp.exp(gcum), hm).astype(BF16), s_old.astype(BF16),
                        preferred_element_type=F32)
        ab = a.astype(BF16)
        nw = nw_ref[...]
        kv = lax.dot_general((k * jnp.exp(gsuf)).astype(BF16), v, (((0,), (0,)), ((), ())),
                             preferred_element_type=F32)
        decay = jnp.exp(_col_bcast(gcum[c - 1:c, :], G_V))
        for hh in range(G_HEADS):
            vs = slice(hh * G_V, (hh + 1) * G_V)
            o = inter[hh * c:(hh + 1) * c] + jnp.dot(ab[hh * c:(hh + 1) * c], v[:, vs],
                                                     preferred_element_type=F32)
            o_ref[rows, vs] = _rms(o, nw) * _silu(rg_ref[rows, vs])
            ks = slice(hh * G_QK, (hh + 1) * G_QK)
            s_ref[ks, :] = decay[ks] * s_old[ks] + kv[ks, vs]
        return carry

    lax.fori_loop(0, n_chunks, chunk, 0, unroll=SCAN_UNROLL)

    @pl.when(pl.program_id(0) == pl.num_programs(0) - 1)
    def _():
        s_out_ref[...] = s_ref[...]


def gla_prompt(h, lrg, w_up, b_up, norm_w):
    t = h.shape[0]
    tb = _tile(t, 512)
    tri, lvl, hm = _gla_constants()
    full = lambda a: pl.BlockSpec(a.shape, lambda i: (0,) * a.ndim)
    o, s = pl.pallas_call(
        functools.partial(_gla_kernel, n_chunks=tb // CHUNK),
        grid=(t // tb,),
        in_specs=[pl.BlockSpec((tb, HK), lambda i: (i, QG // 2)),
                  pl.BlockSpec((tb, HK), lambda i: (i, KG // 2)),
                  pl.BlockSpec((tb, HV), lambda i: (i, VG // 4)),
                  pl.BlockSpec((tb, HV), lambda i: (i, RG // 4)),
                  pl.BlockSpec((tb, LANES), lambda i: (i, 0)),
                  full(w_up), full(b_up), full(norm_w), full(tri), full(lvl), full(hm)],
        out_specs=[pl.BlockSpec((tb, HV), lambda i: (i, 0)),
                   pl.BlockSpec((HK, G_V), lambda i: (0, 0))],
        out_shape=[jax.ShapeDtypeStruct((t, HV), F32), jax.ShapeDtypeStruct((HK, G_V), F32)],
        scratch_shapes=[pltpu.VMEM((HK, G_V), F32)],
        compiler_params=_cparams(("arbitrary",)))(h, h, h, h, lrg, w_up, b_up, norm_w, tri, lvl, hm)
    return o, s.reshape(G_HEADS, G_QK, G_V)


def _ret_kernel(q_ref, k_ref, v_ref, gr_ref, c_ref, s1_ref, s2_ref, nw_ref, dm_ref, cross_ref, tail_ref, gc_ref,
                hm_ref, o_ref, s_out_ref, s_ref, *, n_chunks):
    c = CHUNK

    def rotate(x, rows, scale):
        cc, s1, s2 = c_ref[rows, :], s1_ref[rows, :], s2_ref[rows, :]
        parts = []
        for b in range(HK // LANES):
            xb = x[:, b * LANES:(b + 1) * LANES]
            parts.append((xb * cc + pltpu.roll(xb, LANES - 1, 1) * s1 + pltpu.roll(xb, 1, 1) * s2) * scale)
        return jnp.concatenate(parts, axis=1)

    @pl.when(pl.program_id(0) == 0)
    def _():
        s_ref[...] = jnp.zeros(s_ref.shape, F32)

    hm = hm_ref[...]
    dm = dm_ref[...]
    cross = cross_ref[...]
    tail = tail_ref[...]
    gc = gc_ref[...]
    nw = nw_ref[...]

    def chunk(ci, carry):
        r0 = pl.multiple_of(ci * c, c)
        rows = pl.ds(r0, c)
        q = rotate(q_ref[rows, :], rows, 1.0)
        k = rotate(k_ref[rows, :], rows, R_QK ** -0.5)
        v = v_ref[rows, :].astype(BF16)
        qs = _stack_heads(q, hm).astype(BF16)
        a = lax.dot_general(qs, k.astype(BF16), (((1,), (1,)), ((), ())), preferred_element_type=F32) * dm
        s_old = s_ref[...]
        inter = jnp.dot(qs, s_old.astype(BF16), preferred_element_type=F32) * cross
        kv = lax.dot_general((k * tail).astype(BF16), v, (((0,), (0,)), ((), ())), preferred_element_type=F32)
        ab = a.astype(BF16)
        for hh in range(R_HEADS):
            vs = slice(hh * G_V, (hh + 1) * G_V)
            o = inter[hh * c:(hh + 1) * c] + jnp.dot(ab[hh * c:(hh + 1) * c], v[:, vs],
                                                     preferred_element_type=F32)
            o_ref[rows, vs] = _rms(o, nw) * _silu(gr_ref[rows, vs])
            ks = slice(hh * R_QK, (hh + 1) * R_QK)
            s_ref[ks, :] = gc[ks] * s_old[ks] + kv[ks, vs]
        return carry

    lax.fori_loop(0, n_chunks, chunk, 0, unroll=SCAN_UNROLL)

    @pl.when(pl.program_id(0) == pl.num_programs(0) - 1)
    def _():
        s_out_ref[...] = s_ref[...]


def _ret_log_gamma():
    return jnp.log(1.0 - jnp.exp2(-5.0 - jnp.arange(R_HEADS, dtype=F32)))


def ret_prompt(h, tables, norm_w):
    t = h.shape[0]
    tb = _tile(t, 512)
    tspec = pl.BlockSpec((tb, LANES), lambda i: (i, 0))
    c = CHUNK
    lg = _ret_log_gamma()
    idx = jnp.arange(c, dtype=F32)
    rel = idx[:, None] - idx[None, :]
    dmat = jnp.where(rel[None] >= 0, jnp.exp(jnp.maximum(rel, 0.0)[None] * lg[:, None, None]), 0.0)
    dmat = dmat.reshape(R_HEADS * c, c)
    cross = jnp.exp((idx + 1.0)[None, :] * lg[:, None]).reshape(R_HEADS * c, 1)
    cross = jnp.broadcast_to(cross, (R_HEADS * c, G_V))
    tail = jnp.exp((c - 1.0 - idx)[None, :] * lg[:, None])
    tail = jnp.repeat(tail.T, R_QK, axis=1)
    gc = jnp.broadcast_to(jnp.repeat(jnp.exp(c * lg), R_QK)[:, None], (HK, G_V))
    _, _, hm = _gla_constants()
    full = lambda a: pl.BlockSpec(a.shape, lambda i: (0,) * a.ndim)
    o, s = pl.pallas_call(
        functools.partial(_ret_kernel, n_chunks=tb // c),
        grid=(t // tb,),
        in_specs=[pl.BlockSpec((tb, HK), lambda i: (i, QR // 2)),
                  pl.BlockSpec((tb, HK), lambda i: (i, KR // 2)),
                  pl.BlockSpec((tb, HV), lambda i: (i, VR // 4)),
                  pl.BlockSpec((tb, HV), lambda i: (i, GR // 4)),
                  tspec, tspec, tspec,
                  full(norm_w), full(dmat), full(cross), full(tail), full(gc), full(hm)],
        out_specs=[pl.BlockSpec((tb, HV), lambda i: (i, 0)),
                   pl.BlockSpec((HK, G_V), lambda i: (0, 0))],
        out_shape=[jax.ShapeDtypeStruct((t, HV), F32), jax.ShapeDtypeStruct((HK, G_V), F32)],
        scratch_shapes=[pltpu.VMEM((HK, G_V), F32)],
        compiler_params=_cparams(("arbitrary",)))(h, h, h, h, *tables, norm_w, dmat, cross, tail, gc, hm)
    return o, s.reshape(R_HEADS, R_QK, G_V)


def _step_kernel(s_ref, q_ref, k_ref, d_ref, v_ref, gate_ref, nw_ref, s_out_ref, o_ref, *, is_gla, q_scale):
    d = d_ref[...]
    if is_gla:
        d = _log_sigmoid(d) / G_NORMALIZER
    s_new = jnp.exp(d) * s_ref[...] + k_ref[...] * v_ref[...]
    s_out_ref[...] = s_new
    o = jnp.sum((q_ref[...] * q_scale) * s_new, axis=1, keepdims=True)
    o_ref[...] = _rms(o, nw_ref[...]) * _silu(gate_ref[...])


def recurrent_step(state, q, k, dlog, v, gate, norm_w, is_gla, q_scale):
    b = state.shape[0]
    col = lambda a: a.reshape(b, G_HEADS, G_QK, 1)
    rowv = lambda a: a.reshape(b, G_HEADS, 1, G_V)
    cspec = pl.BlockSpec((None, G_HEADS, G_QK, 1), lambda i: (i, 0, 0, 0))
    rspec = pl.BlockSpec((None, G_HEADS, 1, G_V), lambda i: (i, 0, 0, 0))
    sspec = pl.BlockSpec((None, G_HEADS, G_QK, G_V), lambda i: (i, 0, 0, 0))
    s_new, o = pl.pallas_call(
        functools.partial(_step_kernel, is_gla=is_gla, q_scale=q_scale),
        grid=(b,),
        in_specs=[sspec, cspec, cspec, cspec, rspec, rspec, pl.BlockSpec((1, G_V), lambda i: (0, 0))],
        out_specs=[sspec, rspec],
        out_shape=[jax.ShapeDtypeStruct(state.shape, F32), jax.ShapeDtypeStruct((b, G_HEADS, 1, G_V), F32)],
        compiler_params=_cparams(("parallel",)))(state, col(q), col(k), col(dlog), rowv(v), rowv(gate), norm_w)
    return o.reshape(b, HV), s_new


def _mixer_out_kernel(oa_ref, og_ref, or_ref, za_ref, zb_ref, zc_ref, x_ref, wpa_ref, wpb_ref, wpc_ref,
                      wo_ref, g_ref, b_ref, y_ref, *, alpha):
    def branch(o_ref, z_ref, w_ref):
        return _sigmoid(z_ref[...]) * jnp.dot(o_ref[...].astype(BF16), w_ref[...], preferred_element_type=F32)

    merged = branch(oa_ref, za_ref, wpa_ref) + branch(og_ref, zb_ref, wpb_ref) + branch(or_ref, zc_ref, wpc_ref)
    y = alpha * x_ref[...] + jnp.dot(merged.astype(BF16), wo_ref[...], preferred_element_type=F32)
    y_ref[...] = _layer_norm(y, g_ref[...], b_ref[...])


def mixer_out(oa, og, orr, h, x, wpa, wpb, wpc, wo, g, b, alpha):
    m = x.shape[0]
    tm = _tile(m, 512)
    bw = oa.shape[1]
    ospec = pl.BlockSpec((tm, bw), lambda i: (i, 0))
    zspec = lambda blk: pl.BlockSpec((tm, D_MODEL), lambda i: (i, blk // 8))
    xspec = pl.BlockSpec((tm, D_MODEL), lambda i: (i, 0))
    full = lambda a: pl.BlockSpec(a.shape, lambda i: (0,) * a.ndim)
    return pl.pallas_call(
        functools.partial(_mixer_out_kernel, alpha=alpha),
        grid=(m // tm,),
        in_specs=[ospec, ospec, ospec, zspec(ZA), zspec(ZB), zspec(ZC), xspec,
                  full(wpa), full(wpb), full(wpc), full(wo), full(g), full(b)],
        out_specs=xspec,
        out_shape=jax.ShapeDtypeStruct((m, D_MODEL), F32),
        compiler_params=_cparams(("parallel",)))(oa, og, orr, h, h, h, x, wpa, wpb, wpc, wo, g, b)


def _ffn_up_kernel(x_ref, wg_ref, wu_ref, h_ref):
    x = x_ref[...].astype(BF16)
    a = jnp.dot(x, wg_ref[...], preferred_element_type=F32)
    u = jnp.dot(x, wu_ref[...], preferred_element_type=F32)
    h_ref[...] = (_silu(a) * u).astype(h_ref.dtype)


def _ffn_down_kernel(h_ref, wd_ref, x_ref, g_ref, b_ref, y_ref, *, alpha):
    y = alpha * x_ref[...] + jnp.dot(h_ref[...], wd_ref[...], preferred_element_type=F32)
    y_ref[...] = _layer_norm(y, g_ref[...], b_ref[...])


def dense_ffn(x, wg, wu, wd, g, b, alpha):
    m = x.shape[0]
    f = wg.shape[1]
    tm, tf = _tile(m, 512), _tile(f, 1408)
    hmid = pl.pallas_call(
        _ffn_up_kernel, grid=(m // tm, f // tf),
        in_specs=[pl.BlockSpec((tm, D_MODEL), lambda i, j: (i, 0)),
                  pl.BlockSpec((D_MODEL, tf), lambda i, j: (0, j)),
                  pl.BlockSpec((D_MODEL, tf), lambda i, j: (0, j))],
        out_specs=pl.BlockSpec((tm, tf), lambda i, j: (i, j)),
        out_shape=jax.ShapeDtypeStruct((m, f), BF16),
        compiler_params=_cparams(("parallel", "parallel")))(x, wg, wu)
    full = lambda a: pl.BlockSpec(a.shape, lambda i: (0,) * a.ndim)
    xspec = pl.BlockSpec((tm, D_MODEL), lambda i: (i, 0))
    return pl.pallas_call(
        functools.partial(_ffn_down_kernel, alpha=alpha), grid=(m // tm,),
        in_specs=[pl.BlockSpec((tm, f), lambda i: (i, 0)), full(wd), xspec, full(g), full(b)],
        out_specs=xspec,
        out_shape=jax.ShapeDtypeStruct((m, D_MODEL), F32),
        compiler_params=_cparams(("parallel",)))(hmid, wd, x, g, b)


def _top2_weights(x, wh, wl):
    xh = x.astype(BF16)
    xl = (x - xh.astype(F32)).astype(BF16)
    logits = (jnp.dot(xh, wh, preferred_element_type=F32) + jnp.dot(xh, wl, preferred_element_type=F32)
              + jnp.dot(xl, wh, preferred_element_type=F32))
    lane = lax.broadcasted_iota(jnp.int32, logits.shape, 1)
    logits = jnp.where(lane < N_EXPERTS, logits, NEG_BIG)
    m1 = jnp.max(logits, axis=-1, keepdims=True)
    i1 = jnp.min(jnp.where(logits == m1, lane, LANES), axis=-1, keepdims=True)
    rest = jnp.where(lane == i1, NEG_BIG, logits)
    m2 = jnp.max(rest, axis=-1, keepdims=True)
    i2 = jnp.min(jnp.where(rest == m2, lane, LANES), axis=-1, keepdims=True)
    e2 = jnp.exp(m2 - m1)
    w1 = 1.0 / (1.0 + e2)
    w2 = e2 / (1.0 + e2)
    return jnp.where(lane == i1, w1, jnp.where(lane == i2, w2, 0.0))


def _router_kernel(x_ref, wh_ref, wl_ref, comb_ref):
    comb_ref[...] = _top2_weights(x_ref[...], wh_ref[...], wl_ref[...])


def _router_rank_kernel(x_ref, wh_ref, wl_ref, comb_ref, rankc_ref, rankr_ref, cnt_ref):
    comb = _top2_weights(x_ref[...], wh_ref[...], wl_ref[...])
    tb = comb.shape[0]
    routed = comb > 0.0
    ones = jnp.where(routed, 1.0, 0.0)
    earlier = (lax.broadcasted_iota(jnp.int32, (tb, tb), 0) > lax.broadcasted_iota(jnp.int32, (tb, tb), 1))
    rank = jnp.dot(jnp.where(earlier, 1.0, 0.0).astype(BF16), ones.astype(BF16), preferred_element_type=F32)
    rankc = jnp.where(routed, rank, -1.0)
    comb_ref[...] = comb
    rankc_ref[...] = rankc
    cnt_ref[...] = jnp.broadcast_to(jnp.sum(ones, axis=0, keepdims=True), cnt_ref.shape)
    eye = (lax.broadcasted_iota(jnp.int32, (LANES, LANES), 0) == lax.broadcasted_iota(jnp.int32, (LANES, LANES), 1))
    eye = jnp.where(eye, 1.0, 0.0).astype(BF16)
    rt = jnp.zeros((LANES, tb), F32)
    for part in _split3(rankc):
        rt = rt + lax.dot_general(eye, part, (((1,), (1,)), ((), ())), preferred_element_type=F32)
    rankr_ref[...] = rt[:N_EXPERTS]


def _router_weights(w_router):
    wpad = jnp.zeros((D_MODEL, LANES), F32).at[:, :N_EXPERTS].set(w_router)
    wh = wpad.astype(BF16)
    return wh, (wpad - wh.astype(F32)).astype(BF16)


def router(x, w_router):
    m = x.shape[0]
    tm = _tile(m, 512)
    wh, wl = _router_weights(w_router)
    full = lambda a: pl.BlockSpec(a.shape, lambda i: (0,) * a.ndim)
    return pl.pallas_call(
        _router_kernel, grid=(m // tm,),
        in_specs=[pl.BlockSpec((tm, D_MODEL), lambda i: (i, 0)), full(wh), full(wl)],
        out_specs=pl.BlockSpec((tm, LANES), lambda i: (i, 0)),
        out_shape=jax.ShapeDtypeStruct((m, LANES), F32),
        compiler_params=_cparams(("parallel",)))(x, wh, wl)


def router_ranked(x, w_router, tb):
    m = x.shape[0]
    nb = m // tb
    wh, wl = _router_weights(w_router)
    full = lambda a: pl.BlockSpec(a.shape, lambda i: (0,) * a.ndim)
    tok = pl.BlockSpec((tb, LANES), lambda i: (i, 0))
    return pl.pallas_call(
        _router_rank_kernel, grid=(nb,),
        in_specs=[pl.BlockSpec((tb, D_MODEL), lambda i: (i, 0)), full(wh), full(wl)],
        out_specs=[tok, tok, pl.BlockSpec((None, N_EXPERTS, tb), lambda i: (i, 0, 0)),
                   pl.BlockSpec((None, N_EXPERTS, LANES), lambda i: (i, 0, 0))],
        out_shape=[jax.ShapeDtypeStruct((m, LANES), F32), jax.ShapeDtypeStruct((m, LANES), F32),
                   jax.ShapeDtypeStruct((nb, N_EXPERTS, tb), F32),
                   jax.ShapeDtypeStruct((nb, N_EXPERTS, LANES), F32)],
        compiler_params=_cparams(("parallel",)))(x, wh, wl)


def _moe_kernel(x_ref, comb_ref, wg_ref, wu_ref, wd_ref, g_ref, b_ref, y_ref, acc_ref, *, alpha):
    e, f = pl.program_id(1), pl.program_id(2)

    @pl.when((e == 0) & (f == 0))
    def _():
        acc_ref[...] = jnp.zeros(acc_ref.shape, F32)

    x = x_ref[...].astype(BF16)
    a = jnp.dot(x, wg_ref[...], preferred_element_type=F32)
    u = jnp.dot(x, wu_ref[...], preferred_element_type=F32)
    hmid = (_silu(a) * u).astype(BF16)
    comb = comb_ref[...]
    lane = lax.broadcasted_iota(jnp.int32, comb.shape, 1)
    ce = jnp.sum(jnp.where(lane == e, comb, 0.0), axis=-1, keepdims=True)
    acc_ref[...] += ce * jnp.dot(hmid, wd_ref[...], preferred_element_type=F32)

    @pl.when((e == pl.num_programs(1) - 1) & (f == pl.num_programs(2) - 1))
    def _():
        y_ref[...] = _layer_norm(alpha * x_ref[...] + acc_ref[...], g_ref[...], b_ref[...])


def moe_ffn(x, comb, wg, wu, wd, g, b, alpha):
    m = x.shape[0]
    ne, _, f = wg.shape
    tm, tf = _tile(m, 1024), _tile(f, 512)
    xspec = pl.BlockSpec((tm, D_MODEL), lambda i, e, j: (i, 0))
    full = lambda a: pl.BlockSpec(a.shape, lambda i, e, j: (0,) * a.ndim)
    return pl.pallas_call(
        functools.partial(_moe_kernel, alpha=alpha), grid=(m // tm, ne, f // tf),
        in_specs=[xspec, pl.BlockSpec((tm, LANES), lambda i, e, j: (i, 0)),
                  pl.BlockSpec((None, D_MODEL, tf), lambda i, e, j: (e, 0, j)),
                  pl.BlockSpec((None, D_MODEL, tf), lambda i, e, j: (e, 0, j)),
                  pl.BlockSpec((None, tf, D_MODEL), lambda i, e, j: (e, j, 0)),
                  full(g), full(b)],
        out_specs=xspec,
        out_shape=jax.ShapeDtypeStruct((m, D_MODEL), F32),
        scratch_shapes=[pltpu.VMEM((tm, D_MODEL), F32)],
        compiler_params=_cparams(("parallel", "arbitrary", "arbitrary")))(x, comb, wg, wu, wd, g, b)


def _moe_routed_kernel(cnt_ref, x_ref, comb_ref, rankc_ref, rankr_ref, wg_ref, wu_ref, wd_ref, g_ref, b_ref,
                       y_ref, acc_ref, xb_ref, xc_ref, yacc_ref, *, alpha, rows, cap):
    blk, e, f = pl.program_id(0), pl.program_id(1), pl.program_id(2)
    last_f = pl.num_programs(2) - 1
    n = cnt_ref[blk * N_EXPERTS + e]
    tb = x_ref.shape[0]

    @pl.when((e == 0) & (f == 0))
    def _():
        acc_ref[...] = jnp.zeros(acc_ref.shape, F32)
        xb_ref[...] = x_ref[...].astype(BF16)

    def column(ref):
        a = ref[...]
        lane = lax.broadcasted_iota(jnp.int32, a.shape, 1)
        return jnp.sum(jnp.where(lane == e, a, 0.0), axis=-1, keepdims=True)

    def expert(xs):
        a = jnp.dot(xs, wg_ref[...], preferred_element_type=F32)
        u = jnp.dot(xs, wu_ref[...], preferred_element_type=F32)
        return jnp.dot((_silu(a) * u).astype(BF16), wd_ref[...], preferred_element_type=F32)

    @pl.when((n > 0) & (n <= rows))
    def _():
        @pl.when(f == 0)
        def _():
            rr = rankr_ref[...]
            sub = lax.broadcasted_iota(jnp.int32, rr.shape, 0)
            rrow = jnp.sum(jnp.where(sub == e, rr, 0.0), axis=0, keepdims=True)
            slot = lax.broadcasted_iota(jnp.int32, (rows, tb), 0).astype(F32)
            gather = jnp.where(slot == rrow, 1.0, 0.0).astype(BF16)
            xc_ref[...] = jnp.dot(gather, xb_ref[...], preferred_element_type=F32).astype(BF16)
            yacc_ref[0:cap, :] = jnp.zeros((cap, D_MODEL), F32)

        yacc_ref[0:rows, :] += expert(xc_ref[...])

        @pl.when(f == last_f)
        def _():
            slot = lax.broadcasted_iota(jnp.int32, (tb, cap), 1).astype(F32)
            scatter = jnp.where(slot == column(rankc_ref), 1.0, 0.0).astype(BF16)
            acc_ref[...] += column(comb_ref) * jnp.dot(scatter, yacc_ref[0:cap, :].astype(BF16),
                                                       preferred_element_type=F32)

    @pl.when(n > rows)
    def _():
        @pl.when(f == 0)
        def _():
            yacc_ref[...] = jnp.zeros(yacc_ref.shape, F32)

        yacc_ref[...] += expert(xb_ref[...])

        @pl.when(f == last_f)
        def _():
            acc_ref[...] += column(comb_ref) * yacc_ref[...]

    @pl.when((e == pl.num_programs(1) - 1) & (f == last_f))
    def _():
        y_ref[...] = _layer_norm(alpha * x_ref[...] + acc_ref[...], g_ref[...], b_ref[...])


def moe_routed_ffn(x, w_router, wg, wu, wd, g, b, alpha):
    m = x.shape[0]
    ne, _, f = wg.shape
    tb, tf = _tile(m, MOE_BLOCK), _tile(f, MOE_TF)
    rows = min(tb, MOE_ROWS)
    cap = min(tb, -(-rows // LANES) * LANES)
    comb, rankc, rankr, cnt = router_ranked(x, w_router, tb)
    counts = cnt[:, 0, :N_EXPERTS].astype(jnp.int32).reshape(-1)
    xspec = pl.BlockSpec((tb, D_MODEL), lambda i, e, j, c: (i, 0))
    tok = pl.BlockSpec((tb, LANES), lambda i, e, j, c: (i, 0))
    full = lambda a: pl.BlockSpec(a.shape, lambda i, e, j, c: (0,) * a.ndim)
    grid_spec = pltpu.PrefetchScalarGridSpec(
        num_scalar_prefetch=1, grid=(m // tb, ne, f // tf),
        in_specs=[xspec, tok, tok, pl.BlockSpec((None, N_EXPERTS, tb), lambda i, e, j, c: (i, 0, 0)),
                  pl.BlockSpec((None, D_MODEL, tf), lambda i, e, j, c: (e, 0, j)),
                  pl.BlockSpec((None, D_MODEL, tf), lambda i, e, j, c: (e, 0, j)),
                  pl.BlockSpec((None, tf, D_MODEL), lambda i, e, j, c: (e, j, 0)),
                  full(g), full(b)],
        out_specs=xspec,
        scratch_shapes=[pltpu.VMEM((tb, D_MODEL), F32), pltpu.VMEM((tb, D_MODEL), BF16),
                        pltpu.VMEM((rows, D_MODEL), BF16), pltpu.VMEM((tb, D_MODEL), F32)])
    return pl.pallas_call(
        functools.partial(_moe_routed_kernel, alpha=alpha, rows=rows, cap=cap),
        grid_spec=grid_spec,
        out_shape=jax.ShapeDtypeStruct((m, D_MODEL), F32),
        compiler_params=_cparams(("parallel", "arbitrary", "arbitrary")))(
            counts, x, comb, rankc, rankr, wg, wu, wd, g, b)


def _layer_weights(l, w_in, lam_q1, lam_k1, lam_q2, lam_k2, subln_w, w_gla_up, b_gla_up, gla_norm_w,
                   ret_norm_w, w_pa, w_pb, w_pc, w_out, ln1_g, ln1_b, ln2_g, ln2_b):
    wl = w_in[l]
    w_main = jnp.concatenate([wl[:, a:b] for a, b in PACK_ORDER], axis=1).astype(BF16)
    w_lrg = jnp.zeros((D_MODEL, LANES), F32).at[:, :G_RANK].set(wl[:, LRG_OFF:LRG_OFF + G_RANK]).astype(BF16)
    w_up = jnp.zeros((LANES, HK), F32).at[:G_RANK].set(w_gla_up[l]).astype(BF16)
    lamvec = jnp.zeros((8, LANES), F32)
    for r, vec in enumerate((lam_q1, lam_k1, lam_q2, lam_k2)):
        lamvec = lamvec.at[r, :A_QK].set(vec[l].astype(F32))
    row = lambda a: a[l].reshape(1, -1)
    return dict(w_main=w_main, w_lrg=w_lrg, w_up=w_up, b_up=row(b_gla_up), lamvec=lamvec,
                subln=row(subln_w), gla_nw=row(gla_norm_w), ret_nw=row(ret_norm_w),
                wpa=w_pa[l].astype(BF16), wpb=w_pb[l].astype(BF16), wpc=w_pc[l].astype(BF16),
                wo=w_out[l].astype(BF16), ln1_g=row(ln1_g), ln1_b=row(ln1_b),
                ln2_g=row(ln2_g), ln2_b=row(ln2_b), lam_init=0.8 - 0.6 * math.exp(-0.3 * l))


def _project(x, lw):
    h = matmul(x, lw['w_main'], tn_cap=INPROJ_TN)
    lrg = matmul(x, lw['w_lrg'], tn_cap=LANES)
    return h, lrg


def _channel_mix(x1, l, lw, ffn_w, alpha):
    if l % 2 == 0:
        wg, wu, wd = ffn_w['dense'][l // 2]
        return dense_ffn(x1, wg, wu, wd, lw['ln2_g'], lw['ln2_b'], alpha)
    w_r, wg, wu, wd = ffn_w['moe'][l // 2]
    if x1.shape[0] >= MOE_MIN_ROUTED:
        return moe_routed_ffn(x1, w_r, wg, wu, wd, lw['ln2_g'], lw['ln2_b'], alpha)
    comb = router(x1, w_r)
    return moe_ffn(x1, comb, wg, wu, wd, lw['ln2_g'], lw['ln2_b'], alpha)


def kernel(x_prompt, x_sample, cache_k, cache_v, state_gla, state_ret, page_table, w_in, lam_q1, lam_k1, lam_q2, lam_k2, subln_w, w_gla_up, b_gla_up, gla_norm_w, ret_norm_w, w_pa, w_pb, w_pc, w_out, ln1_g, ln1_b, w_ff_gate, w_ff_up, w_ff_down, w_router, w_exp_gate, w_exp_up, w_exp_down, ln2_g, ln2_b):
    bp, tp, _ = x_prompt.shape
    bs, ts, _ = x_sample.shape
    assert bp == 1 and ts == 1
    depth = w_in.shape[0]
    alpha = (2 * depth) ** 0.25
    n_pool, page = cache_k.shape[1], cache_k.shape[2]
    past_len = page_table.shape[1] * page
    pos_p = jnp.arange(tp, dtype=jnp.int32)
    pos_s = jnp.full((bs,), past_len, jnp.int32)
    ffn_w = dict(
        dense=[(w_ff_gate[i].astype(BF16), w_ff_up[i].astype(BF16), w_ff_down[i].astype(BF16))
               for i in range(w_ff_gate.shape[0])],
        moe=[(w_router[i], w_exp_gate[i].astype(BF16), w_exp_up[i].astype(BF16), w_exp_down[i].astype(BF16))
             for i in range(w_router.shape[0])])
    ck = cache_k.reshape(depth, n_pool, page * A_HEADS, LANES)
    cv = cache_v.reshape(depth, n_pool, page * A_HEADS, LANES)
    lg_col = jnp.broadcast_to(jnp.repeat(_ret_log_gamma(), R_QK)[None, :], (bs, HK))

    yp = x_prompt.reshape(tp, D_MODEL)
    ys = x_sample.reshape(bs, D_MODEL)
    outs = {n: [] for n in ('gp', 'rp', 'ks', 'vs', 'gs', 'rs')}
    row_bufs = None
    cols = lambda a, blk, n: a[:, blk * LANES:(blk + n) * LANES]
    for l in range(depth):
        lw = _layer_weights(l, w_in, lam_q1, lam_k1, lam_q2, lam_k2, subln_w, w_gla_up, b_gla_up,
                            gla_norm_w, ret_norm_w, w_pa, w_pb, w_pc, w_out, ln1_g, ln1_b, ln2_g, ln2_b)
        h, lrg = _project(yp, lw)
        qa, ka, va, *row_bufs = rotary_attention(h, _rope_tables(pos_p), ROPE_DIM // 2,
                                                 A_QK ** -0.5 * LOG2E, l, depth, row_bufs)
        oa = flash_diff_attention(qa, ka, va, lw['lamvec'], lw['subln'], lw['lam_init'])
        og, s_gla = gla_prompt(h, lrg, lw['w_up'], lw['b_up'], lw['gla_nw'])
        orr, s_ret = ret_prompt(h, _ret_tables(pos_p), lw['ret_nw'])
        x1 = mixer_out(oa, og, orr, h, yp, lw['wpa'], lw['wpb'], lw['wpc'], lw['wo'],
                       lw['ln1_g'], lw['ln1_b'], alpha)
        yp = _channel_mix(x1, l, lw, ffn_w, alpha)
        outs['gp'].append(s_gla[None])
        outs['rp'].append(s_ret[None])
        h, lrg = _project(ys, lw)
        qr, kr = rotary(h, QR, HK, _ret_tables(pos_s), 1, 1.0, R_QK ** -0.5, F32)
        qa, ka = rotary(h, QA, A_HEADS * LANES, _rope_tables(pos_s), ROPE_DIM // 2, A_QK ** -0.5, 1.0, F32)
        va = cols(h, VA, 4)
        oa = decode_diff_attention(qa, ka, va, ck, cv, l, page_table, lw['lamvec'], lw['subln'],
                                   lw['lam_init'])
        gk = matmul(lrg, lw['w_up'], bias=lw['b_up'])
        og, s_gla = recurrent_step(state_gla[l], cols(h, QG, 2), cols(h, KG, 2), gk, cols(h, VG, 4),
                                   cols(h, RG, 4), lw['gla_nw'], True, G_QK ** -0.5)
        orr, s_ret = recurrent_step(state_ret[l], qr, kr, lg_col, cols(h, VR, 4), cols(h, GR, 4),
                                    lw['ret_nw'], False, 1.0)
        x1 = mixer_out(oa, og, orr, h, ys, lw['wpa'], lw['wpb'], lw['wpc'], lw['wo'],
                       lw['ln1_g'], lw['ln1_b'], alpha)
        ys = _channel_mix(x1, l, lw, ffn_w, alpha)
        outs['ks'].append(ka.reshape(bs, 1, A_HEADS, 2 * A_QK))
        outs['vs'].append(va.reshape(bs, 1, A_HEADS, 2 * A_QK))
        outs['gs'].append(s_gla)
        outs['rs'].append(s_ret)

    st = lambda n: jnp.stack(outs[n])
    kv_shape = (depth, bp, tp, A_HEADS, 2 * A_QK)
    return (yp.reshape(bp, tp, D_MODEL), ys.reshape(bs, ts, D_MODEL), row_bufs[0].reshape(kv_shape),
            row_bufs[1].reshape(kv_shape), st('gp'),
            st('rp'), st('ks'), st('vs'), st('gs'), st('rs'))
```

```python
import functools
import math

import numpy as np
import jax
import jax.numpy as jnp
from jax import lax
from jax.experimental import pallas as pl
from jax.experimental.pallas import tpu as pltpu

F32 = jnp.float32
BF16 = jnp.bfloat16

D_MODEL = 1024
A_HEADS = 4
A_QK = 64
ROPE_DIM = 16
ROPE_THETA = 500000.0
G_HEADS = 4
G_QK = 64
G_V = 128
G_RANK = 16
G_NORMALIZER = 16.0
R_HEADS = 4
R_QK = 64
R_ANGLE_BASE = 10000.0
CHUNK = 64
N_EXPERTS = 8
NORM_EPS = 1e-5
NEG_BIG = -1e30
LOG2E = math.log2(math.e)
FLASH_T = 1024
FLASH_RC = 32
FLASH_KCOLS = 256
FLASH_PVROWS = 128
DECODE_PAGES = 32
SCAN_UNROLL = 4
MOE_BLOCK = 1024
MOE_ROWS = 320
MOE_MIN_ROUTED = 256
MOE_TF = 896
INPROJ_TN = 1536

LANES = 128
VMEM_LIMIT = 56 * 1024 * 1024

ZA, ZB, ZC, QA, KA, VA, VG, RG, VR, GR, QG, KG, QR, KR = 0, 8, 16, 24, 28, 32, 36, 40, 44, 48, 52, 54, 56, 58
PACK_ORDER = ((4624, 7696), (0, 1536), (2048, 2560), (2576, 3088), (3600, 4112), (4112, 4624),
              (1536, 2048), (3088, 3600))
LRG_OFF = 2560


def _cparams(sem):
    return pltpu.CompilerParams(dimension_semantics=sem, vmem_limit_bytes=VMEM_LIMIT)


def _tile(n, cap):
    c = min(n, cap)
    while n % c:
        c -= 1
    return c


def _layer_norm(y, g, b):
    mu = jnp.mean(y, axis=-1, keepdims=True)
    d = y - mu
    var = jnp.mean(d * d, axis=-1, keepdims=True)
    return d * lax.rsqrt(var + NORM_EPS) * g + b


def _rms(o, w):
    return o * lax.rsqrt(jnp.mean(o * o, axis=-1, keepdims=True) + NORM_EPS) * w


def _silu(x):
    return x * (1.0 / (1.0 + jnp.exp(-x)))


def _sigmoid(x):
    return 1.0 / (1.0 + jnp.exp(-x))


def _log_sigmoid(x):
    return jnp.minimum(x, 0.0) - jnp.log(1.0 + jnp.exp(-jnp.abs(x)))


def _split3(x):
    a = x.astype(BF16)
    r = x - a.astype(F32)
    b = r.astype(BF16)
    c = (r - b.astype(F32)).astype(BF16)
    return a, b, c


def _mm_kernel(x_ref, w_ref, o_ref):
    o_ref[...] = jnp.dot(x_ref[...].astype(BF16), w_ref[...],
                         preferred_element_type=F32).astype(o_ref.dtype)


def _mm_bias_kernel(x_ref, w_ref, b_ref, o_ref):
    o_ref[...] = (jnp.dot(x_ref[...].astype(BF16), w_ref[...],
                          preferred_element_type=F32) + b_ref[...]).astype(o_ref.dtype)


def matmul(x, w, bias=None, out_dtype=F32, tm_cap=1024, tn_cap=768):
    m, k = x.shape
    n = w.shape[1]
    tm, tn = _tile(m, tm_cap), _tile(n, tn_cap)
    in_specs = [pl.BlockSpec((tm, k), lambda i, j: (i, 0)),
                pl.BlockSpec((k, tn), lambda i, j: (0, j))]
    args = [x, w]
    kern = _mm_kernel
    if bias is not None:
        in_specs.append(pl.BlockSpec((1, tn), lambda i, j: (0, j)))
        args.append(bias)
        kern = _mm_bias_kernel
    return pl.pallas_call(
        kern, grid=(m // tm, n // tn), in_specs=in_specs,
        out_specs=pl.BlockSpec((tm, tn), lambda i, j: (i, j)),
        out_shape=jax.ShapeDtypeStruct((m, n), out_dtype),
        compiler_params=_cparams(("parallel", "parallel")))(*args)


def _rot_kernel(x_ref, c_ref, s1_ref, s2_ref, oq_ref, ok_ref, *, shift, q_scale, k_scale):
    c, s1, s2 = c_ref[...], s1_ref[...], s2_ref[...]
    nq = oq_ref.shape[1] // LANES
    nk = ok_ref.shape[1] // LANES
    for b in range(nq + nk):
        x = x_ref[:, b * LANES:(b + 1) * LANES]
        y = x * c + pltpu.roll(x, LANES - shift, 1) * s1 + pltpu.roll(x, shift, 1) * s2
        if b < nq:
            oq_ref[:, b * LANES:(b + 1) * LANES] = (y * q_scale).astype(oq_ref.dtype)
        else:
            ok_ref[:, (b - nq) * LANES:(b - nq + 1) * LANES] = (y * k_scale).astype(ok_ref.dtype)


def rotary(h, col_blk, width, tables, shift, q_scale, k_scale, q_dtype):
    m = h.shape[0]
    tm = _tile(m, 512)
    blk = col_blk * LANES // (2 * width)
    tspec = pl.BlockSpec((tm, LANES), lambda i: (i, 0))
    return pl.pallas_call(
        functools.partial(_rot_kernel, shift=shift, q_scale=q_scale, k_scale=k_scale),
        grid=(m // tm,),
        in_specs=[pl.BlockSpec((tm, 2 * width), lambda i: (i, blk)), tspec, tspec, tspec],
        out_specs=[pl.BlockSpec((tm, width), lambda i: (i, 0)),
                   pl.BlockSpec((tm, width), lambda i: (i, 0))],
        out_shape=[jax.ShapeDtypeStruct((m, width), q_dtype),
                   jax.ShapeDtypeStruct((m, width), F32)],
        compiler_params=_cparams(("parallel",)))(h, *tables)


def _rot_att_kernel(x_ref, v_ref, c_ref, s1_ref, s2_ref, *rest, shift, q_scale):
    oq_ref, ok_ref, ov_ref, okr_ref, ovr_ref = rest[-5:]
    c, s1, s2 = c_ref[...], s1_ref[...], s2_ref[...]
    tm = x_ref.shape[0]
    for b in range(2 * A_HEADS):
        x = x_ref[:, b * LANES:(b + 1) * LANES]
        y = x * c + pltpu.roll(x, LANES - shift, 1) * s1 + pltpu.roll(x, shift, 1) * s2
        if b < A_HEADS:
            oq_ref[:, b * LANES:(b + 1) * LANES] = (y * q_scale).astype(oq_ref.dtype)
        else:
            hh = b - A_HEADS
            ok_ref[:, hh * LANES:(hh + 1) * LANES] = y.astype(ok_ref.dtype)
            okr_ref[pl.ds(hh, tm, stride=A_HEADS), :] = y
    for hh in range(A_HEADS):
        v = v_ref[:, hh * LANES:(hh + 1) * LANES]
        ov_ref[:, hh * LANES:(hh + 1) * LANES] = v.astype(ov_ref.dtype)
        ovr_ref[pl.ds(hh, tm, stride=A_HEADS), :] = v


def rotary_attention(h, tables, shift, q_scale, layer, depth, row_bufs=None):
    m = h.shape[0]
    tm = _tile(m, 512)
    nblk = m // tm
    width = A_HEADS * LANES
    tspec = pl.BlockSpec((tm, LANES), lambda i: (i, 0))
    wide = pl.BlockSpec((tm, width), lambda i: (i, 0))
    rowsp = pl.BlockSpec((tm * A_HEADS, LANES), lambda i: (layer * nblk + i, 0))
    in_specs = [pl.BlockSpec((tm, 2 * width), lambda i: (i, QA // 8)),
                pl.BlockSpec((tm, width), lambda i: (i, VA // 4)), tspec, tspec, tspec]
    args = [h, h, *tables]
    aliases = {}
    if row_bufs is not None:
        in_specs += [pl.BlockSpec(memory_space=pl.ANY)] * 2
        aliases = {len(args): 3, len(args) + 1: 4}
        args += list(row_bufs)
    return pl.pallas_call(
        functools.partial(_rot_att_kernel, shift=shift, q_scale=q_scale),
        grid=(nblk,),
        in_specs=in_specs,
        out_specs=[wide, wide, wide, rowsp, rowsp],
        out_shape=[jax.ShapeDtypeStruct((m, width), BF16)] * 3
                  + [jax.ShapeDtypeStruct((depth * m * A_HEADS, LANES), F32)] * 2,
        input_output_aliases=aliases,
        compiler_params=_cparams(("parallel",)))(*args)


def _rope_tables(pos):
    half = ROPE_DIM // 2
    inv = ROPE_THETA ** (-jnp.arange(half, dtype=F32) * 2.0 / ROPE_DIM)
    ang = pos.astype(F32)[:, None] * inv[None, :]
    cos, sin = jnp.cos(ang), jnp.sin(ang)
    lane = np.arange(LANES) % A_QK
    fi = lane % half
    cos_l, sin_l = cos[:, fi], sin[:, fi]
    in_rot = (lane < ROPE_DIM)[None, :]
    lo = (lane < half)[None, :]
    c = jnp.where(in_rot, cos_l, 1.0)
    s1 = jnp.where(lo, -sin_l, 0.0)
    s2 = jnp.where(in_rot & ~lo, sin_l, 0.0)
    return c, s1, s2


def _ret_tables(pos):
    inv = 1.0 / (R_ANGLE_BASE ** jnp.linspace(0.0, 1.0, R_QK // 2, dtype=F32))
    ang = pos.astype(F32)[:, None] * inv[None, :]
    cos, sin = jnp.cos(ang), jnp.sin(ang)
    lane = np.arange(LANES) % R_QK
    cos_l, sin_l = cos[:, lane // 2], sin[:, lane // 2]
    even = (lane % 2 == 0)[None, :]
    return cos_l, jnp.where(even, -sin_l, 0.0), jnp.where(even, 0.0, sin_l)


def _lam_from(lam_ref, lam_init):
    v = lam_ref[...]
    t1 = jnp.sum(v[0:1] * v[1:2], axis=-1, keepdims=True)
    t2 = jnp.sum(v[2:3] * v[3:4], axis=-1, keepdims=True)
    return jnp.exp(t1) - jnp.exp(t2) + lam_init


def _flash_kernel(qi_ref, kj_ref, lam_ref, q_ref, k_ref, v_ref, w_ref, o_ref, m_ref, acc_ref, sa_ref, sb_ref,
                  p_ref, *, t, rc, lam_init):
    step = pl.program_id(1)
    i, j = qi_ref[step], kj_ref[step]
    nch = t // LANES

    kw = min(t, FLASH_KCOLS)
    pvr = min(t, FLASH_PVROWS)

    def produce_items(s_dst):
        items = []
        for c in range(2):
            for kc in range(t // kw):
                def item(c=c, kc=kc):
                    q = q_ref[...]
                    lane = lax.broadcasted_iota(jnp.int32, q.shape, 1)
                    qc = jnp.where((lane < A_QK) if c == 0 else (lane >= A_QK), q, jnp.zeros_like(q))
                    kk = k_ref[kc * kw:(kc + 1) * kw, :].astype(BF16)
                    s_dst[c, :, kc * kw:(kc + 1) * kw] = lax.dot_general(
                        qc, kk, (((1,), (1,)), ((), ())), preferred_element_type=F32)
                items.append(item)
        return items

    def consume_items(s_src, masked):
        items = []
        for g in range(t // pvr):
            def sweep(g=g):
                for r in range(g * pvr // rc, (g + 1) * pvr // rc):
                    rows = slice(r * rc, (r + 1) * rc)
                    if masked:
                        keep = (lax.broadcasted_iota(jnp.int32, (rc, t), 1)
                                <= r * rc + lax.broadcasted_iota(jnp.int32, (rc, t), 0))
                    for c in range(2):
                        s = s_src[c, rows, :]
                        if masked:
                            s = jnp.where(keep, s, NEG_BIG)
                        m_prev = m_ref[c, rows, :]
                        m_new = jnp.maximum(m_prev, jnp.max(s, axis=-1, keepdims=True))
                        p = jnp.exp2(s - jnp.concatenate([m_new] * nch, axis=1))
                        alpha = jnp.exp2(m_prev - m_new)
                        acc_ref[c, rows, :] = jnp.concatenate([alpha, alpha], axis=1) * acc_ref[c, rows, :]
                        m_ref[c, rows, :] = m_new
                        p_ref[c, rows, :] = p.astype(BF16)

            def pv(g=g):
                v = jnp.concatenate([v_ref[...].astype(BF16), jnp.ones((t, LANES), BF16)], axis=1)
                rows = slice(g * pvr, (g + 1) * pvr)
                for c in range(2):
                    acc_ref[c, rows, :] += jnp.dot(p_ref[c, rows, :], v, preferred_element_type=F32)
            items += [sweep, pv]
        return items

    def run(*item_lists):
        longest = max(len(l) for l in item_lists)
        for n in range(longest):
            for l in item_lists:
                lo, hi = n * len(l) // longest, (n + 1) * len(l) // longest
                for it in l[lo:hi]:
                    it()

    def produce(s_dst):
        run(produce_items(s_dst))

    def consume(s_src, masked):
        run(consume_items(s_src, masked))

    even = j % 2 == 0

    @pl.when(j == 0)
    def _():
        m_ref[...] = jnp.full(m_ref.shape, NEG_BIG, F32)
        acc_ref[...] = jnp.zeros(acc_ref.shape, F32)
        produce(sa_ref)

    steady = (j >= 1) & (j <= i)

    @pl.when(steady & even)
    def _():
        run(consume_items(sb_ref, False), produce_items(sa_ref))

    @pl.when(steady & jnp.logical_not(even))
    def _():
        run(consume_items(sa_ref, False), produce_items(sb_ref))

    drain = j == i + 1
    pl.when(drain & even)(functools.partial(consume, sb_ref, True))
    pl.when(drain & jnp.logical_not(even))(functools.partial(consume, sa_ref, True))

    @pl.when(drain)
    def _():
        lam = _lam_from(lam_ref, lam_init)
        o = (acc_ref[0, :, :LANES] / acc_ref[0, :, LANES:]
             - lam * (acc_ref[1, :, :LANES] / acc_ref[1, :, LANES:]))
        o_ref[...] = _rms(o, w_ref[...]) * (1.0 - lam_init)


def flash_diff_attention(q, k, v, lamvec, subln_w, lam_init, t_cap=FLASH_T, rc=FLASH_RC):
    n = q.shape[0]
    t = _tile(n, t_cap)
    rc = _tile(t, rc)
    nb = n // t
    pairs = [(i, j) for i in range(nb) for j in range(i + 2)]
    qi = jnp.asarray([p[0] for p in pairs], jnp.int32)
    kj = jnp.asarray([p[1] for p in pairs], jnp.int32)
    grid_spec = pltpu.PrefetchScalarGridSpec(
        num_scalar_prefetch=2, grid=(A_HEADS, len(pairs)),
        in_specs=[pl.BlockSpec((8, LANES), lambda hh, s, qi, kj: (0, 0)),
                  pl.BlockSpec((t, LANES), lambda hh, s, qi, kj: (qi[s], hh)),
                  pl.BlockSpec((t, LANES), lambda hh, s, qi, kj: (jnp.minimum(kj[s], qi[s]), hh)),
                  pl.BlockSpec((t, LANES), lambda hh, s, qi, kj: (jnp.clip(kj[s] - 1, 0, qi[s]), hh)),
                  pl.BlockSpec((1, LANES), lambda hh, s, qi, kj: (0, 0))],
        out_specs=pl.BlockSpec((t, LANES), lambda hh, s, qi, kj: (qi[s], hh)),
        scratch_shapes=[pltpu.VMEM((2, t, LANES), F32), pltpu.VMEM((2, t, 2 * LANES), F32),
                        pltpu.VMEM((2, t, t), F32), pltpu.VMEM((2, t, t), F32),
                        pltpu.VMEM((2, t, t), BF16)])
    return pl.pallas_call(
        functools.partial(_flash_kernel, t=t, rc=rc, lam_init=lam_init),
        grid_spec=grid_spec,
        out_shape=jax.ShapeDtypeStruct((n, A_HEADS * LANES), F32),
        compiler_params=_cparams(("parallel", "arbitrary")))(qi, kj, lamvec, q, k, v, subln_w)


def _decode_kernel(pt_ref, lam_ref, q_ref, kn_ref, vn_ref, w_ref, *rest, pp, lam_init):
    k_refs, v_refs = rest[:pp], rest[pp:2 * pp]
    o_ref, m_ref, l_ref, acc_ref = rest[2 * pp:]
    c = pl.program_id(1)
    rows = 2 * A_HEADS
    prow = k_refs[0].shape[0]

    @pl.when(c == 0)
    def _():
        m_ref[...] = jnp.full(m_ref.shape, NEG_BIG, F32)
        l_ref[...] = jnp.zeros(l_ref.shape, F32)
        acc_ref[...] = jnp.zeros(acc_ref.shape, F32)

    rid = lax.broadcasted_iota(jnp.int32, (rows, LANES), 0)
    lid = lax.broadcasted_iota(jnp.int32, (rows, LANES), 1)
    qb = jnp.where(lid // A_QK == rid % 2, q_ref[...], 0.0).astype(BF16)
    s = jnp.concatenate(
        [lax.dot_general(qb, k_refs[p][...].astype(BF16), (((1,), (1,)), ((), ())),
                         preferred_element_type=F32) for p in range(pp)], axis=-1)
    srow = lax.broadcasted_iota(jnp.int32, s.shape, 0)
    scol = lax.broadcasted_iota(jnp.int32, s.shape, 1)
    s = jnp.where(scol % A_HEADS == srow // 2, s, NEG_BIG)
    m_prev = m_ref[...]
    m_new = jnp.maximum(m_prev, jnp.max(s, axis=-1, keepdims=True))
    p_ = jnp.exp(s - m_new)
    alpha = jnp.exp(m_prev - m_new)
    l_new = alpha * l_ref[...] + jnp.sum(p_, axis=-1, keepdims=True)
    acc = alpha * acc_ref[...]
    for p in range(pp):
        acc = acc + jnp.dot(p_[:, p * prow:(p + 1) * prow].astype(BF16), v_refs[p][...].astype(BF16),
                            preferred_element_type=F32)
    m_ref[...] = m_new
    l_ref[...] = l_new
    acc_ref[...] = acc

    @pl.when(c == pl.num_programs(1) - 1)
    def _():
        kn = kn_ref[...].astype(BF16).astype(F32)
        vn = vn_ref[...].astype(BF16).astype(F32)
        s_self = jnp.sum(qb.astype(F32) * kn, axis=-1, keepdims=True)
        m_fin = jnp.maximum(m_new, s_self)
        a2 = jnp.exp(m_new - m_fin)
        p_self = jnp.exp(s_self - m_fin)
        l_fin = a2 * l_new + p_self
        acc_fin = a2 * acc + p_self.astype(BF16).astype(F32) * vn
        on = acc_fin / l_fin
        lam = _lam_from(lam_ref, lam_init)
        w = w_ref[...]
        for hh in range(A_HEADS):
            o = on[2 * hh:2 * hh + 1] - lam * on[2 * hh + 1:2 * hh + 2]
            o_ref[hh:hh + 1, :] = _rms(o, w) * (1.0 - lam_init)


def decode_diff_attention(q, k_new, v_new, cache_k, cache_v, layer, page_table, lamvec, subln_w, lam_init):
    b, n_pages = page_table.shape
    prow = cache_k.shape[2]
    pp = _tile(n_pages, DECODE_PAGES)
    rows = 2 * A_HEADS

    def row(bb, c, pt):
        return (bb, 0, 0)

    def page_idx(p):
        return lambda bb, c, pt: (layer, pt[bb, c * pp + p], 0, 0)

    row_spec = pl.BlockSpec((None, rows, LANES), row)
    page_specs = [pl.BlockSpec((None, None, prow, LANES), page_idx(p)) for p in range(pp)]
    grid_spec = pltpu.PrefetchScalarGridSpec(
        num_scalar_prefetch=1, grid=(b, n_pages // pp),
        in_specs=[pl.BlockSpec((8, LANES), lambda bb, c, pt: (0, 0)), row_spec, row_spec, row_spec,
                  pl.BlockSpec((1, LANES), lambda bb, c, pt: (0, 0))] + page_specs + page_specs,
        out_specs=pl.BlockSpec((None, A_HEADS, LANES), row),
        scratch_shapes=[pltpu.VMEM((rows, 1), F32), pltpu.VMEM((rows, 1), F32),
                        pltpu.VMEM((rows, LANES), F32)])
    r8 = lambda a: jnp.repeat(a.reshape(b, A_HEADS, LANES), 2, axis=1)
    out = pl.pallas_call(
        functools.partial(_decode_kernel, pp=pp, lam_init=lam_init),
        grid_spec=grid_spec,
        out_shape=jax.ShapeDtypeStruct((b, A_HEADS, LANES), F32),
        compiler_params=_cparams(("parallel", "arbitrary")))(
            page_table, lamvec, r8(q), r8(k_new), r8(v_new), subln_w,
            *([cache_k] * pp), *([cache_v] * pp))
    return out.reshape(b, A_HEADS * LANES)


HK = G_HEADS * G_QK
HV = G_HEADS * G_V
GLA_LEVELS = (1, 2, 4, 8, 16, 32)


def _gla_constants():
    c = CHUNK
    t = np.arange(c)[:, None]
    i = np.arange(c)[None, :]
    def prefix(s):
        return (i // s == t // s) & (i <= t)

    def suffix(s):
        return (i // s == t // s) & (i > t)

    mats = [np.where(t % (2 * s) >= s, prefix(s), suffix(s)) for s in GLA_LEVELS]
    mats += [prefix(c), suffix(c)]
    tri = np.concatenate(mats, axis=0).astype(np.float32)
    j = i
    level = np.full((c, c), -1, np.int32)
    level[t == j] = 0
    for n, s in enumerate(GLA_LEVELS):
        sel = (t // (2 * s) == j // (2 * s)) & (t % (2 * s) >= s) & (j % (2 * s) < s)
        level[sel] = n + 1
    level = np.tile(level, (G_HEADS, 1))
    headmask = (np.arange(HK)[None, :] // G_QK == np.arange(G_HEADS * c)[:, None] // c).astype(np.float32)
    return jnp.asarray(tri, BF16), jnp.asarray(level), jnp.asarray(headmask)


def _stack_heads(x, hm):
    return jnp.concatenate([x] * G_HEADS, axis=0) * hm


def _col_bcast(row, width):
    n = row.shape[1]
    eye = lax.broadcasted_iota(jnp.int32, (n, n), 0) == lax.broadcasted_iota(jnp.int32, (n, n), 1)
    ones = jnp.ones((n, width), BF16)
    out = jnp.zeros((n, width), F32)
    for part in _split3(row):
        d = jnp.where(eye, jnp.broadcast_to(part.astype(F32), (n, n)), 0.0)
        out = out + jnp.dot(d.astype(BF16), ones, preferred_element_type=F32)
    return out


def _gla_kernel(q_ref, k_ref, v_ref, rg_ref, lr_ref, wup_ref, bup_ref, nw_ref, tri_ref, lvl_ref, hm_ref,
                o_ref, s_out_ref, s_ref, *, n_chunks):
    c = CHUNK

    @pl.when(pl.program_id(0) == 0)
    def _():
        s_ref[...] = jnp.zeros(s_ref.shape, F32)

    tri = tri_ref[...]
    lvl = lvl_ref[...]
    hm = hm_ref[...]
    nl = len(GLA_LEVELS)

    def chunk(ci, carry):
        r0 = pl.multiple_of(ci * c, c)
        rows = pl.ds(r0, c)
        q = q_ref[rows, :] * (G_QK ** -0.5)
        k = k_ref[rows, :]
        v = v_ref[rows, :].astype(BF16)
        gk = jnp.dot(lr_ref[rows, :].astype(BF16), wup_ref[...], preferred_element_type=F32) + bup_ref[...]
        g = _log_sigmoid(gk) / G_NORMALIZER
        ps = jnp.zeros(((nl + 2) * c, HK), F32)
        for part in _split3(g):
            ps = ps + jnp.dot(tri, part, preferred_element_type=F32)
        gcum, gsuf = ps[nl * c:(nl + 1) * c], ps[(nl + 1) * c:(nl + 2) * c]
        a = jnp.zeros((G_HEADS * c, c), F32)
        for n in range(nl + 1):
            if n == 0:
                qq, kk = q, k
            else:
                f = jnp.exp(ps[(n - 1) * c:n * c])
                qq, kk = q * f, k * f
            d = lax.dot_general(_stack_heads(qq, hm).astype(BF16), kk.astype(BF16),
                                (((1,), (1,)), ((), ())), preferred_element_type=F32)
            a = jnp.where(lvl == n, d, a)
        s_old = s_ref[...]
        inter = jnp.dot(_stack_heads(q * jnp.exp(gcum), hm).astype(BF16), s_old.astype(BF16),
                        preferred_element_type=F32)
        ab = a.astype(BF16)
        nw = nw_ref[...]
        kv = lax.dot_general((k * jnp.exp(gsuf)).astype(BF16), v, (((0,), (0,)), ((), ())),
                             preferred_element_type=F32)
        decay = jnp.exp(_col_bcast(gcum[c - 1:c, :], G_V))
        for hh in range(G_HEADS):
            vs = slice(hh * G_V, (hh + 1) * G_V)
            o = inter[hh * c:(hh + 1) * c] + jnp.dot(ab[hh * c:(hh + 1) * c], v[:, vs],
                                                     preferred_element_type=F32)
            o_ref[rows, vs] = _rms(o, nw) * _silu(rg_ref[rows, vs])
            ks = slice(hh * G_QK, (hh + 1) * G_QK)
            s_ref[ks, :] = decay[ks] * s_old[ks] + kv[ks, vs]
        return carry

    lax.fori_loop(0, n_chunks, chunk, 0, unroll=SCAN_UNROLL)

    @pl.when(pl.program_id(0) == pl.num_programs(0) - 1)
    def _():
        s_out_ref[...] = s_ref[...]


def gla_prompt(h, lrg, w_up, b_up, norm_w):
    t = h.shape[0]
    tb = _tile(t, 512)
    tri, lvl, hm = _gla_constants()
    full = lambda a: pl.BlockSpec(a.shape, lambda i: (0,) * a.ndim)
    o, s = pl.pallas_call(
        functools.partial(_gla_kernel, n_chunks=tb // CHUNK),
        grid=(t // tb,),
        in_specs=[pl.BlockSpec((tb, HK), lambda i: (i, QG // 2)),
                  pl.BlockSpec((tb, HK), lambda i: (i, KG // 2)),
                  pl.BlockSpec((tb, HV), lambda i: (i, VG // 4)),
                  pl.BlockSpec((tb, HV), lambda i: (i, RG // 4)),
                  pl.BlockSpec((tb, LANES), lambda i: (i, 0)),
                  full(w_up), full(b_up), full(norm_w), full(tri), full(lvl), full(hm)],
        out_specs=[pl.BlockSpec((tb, HV), lambda i: (i, 0)),
                   pl.BlockSpec((HK, G_V), lambda i: (0, 0))],
        out_shape=[jax.ShapeDtypeStruct((t, HV), F32), jax.ShapeDtypeStruct((HK, G_V), F32)],
        scratch_shapes=[pltpu.VMEM((HK, G_V), F32)],
        compiler_params=_cparams(("arbitrary",)))(h, h, h, h, lrg, w_up, b_up, norm_w, tri, lvl, hm)
    return o, s.reshape(G_HEADS, G_QK, G_V)


def _ret_kernel(q_ref, k_ref, v_ref, gr_ref, c_ref, s1_ref, s2_ref, nw_ref, dm_ref, cross_ref, tail_ref, gc_ref,
                hm_ref, o_ref, s_out_ref, s_ref, *, n_chunks):
    c = CHUNK

    def rotate(x, rows, scale):
        cc, s1, s2 = c_ref[rows, :], s1_ref[rows, :], s2_ref[rows, :]
        parts = []
        for b in range(HK // LANES):
            xb = x[:, b * LANES:(b + 1) * LANES]
            parts.append((xb * cc + pltpu.roll(xb, LANES - 1, 1) * s1 + pltpu.roll(xb, 1, 1) * s2) * scale)
        return jnp.concatenate(parts, axis=1)

    @pl.when(pl.program_id(0) == 0)
    def _():
        s_ref[...] = jnp.zeros(s_ref.shape, F32)

    hm = hm_ref[...]
    dm = dm_ref[...]
    cross = cross_ref[...]
    tail = tail_ref[...]
    gc = gc_ref[...]
    nw = nw_ref[...]

    def chunk(ci, carry):
        r0 = pl.multiple_of(ci * c, c)
        rows = pl.ds(r0, c)
        q = rotate(q_ref[rows, :], rows, 1.0)
        k = rotate(k_ref[rows, :], rows, R_QK ** -0.5)
        v = v_ref[rows, :].astype(BF16)
        qs = _stack_heads(q, hm).astype(BF16)
        a = lax.dot_general(qs, k.astype(BF16), (((1,), (1,)), ((), ())), preferred_element_type=F32) * dm
        s_old = s_ref[...]
        inter = jnp.dot(qs, s_old.astype(BF16), preferred_element_type=F32) * cross
        kv = lax.dot_general((k * tail).astype(BF16), v, (((0,), (0,)), ((), ())), preferred_element_type=F32)
        ab = a.astype(BF16)
        for hh in range(R_HEADS):
            vs = slice(hh * G_V, (hh + 1) * G_V)
            o = inter[hh * c:(hh + 1) * c] + jnp.dot(ab[hh * c:(hh + 1) * c], v[:, vs],
                                                     preferred_element_type=F32)
            o_ref[rows, vs] = _rms(o, nw) * _silu(gr_ref[rows, vs])
            ks = slice(hh * R_QK, (hh + 1) * R_QK)
            s_ref[ks, :] = gc[ks] * s_old[ks] + kv[ks, vs]
        return carry

    lax.fori_loop(0, n_chunks, chunk, 0, unroll=SCAN_UNROLL)

    @pl.when(pl.program_id(0) == pl.num_programs(0) - 1)
    def _():
        s_out_ref[...] = s_ref[...]


def _ret_log_gamma():
    return jnp.log(1.0 - jnp.exp2(-5.0 - jnp.arange(R_HEADS, dtype=F32)))


def ret_prompt(h, tables, norm_w):
    t = h.shape[0]
    tb = _tile(t, 512)
    tspec = pl.BlockSpec((tb, LANES), lambda i: (i, 0))
    c = CHUNK
    lg = _ret_log_gamma()
    idx = jnp.arange(c, dtype=F32)
    rel = idx[:, None] - idx[None, :]
    dmat = jnp.where(rel[None] >= 0, jnp.exp(jnp.maximum(rel, 0.0)[None] * lg[:, None, None]), 0.0)
    dmat = dmat.reshape(R_HEADS * c, c)
    cross = jnp.exp((idx + 1.0)[None, :] * lg[:, None]).reshape(R_HEADS * c, 1)
    cross = jnp.broadcast_to(cross, (R_HEADS * c, G_V))
    tail = jnp.exp((c - 1.0 - idx)[None, :] * lg[:, None])
    tail = jnp.repeat(tail.T, R_QK, axis=1)
    gc = jnp.broadcast_to(jnp.repeat(jnp.exp(c * lg), R_QK)[:, None], (HK, G_V))
    _, _, hm = _gla_constants()
    full = lambda a: pl.BlockSpec(a.shape, lambda i: (0,) * a.ndim)
    o, s = pl.pallas_call(
        functools.partial(_ret_kernel, n_chunks=tb // c),
        grid=(t // tb,),
        in_specs=[pl.BlockSpec((tb, HK), lambda i: (i, QR // 2)),
                  pl.BlockSpec((tb, HK), lambda i: (i, KR // 2)),
                  pl.BlockSpec((tb, HV), lambda i: (i, VR // 4)),
                  pl.BlockSpec((tb, HV), lambda i: (i, GR // 4)),
                  tspec, tspec, tspec,
                  full(norm_w), full(dmat), full(cross), full(tail), full(gc), full(hm)],
        out_specs=[pl.BlockSpec((tb, HV), lambda i: (i, 0)),
                   pl.BlockSpec((HK, G_V), lambda i: (0, 0))],
        out_shape=[jax.ShapeDtypeStruct((t, HV), F32), jax.ShapeDtypeStruct((HK, G_V), F32)],
        scratch_shapes=[pltpu.VMEM((HK, G_V), F32)],
        compiler_params=_cparams(("arbitrary",)))(h, h, h, h, *tables, norm_w, dmat, cross, tail, gc, hm)
    return o, s.reshape(R_HEADS, R_QK, G_V)


def _step_kernel(s_ref, q_ref, k_ref, d_ref, v_ref, gate_ref, nw_ref, s_out_ref, o_ref, *, is_gla, q_scale):
    d = d_ref[...]
    if is_gla:
        d = _log_sigmoid(d) / G_NORMALIZER
    s_new = jnp.exp(d) * s_ref[...] + k_ref[...] * v_ref[...]
    s_out_ref[...] = s_new
    o = jnp.sum((q_ref[...] * q_scale) * s_new, axis=1, keepdims=True)
    o_ref[...] = _rms(o, nw_ref[...]) * _silu(gate_ref[...])


def recurrent_step(state, q, k, dlog, v, gate, norm_w, is_gla, q_scale):
    b = state.shape[0]
    col = lambda a: a.reshape(b, G_HEADS, G_QK, 1)
    rowv = lambda a: a.reshape(b, G_HEADS, 1, G_V)
    cspec = pl.BlockSpec((None, G_HEADS, G_QK, 1), lambda i: (i, 0, 0, 0))
    rspec = pl.BlockSpec((None, G_HEADS, 1, G_V), lambda i: (i, 0, 0, 0))
    sspec = pl.BlockSpec((None, G_HEADS, G_QK, G_V), lambda i: (i, 0, 0, 0))
    s_new, o = pl.pallas_call(
        functools.partial(_step_kernel, is_gla=is_gla, q_scale=q_scale),
        grid=(b,),
        in_specs=[sspec, cspec, cspec, cspec, rspec, rspec, pl.BlockSpec((1, G_V), lambda i: (0, 0))],
        out_specs=[sspec, rspec],
        out_shape=[jax.ShapeDtypeStruct(state.shape, F32), jax.ShapeDtypeStruct((b, G_HEADS, 1, G_V), F32)],
        compiler_params=_cparams(("parallel",)))(state, col(q), col(k), col(dlog), rowv(v), rowv(gate), norm_w)
    return o.reshape(b, HV), s_new


def _mixer_out_kernel(oa_ref, og_ref, or_ref, za_ref, zb_ref, zc_ref, x_ref, wpa_ref, wpb_ref, wpc_ref,
                      wo_ref, g_ref, b_ref, y_ref, *, alpha):
    def branch(o_ref, z_ref, w_ref):
        return _sigmoid(z_ref[...]) * jnp.dot(o_ref[...].astype(BF16), w_ref[...], preferred_element_type=F32)

    merged = branch(oa_ref, za_ref, wpa_ref) + branch(og_ref, zb_ref, wpb_ref) + branch(or_ref, zc_ref, wpc_ref)
    y = alpha * x_ref[...] + jnp.dot(merged.astype(BF16), wo_ref[...], preferred_element_type=F32)
    y_ref[...] = _layer_norm(y, g_ref[...], b_ref[...])


def mixer_out(oa, og, orr, h, x, wpa, wpb, wpc, wo, g, b, alpha):
    m = x.shape[0]
    tm = _tile(m, 512)
    bw = oa.shape[1]
    ospec = pl.BlockSpec((tm, bw), lambda i: (i, 0))
    zspec = lambda blk: pl.BlockSpec((tm, D_MODEL), lambda i: (i, blk // 8))
    xspec = pl.BlockSpec((tm, D_MODEL), lambda i: (i, 0))
    full = lambda a: pl.BlockSpec(a.shape, lambda i: (0,) * a.ndim)
    return pl.pallas_call(
        functools.partial(_mixer_out_kernel, alpha=alpha),
        grid=(m // tm,),
        in_specs=[ospec, ospec, ospec, zspec(ZA), zspec(ZB), zspec(ZC), xspec,
                  full(wpa), full(wpb), full(wpc), full(wo), full(g), full(b)],
        out_specs=xspec,
        out_shape=jax.ShapeDtypeStruct((m, D_MODEL), F32),
        compiler_params=_cparams(("parallel",)))(oa, og, orr, h, h, h, x, wpa, wpb, wpc, wo, g, b)


def _ffn_up_kernel(x_ref, wg_ref, wu_ref, h_ref):
    x = x_ref[...].astype(BF16)
    a = jnp.dot(x, wg_ref[...], preferred_element_type=F32)
    u = jnp.dot(x, wu_ref[...], preferred_element_type=F32)
    h_ref[...] = (_silu(a) * u).astype(h_ref.dtype)


def _ffn_down_kernel(h_ref, wd_ref, x_ref, g_ref, b_ref, y_ref, *, alpha):
    y = alpha * x_ref[...] + jnp.dot(h_ref[...], wd_ref[...], preferred_element_type=F32)
    y_ref[...] = _layer_norm(y, g_ref[...], b_ref[...])


def dense_ffn(x, wg, wu, wd, g, b, alpha):
    m = x.shape[0]
    f = wg.shape[1]
    tm, tf = _tile(m, 512), _tile(f, 1408)
    hmid = pl.pallas_call(
        _ffn_up_kernel, grid=(m // tm, f // tf),
        in_specs=[pl.BlockSpec((tm, D_MODEL), lambda i, j: (i, 0)),
                  pl.BlockSpec((D_MODEL, tf), lambda i, j: (0, j)),
                  pl.BlockSpec((D_MODEL, tf), lambda i, j: (0, j))],
        out_specs=pl.BlockSpec((tm, tf), lambda i, j: (i, j)),
        out_shape=jax.ShapeDtypeStruct((m, f), BF16),
        compiler_params=_cparams(("parallel", "parallel")))(x, wg, wu)
    full = lambda a: pl.BlockSpec(a.shape, lambda i: (0,) * a.ndim)
    xspec = pl.BlockSpec((tm, D_MODEL), lambda i: (i, 0))
    return pl.pallas_call(
        functools.partial(_ffn_down_kernel, alpha=alpha), grid=(m // tm,),
        in_specs=[pl.BlockSpec((tm, f), lambda i: (i, 0)), full(wd), xspec, full(g), full(b)],
        out_specs=xspec,
        out_shape=jax.ShapeDtypeStruct((m, D_MODEL), F32),
        compiler_params=_cparams(("parallel",)))(hmid, wd, x, g, b)


def _top2_weights(x, wh, wl):
    xh = x.astype(BF16)
    xl = (x - xh.astype(F32)).astype(BF16)
    logits = (jnp.dot(xh, wh, preferred_element_type=F32) + jnp.dot(xh, wl, preferred_element_type=F32)
              + jnp.dot(xl, wh, preferred_element_type=F32))
    lane = lax.broadcasted_iota(jnp.int32, logits.shape, 1)
    logits = jnp.where(lane < N_EXPERTS, logits, NEG_BIG)
    m1 = jnp.max(logits, axis=-1, keepdims=True)
    i1 = jnp.min(jnp.where(logits == m1, lane, LANES), axis=-1, keepdims=True)
    rest = jnp.where(lane == i1, NEG_BIG, logits)
    m2 = jnp.max(rest, axis=-1, keepdims=True)
    i2 = jnp.min(jnp.where(rest == m2, lane, LANES), axis=-1, keepdims=True)
    e2 = jnp.exp(m2 - m1)
    w1 = 1.0 / (1.0 + e2)
    w2 = e2 / (1.0 + e2)
    return jnp.where(lane == i1, w1, jnp.where(lane == i2, w2, 0.0))


def _router_kernel(x_ref, wh_ref, wl_ref, comb_ref):
    comb_ref[...] = _top2_weights(x_ref[...], wh_ref[...], wl_ref[...])


def _router_rank_kernel(x_ref, wh_ref, wl_ref, comb_ref, rankc_ref, rankr_ref, cnt_ref):
    comb = _top2_weights(x_ref[...], wh_ref[...], wl_ref[...])
    tb = comb.shape[0]
    routed = comb > 0.0
    ones = jnp.where(routed, 1.0, 0.0)
    earlier = (lax.broadcasted_iota(jnp.int32, (tb, tb), 0) > lax.broadcasted_iota(jnp.int32, (tb, tb), 1))
    rank = jnp.dot(jnp.where(earlier, 1.0, 0.0).astype(BF16), ones.astype(BF16), preferred_element_type=F32)
    rankc = jnp.where(routed, rank, -1.0)
    comb_ref[...] = comb
    rankc_ref[...] = rankc
    cnt_ref[...] = jnp.broadcast_to(jnp.sum(ones, axis=0, keepdims=True), cnt_ref.shape)
    eye = (lax.broadcasted_iota(jnp.int32, (LANES, LANES), 0) == lax.broadcasted_iota(jnp.int32, (LANES, LANES), 1))
    eye = jnp.where(eye, 1.0, 0.0).astype(BF16)
    rt = jnp.zeros((LANES, tb), F32)
    for part in _split3(rankc):
        rt = rt + lax.dot_general(eye, part, (((1,), (1,)), ((), ())), preferred_element_type=F32)
    rankr_ref[...] = rt[:N_EXPERTS]


def _router_weights(w_router):
    wpad = jnp.zeros((D_MODEL, LANES), F32).at[:, :N_EXPERTS].set(w_router)
    wh = wpad.astype(BF16)
    return wh, (wpad - wh.astype(F32)).astype(BF16)


def router(x, w_router):
    m = x.shape[0]
    tm = _tile(m, 512)
    wh, wl = _router_weights(w_router)
    full = lambda a: pl.BlockSpec(a.shape, lambda i: (0,) * a.ndim)
    return pl.pallas_call(
        _router_kernel, grid=(m // tm,),
        in_specs=[pl.BlockSpec((tm, D_MODEL), lambda i: (i, 0)), full(wh), full(wl)],
        out_specs=pl.BlockSpec((tm, LANES), lambda i: (i, 0)),
        out_shape=jax.ShapeDtypeStruct((m, LANES), F32),
        compiler_params=_cparams(("parallel",)))(x, wh, wl)


def router_ranked(x, w_router, tb):
    m = x.shape[0]
    nb = m // tb
    wh, wl = _router_weights(w_router)
    full = lambda a: pl.BlockSpec(a.shape, lambda i: (0,) * a.ndim)
    tok = pl.BlockSpec((tb, LANES), lambda i: (i, 0))
    return pl.pallas_call(
        _router_rank_kernel, grid=(nb,),
        in_specs=[pl.BlockSpec((tb, D_MODEL), lambda i: (i, 0)), full(wh), full(wl)],
        out_specs=[tok, tok, pl.BlockSpec((None, N_EXPERTS, tb), lambda i: (i, 0, 0)),
                   pl.BlockSpec((None, N_EXPERTS, LANES), lambda i: (i, 0, 0))],
        out_shape=[jax.ShapeDtypeStruct((m, LANES), F32), jax.ShapeDtypeStruct((m, LANES), F32),
                   jax.ShapeDtypeStruct((nb, N_EXPERTS, tb), F32),
                   jax.ShapeDtypeStruct((nb, N_EXPERTS, LANES), F32)],
        compiler_params=_cparams(("parallel",)))(x, wh, wl)


def _moe_kernel(x_ref, comb_ref, wg_ref, wu_ref, wd_ref, g_ref, b_ref, y_ref, acc_ref, *, alpha):
    e, f = pl.program_id(1), pl.program_id(2)

    @pl.when((e == 0) & (f == 0))
    def _():
        acc_ref[...] = jnp.zeros(acc_ref.shape, F32)

    x = x_ref[...].astype(BF16)
    a = jnp.dot(x, wg_ref[...], preferred_element_type=F32)
    u = jnp.dot(x, wu_ref[...], preferred_element_type=F32)
    hmid = (_silu(a) * u).astype(BF16)
    comb = comb_ref[...]
    lane = lax.broadcasted_iota(jnp.int32, comb.shape, 1)
    ce = jnp.sum(jnp.where(lane == e, comb, 0.0), axis=-1, keepdims=True)
    acc_ref[...] += ce * jnp.dot(hmid, wd_ref[...], preferred_element_type=F32)

    @pl.when((e == pl.num_programs(1) - 1) & (f == pl.num_programs(2) - 1))
    def _():
        y_ref[...] = _layer_norm(alpha * x_ref[...] + acc_ref[...], g_ref[...], b_ref[...])


def moe_ffn(x, comb, wg, wu, wd, g, b, alpha):
    m = x.shape[0]
    ne, _, f = wg.shape
    tm, tf = _tile(m, 1024), _tile(f, 512)
    xspec = pl.BlockSpec((tm, D_MODEL), lambda i, e, j: (i, 0))
    full = lambda a: pl.BlockSpec(a.shape, lambda i, e, j: (0,) * a.ndim)
    return pl.pallas_call(
        functools.partial(_moe_kernel, alpha=alpha), grid=(m // tm, ne, f // tf),
        in_specs=[xspec, pl.BlockSpec((tm, LANES), lambda i, e, j: (i, 0)),
                  pl.BlockSpec((None, D_MODEL, tf), lambda i, e, j: (e, 0, j)),
                  pl.BlockSpec((None, D_MODEL, tf), lambda i, e, j: (e, 0, j)),
                  pl.BlockSpec((None, tf, D_MODEL), lambda i, e, j: (e, j, 0)),
                  full(g), full(b)],
        out_specs=xspec,
        out_shape=jax.ShapeDtypeStruct((m, D_MODEL), F32),
        scratch_shapes=[pltpu.VMEM((tm, D_MODEL), F32)],
        compiler_params=_cparams(("parallel", "arbitrary", "arbitrary")))(x, comb, wg, wu, wd, g, b)


def _moe_routed_kernel(cnt_ref, x_ref, comb_ref, rankc_ref, rankr_ref, wg_ref, wu_ref, wd_ref, g_ref, b_ref,
                       y_ref, acc_ref, xb_ref, xc_ref, yacc_ref, *, alpha, rows, cap):
    blk, e, f = pl.program_id(0), pl.program_id(1), pl.program_id(2)
    last_f = pl.num_programs(2) - 1
    n = cnt_ref[blk * N_EXPERTS + e]
    tb = x_ref.shape[0]

    @pl.when((e == 0) & (f == 0))
    def _():
        acc_ref[...] = jnp.zeros(acc_ref.shape, F32)
        xb_ref[...] = x_ref[...].astype(BF16)

    def column(ref):
        a = ref[...]
        lane = lax.broadcasted_iota(jnp.int32, a.shape, 1)
        return jnp.sum(jnp.where(lane == e, a, 0.0), axis=-1, keepdims=True)

    def expert(xs):
        a = jnp.dot(xs, wg_ref[...], preferred_element_type=F32)
        u = jnp.dot(xs, wu_ref[...], preferred_element_type=F32)
        return jnp.dot((_silu(a) * u).astype(BF16), wd_ref[...], preferred_element_type=F32)

    @pl.when((n > 0) & (n <= rows))
    def _():
        @pl.when(f == 0)
        def _():
            rr = rankr_ref[...]
            sub = lax.broadcasted_iota(jnp.int32, rr.shape, 0)
            rrow = jnp.sum(jnp.where(sub == e, rr, 0.0), axis=0, keepdims=True)
            slot = lax.broadcasted_iota(jnp.int32, (rows, tb), 0).astype(F32)
            gather = jnp.where(slot == rrow, 1.0, 0.0).astype(BF16)
            xc_ref[...] = jnp.dot(gather, xb_ref[...], preferred_element_type=F32).astype(BF16)
            yacc_ref[0:cap, :] = jnp.zeros((cap, D_MODEL), F32)

        yacc_ref[0:rows, :] += expert(xc_ref[...])

        @pl.when(f == last_f)
        def _():
            slot = lax.broadcasted_iota(jnp.int32, (tb, cap), 1).astype(F32)
            scatter = jnp.where(slot == column(rankc_ref), 1.0, 0.0).astype(BF16)
            acc_ref[...] += column(comb_ref) * jnp.dot(scatter, yacc_ref[0:cap, :].astype(BF16),
                                                       preferred_element_type=F32)

    @pl.when(n > rows)
    def _():
        @pl.when(f == 0)
        def _():
            yacc_ref[...] = jnp.zeros(yacc_ref.shape, F32)

        yacc_ref[...] += expert(xb_ref[...])

        @pl.when(f == last_f)
        def _():
            acc_ref[...] += column(comb_ref) * yacc_ref[...]

    @pl.when((e == pl.num_programs(1) - 1) & (f == last_f))
    def _():
        y_ref[...] = _layer_norm(alpha * x_ref[...] + acc_ref[...], g_ref[...], b_ref[...])


def moe_routed_ffn(x, w_router, wg, wu, wd, g, b, alpha):
    m = x.shape[0]
    ne, _, f = wg.shape
    tb, tf = _tile(m, MOE_BLOCK), _tile(f, MOE_TF)
    rows = min(tb, MOE_ROWS)
    cap = min(tb, -(-rows // LANES) * LANES)
    comb, rankc, rankr, cnt = router_ranked(x, w_router, tb)
    counts = cnt[:, 0, :N_EXPERTS].astype(jnp.int32).reshape(-1)
    xspec = pl.BlockSpec((tb, D_MODEL), lambda i, e, j, c: (i, 0))
    tok = pl.BlockSpec((tb, LANES), lambda i, e, j, c: (i, 0))
    full = lambda a: pl.BlockSpec(a.shape, lambda i, e, j, c: (0,) * a.ndim)
    grid_spec = pltpu.PrefetchScalarGridSpec(
        num_scalar_prefetch=1, grid=(m // tb, ne, f // tf),
        in_specs=[xspec, tok, tok, pl.BlockSpec((None, N_EXPERTS, tb), lambda i, e, j, c: (i, 0, 0)),
                  pl.BlockSpec((None, D_MODEL, tf), lambda i, e, j, c: (e, 0, j)),
                  pl.BlockSpec((None, D_MODEL, tf), lambda i, e, j, c: (e, 0, j)),
                  pl.BlockSpec((None, tf, D_MODEL), lambda i, e, j, c: (e, j, 0)),
                  full(g), full(b)],
        out_specs=xspec,
        scratch_shapes=[pltpu.VMEM((tb, D_MODEL), F32), pltpu.VMEM((tb, D_MODEL), BF16),
                        pltpu.VMEM((rows, D_MODEL), BF16), pltpu.VMEM((tb, D_MODEL), F32)])
    return pl.pallas_call(
        functools.partial(_moe_routed_kernel, alpha=alpha, rows=rows, cap=cap),
        grid_spec=grid_spec,
        out_shape=jax.ShapeDtypeStruct((m, D_MODEL), F32),
        compiler_params=_cparams(("parallel", "arbitrary", "arbitrary")))(
            counts, x, comb, rankc, rankr, wg, wu, wd, g, b)


def _layer_weights(l, w_in, lam_q1, lam_k1, lam_q2, lam_k2, subln_w, w_gla_up, b_gla_up, gla_norm_w,
                   ret_norm_w, w_pa, w_pb, w_pc, w_out, ln1_g, ln1_b, ln2_g, ln2_b):
    wl = w_in[l]
    w_main = jnp.concatenate([wl[:, a:b] for a, b in PACK_ORDER], axis=1).astype(BF16)
    w_lrg = jnp.zeros((D_MODEL, LANES), F32).at[:, :G_RANK].set(wl[:, LRG_OFF:LRG_OFF + G_RANK]).astype(BF16)
    w_up = jnp.zeros((LANES, HK), F32).at[:G_RANK].set(w_gla_up[l]).astype(BF16)
    lamvec = jnp.zeros((8, LANES), F32)
    for r, vec in enumerate((lam_q1, lam_k1, lam_q2, lam_k2)):
        lamvec = lamvec.at[r, :A_QK].set(vec[l].astype(F32))
    row = lambda a: a[l].reshape(1, -1)
    return dict(w_main=w_main, w_lrg=w_lrg, w_up=w_up, b_up=row(b_gla_up), lamvec=lamvec,
                subln=row(subln_w), gla_nw=row(gla_norm_w), ret_nw=row(ret_norm_w),
                wpa=w_pa[l].astype(BF16), wpb=w_pb[l].astype(BF16), wpc=w_pc[l].astype(BF16),
                wo=w_out[l].astype(BF16), ln1_g=row(ln1_g), ln1_b=row(ln1_b),
                ln2_g=row(ln2_g), ln2_b=row(ln2_b), lam_init=0.8 - 0.6 * math.exp(-0.3 * l))


def _project(x, lw):
    h = matmul(x, lw['w_main'], tn_cap=INPROJ_TN)
    lrg = matmul(x, lw['w_lrg'], tn_cap=LANES)
    return h, lrg


def _channel_mix(x1, l, lw, ffn_w, alpha):
    if l % 2 == 0:
        wg, wu, wd = ffn_w['dense'][l // 2]
        return dense_ffn(x1, wg, wu, wd, lw['ln2_g'], lw['ln2_b'], alpha)
    w_r, wg, wu, wd = ffn_w['moe'][l // 2]
    if x1.shape[0] >= MOE_MIN_ROUTED:
        return moe_routed_ffn(x1, w_r, wg, wu, wd, lw['ln2_g'], lw['ln2_b'], alpha)
    comb = router(x1, w_r)
    return moe_ffn(x1, comb, wg, wu, wd, lw['ln2_g'], lw['ln2_b'], alpha)


def kernel(x_prompt, x_sample, cache_k, cache_v, state_gla, state_ret, page_table, w_in, lam_q1, lam_k1, lam_q2, lam_k2, subln_w, w_gla_up, b_gla_up, gla_norm_w, ret_norm_w, w_pa, w_pb, w_pc, w_out, ln1_g, ln1_b, w_ff_gate, w_ff_up, w_ff_down, w_router, w_exp_gate, w_exp_up, w_exp_down, ln2_g, ln2_b):
    bp, tp, _ = x_prompt.shape
    bs, ts, _ = x_sample.shape
    assert bp == 1 and ts == 1
    depth = w_in.shape[0]
    alpha = (2 * depth) ** 0.25
    n_pool, page = cache_k.shape[1], cache_k.shape[2]
    past_len = page_table.shape[1] * page
    pos_p = jnp.arange(tp, dtype=jnp.int32)
    pos_s = jnp.full((bs,), past_len, jnp.int32)
    ffn_w = dict(
        dense=[(w_ff_gate[i].astype(BF16), w_ff_up[i].astype(BF16), w_ff_down[i].astype(BF16))
               for i in range(w_ff_gate.shape[0])],
        moe=[(w_router[i], w_exp_gate[i].astype(BF16), w_exp_up[i].astype(BF16), w_exp_down[i].astype(BF16))
             for i in range(w_router.shape[0])])
    ck = cache_k.reshape(depth, n_pool, page * A_HEADS, LANES)
    cv = cache_v.reshape(depth, n_pool, page * A_HEADS, LANES)
    lg_col = jnp.broadcast_to(jnp.repeat(_ret_log_gamma(), R_QK)[None, :], (bs, HK))

    yp = x_prompt.reshape(tp, D_MODEL)
    ys = x_sample.reshape(bs, D_MODEL)
    outs = {n: [] for n in ('gp', 'rp', 'ks', 'vs', 'gs', 'rs')}
    row_bufs = None
    cols = lambda a, blk, n: a[:, blk * LANES:(blk + n) * LANES]
    for l in range(depth):
        lw = _layer_weights(l, w_in, lam_q1, lam_k1, lam_q2, lam_k2, subln_w, w_gla_up, b_gla_up,
                            gla_norm_w, ret_norm_w, w_pa, w_pb, w_pc, w_out, ln1_g, ln1_b, ln2_g, ln2_b)
        h, lrg = _project(yp, lw)
        qa, ka, va, *row_bufs = rotary_attention(h, _rope_tables(pos_p), ROPE_DIM // 2,
                                                 A_QK ** -0.5 * LOG2E, l, depth, row_bufs)
        oa = flash_diff_attention(qa, ka, va, lw['lamvec'], lw['subln'], lw['lam_init'])
        og, s_gla = gla_prompt(h, lrg, lw['w_up'], lw['b_up'], lw['gla_nw'])
        orr, s_ret = ret_prompt(h, _ret_tables(pos_p), lw['ret_nw'])
        x1 = mixer_out(oa, og, orr, h, yp, lw['wpa'], lw['wpb'], lw['wpc'], lw['wo'],
                       lw['ln1_g'], lw['ln1_b'], alpha)
        yp = _channel_mix(x1, l, lw, ffn_w, alpha)
        outs['gp'].append(s_gla[None])
        outs['rp'].append(s_ret[None])
        h, lrg = _project(ys, lw)
        qr, kr = rotary(h, QR, HK, _ret_tables(pos_s), 1, 1.0, R_QK ** -0.5, F32)
        qa, ka = rotary(h, QA, A_HEADS * LANES, _rope_tables(pos_s), ROPE_DIM // 2, A_QK ** -0.5, 1.0, F32)
        va = cols(h, VA, 4)
        oa = decode_diff_attention(qa, ka, va, ck, cv, l, page_table, lw['lamvec'], lw['subln'],
                                   lw['lam_init'])
        gk = matmul(lrg, lw['w_up'], bias=lw['b_up'])
        og, s_gla = recurrent_step(state_gla[l], cols(h, QG, 2), cols(h, KG, 2), gk, cols(h, VG, 4),
                                   cols(h, RG, 4), lw['gla_nw'], True, G_QK ** -0.5)
        orr, s_ret = recurrent_step(state_ret[l], qr, kr, lg_col, cols(h, VR, 4), cols(h, GR, 4),
                                    lw['ret_nw'], False, 1.0)
        x1 = mixer_out(oa, og, orr, h, ys, lw['wpa'], lw['wpb'], lw['wpc'], lw['wo'],
                       lw['ln1_g'], lw['ln1_b'], alpha)
        ys = _channel_mix(x1, l, lw, ffn_w, alpha)
        outs['ks'].append(ka.reshape(bs, 1, A_HEADS, 2 * A_QK))
        outs['vs'].append(va.reshape(bs, 1, A_HEADS, 2 * A_QK))
        outs['gs'].append(s_gla)
        outs['rs'].append(s_ret)

    st = lambda n: jnp.stack(outs[n])
    kv_shape = (depth, bp, tp, A_HEADS, 2 * A_QK)
    return (yp.reshape(bp, tp, D_MODEL), ys.reshape(bs, ts, D_MODEL), row_bufs[0].reshape(kv_shape),
            row_bufs[1].reshape(kv_shape), st('gp'),
            st('rp'), st('ks'), st('vs'), st('gs'), st('rs'))
```
